```python
import math
import jax, jax.numpy as jnp
from jax import lax
import numpy as np

D_MODEL = 1024
BATCH = 16
SEQ = 256
DEPTH = 2
DEC_BATCH = 8
DEC_SEQ = 4096
PAST_LEN = 256

GRID_W = 64
N_EVEN = (DEPTH + 1) // 2
N_ODD = DEPTH // 2
N_MOD = 6
EPS = 1e-6
CHUNK = 64
CONV_CH = D_MODEL // 2
CONV_W = 31
RET_HEADS = 4
RET_DK = (D_MODEL // 2) // RET_HEADS
RET_DV = RET_DK
RET_W = RET_HEADS * RET_DV
ROPE_BASE = 10000.0
EVEN_IN = 2 * CONV_CH + 4 * RET_W
GDN_HEADS = 8
GDN_DK = D_MODEL // GDN_HEADS
GDN_DV = GDN_DK
GDN_KW = GDN_HEADS * GDN_DK
GDN_VW = GDN_HEADS * GDN_DV
SHORT_W = 5
ODD_IN = 2 * GDN_KW + 2 * GDN_VW + 4 * GDN_HEADS
N_EXPERTS = 16
N_GROUPS = 4
EXP_PER_GROUP = N_EXPERTS // N_GROUPS
TOP_K = 2
D_EXPERT = 512

kernel_name = 'hybrid_diffusion_prefix_trunk_step'


def rms_norm(x, g):
    xf = x.astype(jnp.float32)
    y = xf * lax.rsqrt(jnp.mean(xf * xf, axis=-1, keepdims=True) + EPS)
    return (y * g.astype(jnp.float32)).astype(x.dtype)


def layer_norm(x, g, b):
    xf = x.astype(jnp.float32)
    xc = xf - jnp.mean(xf, axis=-1, keepdims=True)
    y = xc * lax.rsqrt(jnp.mean(xc * xc, axis=-1, keepdims=True) + EPS)
    return (y * g.astype(jnp.float32) + b.astype(jnp.float32)).astype(x.dtype)


def l2_norm(x):
    return x * lax.rsqrt(jnp.sum(x * x, axis=-1, keepdims=True) + EPS)


def dw_conv(x, w):
    pad = w.shape[0] // 2
    return lax.conv_general_dilated(x, w[:, None, :].astype(x.dtype), window_strides=(1,),
                                    padding=[(pad, pad)], dimension_numbers=('NWC', 'WIO', 'NWC'),
                                    feature_group_count=x.shape[-1])


def split_heads(t, n_heads):
    b, l, w = t.shape
    return t.reshape(b, l, n_heads, w // n_heads).transpose(0, 2, 1, 3)


def flip_seq(t):
    return jnp.flip(t, axis=2)


def modulation(cond, w, b):
    m = jax.nn.silu(cond) @ w + b
    return jnp.split(m[..., None, :], N_MOD, axis=-1)


def axial_rope(x):
    l, dk = x.shape[2], x.shape[3]
    rows = l // GRID_W
    r = jnp.repeat(jnp.arange(rows, dtype=jnp.float32), GRID_W)
    col = jnp.tile(jnp.arange(GRID_W, dtype=jnp.float32), rows)
    half = dk // 2
    quarter = half // 2
    inv = ROPE_BASE ** (-jnp.arange(quarter, dtype=jnp.float32) / quarter)
    ang = jnp.concatenate([r[:, None] * inv, col[:, None] * inv], axis=-1)
    cos, sin = jnp.cos(ang), jnp.sin(ang)
    x1, x2 = x[..., :half], x[..., half:]
    return jnp.concatenate([x1 * cos - x2 * sin, x1 * sin + x2 * cos], axis=-1)


def to_chunks(t):
    b, h, l = t.shape[:3]
    return jnp.moveaxis(t.astype(jnp.float32).reshape(b, h, l // CHUNK, CHUNK, *t.shape[3:]), 2, 0)


def from_chunks(o):
    n, b, h, c, d = o.shape
    return jnp.moveaxis(o, 0, 2).reshape(b, h, n * c, d)


def retention_scan(q, k, v, log_gamma, s0):
    qc, kc, vc = to_chunks(q), to_chunks(k), to_chunks(v)
    idx = jnp.arange(CHUNK, dtype=jnp.float32)
    lg = log_gamma.astype(jnp.float32)[:, None]
    diff = idx[:, None] - idx[None, :]
    dmask = jnp.where(diff >= 0, jnp.exp(lg[:, :, None] * jnp.maximum(diff, 0.0)), 0.0)
    q_decay = jnp.exp(lg * (idx + 1.0))[:, :, None]
    k_decay = jnp.exp(lg * (CHUNK - 1.0 - idx))[:, :, None]
    chunk_decay = jnp.exp(lg * CHUNK)[:, :, None]

    def step(s, inp):
        qi, ki, vi = inp
        scores = jnp.einsum('bhid,bhjd->bhij', qi, ki) * dmask
        o = (jnp.einsum('bhij,bhjv->bhiv', scores, vi)
             + jnp.einsum('bhid,bhdv->bhiv', qi * q_decay, s))
        s = chunk_decay * s + jnp.einsum('bhjd,bhjv->bhdv', ki * k_decay, vi)
        return s, o

    s, o = lax.scan(step, s0.astype(jnp.float32), (qc, kc, vc))
    return from_chunks(o), s


def gated_delta_scan(q, k, v, g, beta, s0):
    qc, kc, vc, gc, bc = to_chunks(q), to_chunks(k), to_chunks(v), to_chunks(g), to_chunks(beta)
    tril = jnp.tril(jnp.ones((CHUNK, CHUNK), dtype=bool))
    strict = jnp.tril(jnp.ones((CHUNK, CHUNK), dtype=bool), -1)
    eye = jnp.eye(CHUNK, dtype=jnp.float32)
    dv = v.shape[-1]

    def step(s, inp):
        qi, ki, vi, gi, bi = inp
        G = jnp.cumsum(gi, axis=-1)
        decay = jnp.exp(jnp.where(tril, G[..., :, None] - G[..., None, :], -jnp.inf))
        kb = ki * bi[..., None]
        a = jnp.where(strict, jnp.einsum('bhid,bhjd->bhij', kb, ki) * decay, 0.0)
        rhs = jnp.concatenate([vi * bi[..., None], kb * jnp.exp(G)[..., None]], axis=-1)
        sol = lax.linalg.triangular_solve(eye + a, rhs, left_side=True, lower=True, unit_diagonal=True)
        u, w = sol[..., :dv], sol[..., dv:]
        v_new = u - jnp.einsum('bhik,bhkv->bhiv', w, s)
        attn = jnp.einsum('bhid,bhjd->bhij', qi, ki) * decay
        o = (jnp.einsum('bhid,bhdv->bhiv', qi * jnp.exp(G)[..., None], s)
             + jnp.einsum('bhij,bhjv->bhiv', attn, v_new))
        g_last = G[..., -1:]
        s = (s * jnp.exp(g_last)[..., None]
             + jnp.einsum('bhjd,bhjv->bhdv', ki * jnp.exp(g_last - G)[..., None], v_new))
        return s, o

    s, o = lax.scan(step, s0.astype(jnp.float32), (qc, kc, vc, gc, bc))
    return from_chunks(o), s


def even_mixer(h, w_in, conv_w, ln_g, ln_b, decay_p, w_out, s0, latent):
    b, l, _ = h.shape
    proj = h @ w_in
    glu_v, glu_g, q, k, v, gt = jnp.split(
        proj, [CONV_CH, 2 * CONV_CH, 2 * CONV_CH + RET_W, 2 * CONV_CH + 2 * RET_W, 2 * CONV_CH + 3 * RET_W], axis=-1)
    a = dw_conv(glu_v * jax.nn.sigmoid(glu_g), conv_w)
    conv_out = jax.nn.silu(layer_norm(a, ln_g, ln_b))
    q = split_heads(q, RET_HEADS).astype(jnp.float32)
    k = split_heads(k, RET_HEADS).astype(jnp.float32)
    v = split_heads(v, RET_HEADS).astype(jnp.float32)
    if latent:
        q, k = axial_rope(q), axial_rope(k)
    k = k * (RET_DK ** -0.5)
    log_gamma = -jnp.exp(decay_p.astype(jnp.float32))
    o_f, s_f = retention_scan(q, k, v, log_gamma[0], s0[:, 0])
    o_b, s_b = retention_scan(flip_seq(q), flip_seq(k), flip_seq(v), log_gamma[1], s0[:, 1])
    o = o_f + flip_seq(o_b)
    oc = o - jnp.mean(o, axis=-1, keepdims=True)
    o = oc * lax.rsqrt(jnp.mean(oc * oc, axis=-1, keepdims=True) + EPS)
    ret_out = (o.transpose(0, 2, 1, 3).reshape(b, l, RET_W) * jax.nn.silu(gt.astype(jnp.float32))).astype(h.dtype)
    out = jnp.concatenate([conv_out, ret_out], axis=-1) @ w_out
    return out, jnp.stack([s_f, s_b], axis=1)


def odd_mixer(h, w_in, conv_w, a_log, dt_bias, norm_w, w_out, s0):
    b, l, _ = h.shape
    proj = h @ w_in
    qkv, z, a, bt = jnp.split(proj, [2 * GDN_KW + GDN_VW, 2 * GDN_KW + 2 * GDN_VW,
                                     2 * GDN_KW + 2 * GDN_VW + 2 * GDN_HEADS], axis=-1)
    qkv = jax.nn.silu(dw_conv(qkv, conv_w))
    q, k, v = jnp.split(qkv, [GDN_KW, 2 * GDN_KW], axis=-1)
    q = l2_norm(split_heads(q, GDN_HEADS).astype(jnp.float32)) * (GDN_DK ** -0.5)
    k = l2_norm(split_heads(k, GDN_HEADS).astype(jnp.float32))
    v = split_heads(v, GDN_HEADS).astype(jnp.float32)
    a = a.astype(jnp.float32).reshape(b, l, 2, GDN_HEADS)
    bt = bt.astype(jnp.float32).reshape(b, l, 2, GDN_HEADS)
    g = -jnp.exp(a_log.astype(jnp.float32)) * jax.nn.softplus(a + dt_bias.astype(jnp.float32))
    g = g.transpose(2, 0, 3, 1)
    beta = jax.nn.sigmoid(bt).transpose(2, 0, 3, 1)
    o_f, s_f = gated_delta_scan(q, k, v, g[0], beta[0], s0[:, 0])
    o_b, s_b = gated_delta_scan(flip_seq(q), flip_seq(k), flip_seq(v), flip_seq(g[1]), flip_seq(beta[1]), s0[:, 1])
    o = (o_f + flip_seq(o_b)).transpose(0, 2, 1, 3)
    zg = jax.nn.silu(z.astype(jnp.float32)).reshape(b, l, GDN_HEADS, GDN_DV)
    o = rms_norm(o, norm_w) * zg
    out = o.reshape(b, l, GDN_VW).astype(h.dtype) @ w_out
    return out, jnp.stack([s_f, s_b], axis=1)


def moe(h, router_w, router_bias, w_gate, w_up, w_down):
    b, l, d = h.shape
    t = h.reshape(b * l, d)
    scores = jax.nn.sigmoid((t @ router_w).astype(jnp.float32))
    sel = scores + router_bias.astype(jnp.float32)
    grp = sel.reshape(-1, N_GROUPS, EXP_PER_GROUP)
    grp_score = jnp.sum(lax.top_k(grp, TOP_K)[0], axis=-1)
    best = jnp.argmax(grp_score, axis=-1)
    in_grp = jnp.take_along_axis(grp, best[:, None, None], axis=1)[:, 0]
    _, local = lax.top_k(in_grp, TOP_K)
    idx = best[:, None] * EXP_PER_GROUP + local
    wts = jnp.take_along_axis(scores, idx, axis=-1)
    wts = wts / jnp.sum(wts, axis=-1, keepdims=True)
    gate = jnp.sum(jax.nn.one_hot(idx, N_EXPERTS, dtype=jnp.float32) * wts[..., None], axis=1).astype(t.dtype)
    y = jnp.zeros_like(t)
    for e in range(N_EXPERTS):
        he = jax.nn.silu(t @ w_gate[e]) * (t @ w_up[e])
        y = y + gate[:, e:e + 1] * (he @ w_down[e])
    return y.reshape(b, l, d)


def trunk(x, cond, ret_init, gdn_init, latent, params):
    (w_mod, b_mod, n_mix_pre, n_mix_post, n_ffn_pre, n_ffn_post,
     ev_w_in, ev_conv_w, ev_ln_g, ev_ln_b, ev_decay, ev_w_out,
     od_w_in, od_conv_w, od_a_log, od_dt_bias, od_norm_w, od_w_out,
     router_w, router_bias, w_gate, w_up, w_down) = params
    ret_states, gdn_states = [], []
    for layer in range(DEPTH):
        sh1, sc1, g1, sh2, sc2, g2 = modulation(cond, w_mod[layer], b_mod[layer])
        h = rms_norm(x, n_mix_pre[layer]) * (1.0 + sc1) + sh1
        i = layer // 2
        if layer % 2 == 0:
            out, st = even_mixer(h, ev_w_in[i], ev_conv_w[i], ev_ln_g[i], ev_ln_b[i], ev_decay[i], ev_w_out[i],
                                 ret_init[:, i], latent)
            ret_states.append(st)
        else:
            out, st = odd_mixer(h, od_w_in[i], od_conv_w[i], od_a_log[i], od_dt_bias[i], od_norm_w[i], od_w_out[i],
                                gdn_init[:, i])
            gdn_states.append(st)
        x = x + g1 * rms_norm(out, n_mix_post[layer])
        h = rms_norm(x, n_ffn_pre[layer]) * (1.0 + sc2) + sh2
        x = x + g2 * rms_norm(moe(h, router_w, router_bias, w_gate[layer], w_up[layer], w_down[layer]),
                              n_ffn_post[layer])
    new_ret = jnp.stack(ret_states, axis=1) if ret_states else ret_init
    new_gdn = jnp.stack(gdn_states, axis=1) if gdn_states else gdn_init
    return x, new_ret, new_gdn


def setup_inputs(seed: int = 0) -> dict:
    key = jax.random.key(seed)
    ks = iter(jax.random.split(key, 40))
    f32 = jnp.float32
    D = D_MODEL

    def nrm(shape, s):
        return jax.random.normal(next(ks), shape, f32) * s

    ret_base = -(5.0 + jnp.arange(RET_HEADS, dtype=f32)) * math.log(2.0)
    dt = jnp.exp(jax.random.uniform(next(ks), (N_ODD, 2, GDN_HEADS), f32, math.log(1e-3), math.log(1e-1)))
    return {
        'x_prompt': nrm((BATCH, SEQ, D), 1.0),
        'x_sample': nrm((DEC_BATCH, DEC_SEQ, D), 1.0),
        'state_ret': nrm((DEC_BATCH, N_EVEN, 2, RET_HEADS, RET_DK, RET_DV), 0.1),
        'state_gdn': nrm((DEC_BATCH, N_ODD, 2, GDN_HEADS, GDN_DK, GDN_DV), 0.1),
        'c': nrm((DEC_BATCH, D), 1.0),
        'c_ctx': nrm((D,), 1.0),
        'w_mod': nrm((DEPTH, D, N_MOD * D), 0.3 * D ** -0.5),
        'b_mod': nrm((DEPTH, N_MOD * D), 0.02),
        'norm_mix_pre': 1.0 + nrm((DEPTH, D), 0.02),
        'norm_mix_post': 1.0 + nrm((DEPTH, D), 0.02),
        'norm_ffn_pre': 1.0 + nrm((DEPTH, D), 0.02),
        'norm_ffn_post': 1.0 + nrm((DEPTH, D), 0.02),
        'ev_w_in': nrm((N_EVEN, D, EVEN_IN), D ** -0.5),
        'ev_conv_w': nrm((N_EVEN, CONV_W, CONV_CH), CONV_W ** -0.5),
        'ev_conv_ln_g': 1.0 + nrm((N_EVEN, CONV_CH), 0.02),
        'ev_conv_ln_b': nrm((N_EVEN, CONV_CH), 0.02),
        'ev_ret_decay': ret_base + nrm((N_EVEN, 2, RET_HEADS), 0.1),
        'ev_w_out': nrm((N_EVEN, CONV_CH + RET_W, D), (CONV_CH + RET_W) ** -0.5),
        'od_w_in': nrm((N_ODD, D, ODD_IN), D ** -0.5),
        'od_conv_w': nrm((N_ODD, SHORT_W, 2 * GDN_KW + GDN_VW), SHORT_W ** -0.5),
        'od_a_log': jnp.log(jax.random.uniform(next(ks), (N_ODD, 2, GDN_HEADS), f32, 1.0, 16.0)),
        'od_dt_bias': dt + jnp.log(-jnp.expm1(-dt)),
        'od_norm_w': 1.0 + nrm((N_ODD, GDN_DV), 0.02),
        'od_w_out': nrm((N_ODD, GDN_VW, D), GDN_VW ** -0.5),
        'router_w': nrm((D, N_EXPERTS), D ** -0.5),
        'router_bias': nrm((N_EXPERTS,), 0.01),
        'moe_w_gate': nrm((DEPTH, N_EXPERTS, D, D_EXPERT), D ** -0.5),
        'moe_w_up': nrm((DEPTH, N_EXPERTS, D, D_EXPERT), D ** -0.5),
        'moe_w_down': nrm((DEPTH, N_EXPERTS, D_EXPERT, D), D_EXPERT ** -0.5),
    }


def reference(x_prompt, x_sample, state_ret, state_gdn, c, c_ctx, w_mod, b_mod,
              norm_mix_pre, norm_mix_post, norm_ffn_pre, norm_ffn_post,
              ev_w_in, ev_conv_w, ev_conv_ln_g, ev_conv_ln_b, ev_ret_decay, ev_w_out,
              od_w_in, od_conv_w, od_a_log, od_dt_bias, od_norm_w, od_w_out,
              router_w, router_bias, moe_w_gate, moe_w_up, moe_w_down):
    params = (w_mod, b_mod, norm_mix_pre, norm_mix_post, norm_ffn_pre, norm_ffn_post,
              ev_w_in, ev_conv_w, ev_conv_ln_g, ev_conv_ln_b, ev_ret_decay, ev_w_out,
              od_w_in, od_conv_w, od_a_log, od_dt_bias, od_norm_w, od_w_out,
              router_w, router_bias, moe_w_gate, moe_w_up, moe_w_down)
    b_ctx = x_prompt.shape[0]
    ret0 = jnp.zeros((b_ctx, N_EVEN, 2, RET_HEADS, RET_DK, RET_DV), jnp.float32)
    gdn0 = jnp.zeros((b_ctx, N_ODD, 2, GDN_HEADS, GDN_DK, GDN_DV), jnp.float32)
    y_prompt, new_ret, new_gdn = trunk(x_prompt, c_ctx, ret0, gdn0, False, params)
    y_sample, _, _ = trunk(x_sample, c, state_ret, state_gdn, True, params)
    return (y_prompt, y_sample, new_ret.astype(x_prompt.dtype), new_gdn.astype(x_prompt.dtype))
```

```python
import functools
import math

import jax
import jax.numpy as jnp
import numpy as np
from jax import lax
from jax.experimental import pallas as pl
from jax.experimental.pallas import tpu as pltpu

F32 = jnp.float32
BF16 = jnp.bfloat16
I32 = jnp.int32

D_MODEL = 1024
N_MOD = 6
EPS = 1e-6
GRID_W = 64
CONV_CH = 512
CONV_W = 31
RET_HEADS = 4
RET_DK = 128
RET_W = 512
ROPE_BASE = 10000.0
EVEN_IN = 2 * CONV_CH + 4 * RET_W
GDN_HEADS = 8
GDN_DK = 128
GDN_KW = 1024
GDN_VW = 1024
SHORT_W = 5
N_EXPERTS = 16
N_GROUPS = 4
EXP_PER_GROUP = 4
D_EXPERT = 512

LANES = 128
TOK = 256
TM_PROJ = 512
TM_ROUTE = 512
TM_EXP = 512
GDN_CHUNK = 128
HALO = 16
COND_PAD = 16
VMEM_LIMIT = 56 * 1024 * 1024
NEG_BIG = -1e30


def _cparams(n_axes=1, vmem=VMEM_LIMIT):
    return pltpu.CompilerParams(dimension_semantics=("arbitrary",) * n_axes, vmem_limit_bytes=vmem)


def _silu(x):
    return x * jax.nn.sigmoid(x)


def _rms(x, g):
    return x * lax.rsqrt(jnp.mean(x * x, axis=-1, keepdims=True) + EPS) * g


def _dot(a, b):
    return jnp.dot(a, b, preferred_element_type=F32)


def _dot_nt(a, b):
    return lax.dot_general(a, b, (((1,), (1,)), ((), ())), preferred_element_type=F32)


def _dot_tn(a, b):
    return lax.dot_general(a, b, (((0,), (0,)), ((), ())), preferred_element_type=F32)


def _split3(x):
    x1 = x.astype(BF16)
    r = x - x1.astype(F32)
    x2 = r.astype(BF16)
    x3 = (r - x2.astype(F32)).astype(BF16)
    return x1, x2, x3


class _Layout:
    def __init__(self, bc, lc, bd, ld):
        self.bc, self.lc, self.bd, self.ld = bc, lc, bd, ld
        self.t_ctx = bc * lc
        self.t = bc * lc + bd * ld
        self.n_seq = bc + bd
        assert lc % TOK == 0 and ld % TOK == 0 and self.t_ctx % TM_PROJ == 0 and ld % TM_PROJ == 0
        assert self.t % TM_ROUTE == 0 and lc % GDN_CHUNK == 0 and ld % GDN_CHUNK == 0

    def seq_of_row(self, r):
        if r < self.t_ctx:
            return r // self.lc, r % self.lc, self.lc
        r2 = r - self.t_ctx
        return self.bc + r2 // self.ld, r2 % self.ld, self.ld

    def cond_rows(self, tile):
        out = []
        for i in range(self.t // tile):
            s, _, _ = self.seq_of_row(i * tile)
            out.append(0 if s < self.bc else 1 + s - self.bc)
        return np.asarray(out, np.int32)

    def edges(self, tile):
        left, right = [], []
        for i in range(self.t // tile):
            _, p, l = self.seq_of_row(i * tile)
            left.append(int(p == 0))
            right.append(int(p + tile == l))
        return np.asarray(left, np.int32), np.asarray(right, np.int32)

    def scan_schedule(self, chunk):
        fb, bb, sq, fi, la, rf, rb = [], [], [], [], [], [], []
        ident = self.ld // chunk
        for s in range(self.n_seq):
            if s < self.bc:
                base, n = s * self.lc // chunk, self.lc // chunk
            else:
                base, n = (self.t_ctx + (s - self.bc) * self.ld) // chunk, self.ld // chunk
            for c in range(n):
                fb.append(base + c)
                bb.append(base + n - 1 - c)
                sq.append(s)
                fi.append(int(c == 0))
                la.append(int(c == n - 1))
                rf.append(ident if s < self.bc else c)
                rb.append(ident if s < self.bc else n - 1 - c)
        return [np.asarray(a, np.int32) for a in (fb, bb, sq, fi, la, rf, rb)]


def _mod_kernel(c_ref, w_ref, b_ref, o_ref):
    s = _silu(c_ref[...])
    o_ref[0] = _dot(s.astype(BF16), w_ref[0].astype(BF16)) + b_ref[0]


def _modulation(cond, w_mod, b_mod):
    depth, d, n = w_mod.shape
    nt = n // d
    return pl.pallas_call(
        _mod_kernel,
        grid=(depth, nt),
        in_specs=[pl.BlockSpec((COND_PAD, d), lambda l, j: (0, 0)),
                  pl.BlockSpec((1, d, d), lambda l, j: (l, 0, j)),
                  pl.BlockSpec((1, 1, d), lambda l, j: (l, 0, j))],
        out_specs=pl.BlockSpec((1, COND_PAD, d), lambda l, j: (l, 0, j)),
        out_shape=jax.ShapeDtypeStruct((depth, COND_PAD, n), F32),
        compiler_params=_cparams(2),
        name="modulation",
    )(cond, w_mod, b_mod.reshape(depth, 1, n))


def _inproj_kernel(cr_ref, x_ref, g_ref, sc_ref, sh_ref, *refs):
    n = len(refs) // 2
    h = _rms(x_ref[...], g_ref[...]) * (1.0 + sc_ref[0]) + sh_ref[0]
    hb = h.astype(BF16)
    for w_ref, o_ref in zip(refs[:n], refs[n:]):
        o_ref[...] = _dot(hb, w_ref[...]).astype(o_ref.dtype)


def _inproj(lay, x, g, sc, sh, weights, out_dtypes):
    t, d = x.shape
    cr = lay.cond_rows(TM_PROJ)
    row = lambda i, cr: (i, 0)
    const = lambda i, cr: (0, 0)
    cond = lambda i, cr: (cr[i], 0, 0)
    in_specs = [pl.BlockSpec((TM_PROJ, d), row), pl.BlockSpec((1, d), const),
                pl.BlockSpec((1, 1, d), cond), pl.BlockSpec((1, 1, d), cond)]
    in_specs += [pl.BlockSpec(w.shape, const) for w in weights]
    out_specs = [pl.BlockSpec((TM_PROJ, w.shape[1]), row) for w in weights]
    out_shape = [jax.ShapeDtypeStruct((t, w.shape[1]), dt) for w, dt in zip(weights, out_dtypes)]
    return pl.pallas_call(
        _inproj_kernel,
        grid_spec=pltpu.PrefetchScalarGridSpec(num_scalar_prefetch=1, grid=(t // TM_PROJ,),
                                               in_specs=in_specs, out_specs=out_specs),
        out_shape=out_shape,
        compiler_params=_cparams(1),
        name="inproj",
    )(cr, x, g.reshape(1, d), sc, sh, *weights)


def _rope(x, c, s):
    return x * c + pltpu.roll(x, RET_DK // 2, 1) * s


def _ret_kernel(fb_ref, bb_ref, sq_ref, fi_ref, la_ref, rf_ref, rb_ref,
                lg_ref, qf_ref, kf_ref, vf_ref, qb_ref, kb_ref, vb_ref,
                cf_ref, sf_ref, cb_ref, sb_ref, s0_ref,
                of_ref, ob_ref, sout_ref,
                st_ref, dm_ref, dec_ref):
    g = pl.program_id(0)
    c = TOK

    @pl.when(g == 0)
    def _():
        ii = lax.broadcasted_iota(I32, (c, c), 0)
        jj = lax.broadcasted_iota(I32, (c, c), 1)
        diff = (ii - jj).astype(F32)
        ri = lax.broadcasted_iota(I32, (c, RET_DK), 0).astype(F32)
        for h in range(RET_HEADS):
            lf = lg_ref[0, h]
            lb = lg_ref[1, h]
            low = jnp.exp(lf * jnp.maximum(diff, 0.0))
            up = jnp.exp(lb * jnp.maximum(-diff, 0.0))
            dm_ref[h] = jnp.where(diff > 0, low, jnp.where(diff < 0, up, 2.0))
            dec_ref[0, h] = jnp.exp(lf * (ri + 1.0))
            dec_ref[1, h] = jnp.exp(lf * (c - 1.0 - ri))
            dec_ref[2, h] = jnp.exp(lb * (c - ri))
            dec_ref[3, h] = jnp.exp(lb * ri)

    @pl.when(fi_ref[g] == 1)
    def _():
        st_ref[...] = s0_ref[0]

    scale = RET_DK ** -0.5
    cf, sf, cb, sb = cf_ref[...], sf_ref[...], cb_ref[...], sb_ref[...]
    zero_row = jnp.zeros((1, RET_DK), F32)
    for h in range(RET_HEADS):
        sl = slice(h * RET_DK, (h + 1) * RET_DK)
        lf = lg_ref[0, h]
        lb = lg_ref[1, h]
        q = _rope(qf_ref[:, sl].astype(F32), cf, sf)
        k = _rope(kf_ref[:, sl].astype(F32), cf, sf) * scale
        v = vf_ref[:, sl]
        s = _dot_nt(q.astype(BF16), k.astype(BF16)) * dm_ref[h]
        st = st_ref[0, h]
        o = _dot(s.astype(BF16), v) + _dot((q * dec_ref[0, h]).astype(BF16), st.astype(BF16))
        of_ref[:, sl] = o
        st_ref[0, h] = jnp.exp(zero_row + lf * c) * st + _dot_tn((k * dec_ref[1, h]).astype(BF16), v)
        q = _rope(qb_ref[:, sl].astype(F32), cb, sb)
        k = _rope(kb_ref[:, sl].astype(F32), cb, sb) * scale
        v = vb_ref[:, sl]
        st = st_ref[1, h]
        ob_ref[:, sl] = _dot((q * dec_ref[2, h]).astype(BF16), st.astype(BF16))
        st_ref[1, h] = jnp.exp(zero_row + lb * c) * st + _dot_tn((k * dec_ref[3, h]).astype(BF16), v)

    @pl.when(la_ref[g] == 1)
    def _():
        sout_ref[0] = st_ref[...]


def _retention(lay, proj, log_gamma, rope_c, rope_s, s0):
    t = proj.shape[0]
    sched = lay.scan_schedule(TOK)
    qcol, kcol, vcol = (2 * CONV_CH) // RET_W, (2 * CONV_CH) // RET_W + 1, (2 * CONV_CH) // RET_W + 2

    def tok(which, col):
        return pl.BlockSpec((TOK, RET_W), lambda g, *m: (m[which][g], col))

    def rope(which):
        return pl.BlockSpec((TOK, RET_DK), lambda g, *m: (m[which][g], 0))

    state = pl.BlockSpec((1, 2, RET_HEADS, RET_DK, RET_DK), lambda g, *m: (m[2][g], 0, 0, 0, 0))
    in_specs = [pl.BlockSpec(memory_space=pltpu.SMEM),
                tok(0, qcol), tok(0, kcol), tok(0, vcol), tok(1, qcol), tok(1, kcol), tok(1, vcol),
                rope(5), rope(5), rope(6), rope(6), state]
    out_specs = [pl.BlockSpec((TOK, RET_W), lambda g, *m: (m[0][g], 0)),
                 pl.BlockSpec((TOK, RET_W), lambda g, *m: (m[1][g], 0)), state]
    out_shape = [jax.ShapeDtypeStruct((t, RET_W), F32), jax.ShapeDtypeStruct((t, RET_W), F32),
                 jax.ShapeDtypeStruct(s0.shape, F32)]
    scratch = [pltpu.VMEM((2, RET_HEADS, RET_DK, RET_DK), F32),
               pltpu.VMEM((RET_HEADS, TOK, TOK), F32),
               pltpu.VMEM((4, RET_HEADS, TOK, RET_DK), F32)]
    return pl.pallas_call(
        _ret_kernel,
        grid_spec=pltpu.PrefetchScalarGridSpec(num_scalar_prefetch=7, grid=(len(sched[0]),),
                                               in_specs=in_specs, out_specs=out_specs,
                                               scratch_shapes=scratch),
        out_shape=out_shape,
        compiler_params=_cparams(1),
        name="retention",
    )(*sched, log_gamma, proj, proj, proj, proj, proj, proj, rope_c, rope_s, rope_c, rope_s, s0)


def _even_post_kernel(cr_ref, le_ref, re_ref,
                      glu_ref, prev_ref, next_ref, gt_ref, of_ref, ob_ref, x_ref,
                      cw_ref, lng_ref, lnb_ref, wout_ref, npost_ref, g1_ref,
                      out_ref, buf_ref):
    i = pl.program_id(0)

    def glu(r):
        r = r.astype(F32)
        return r[:, :CONV_CH] * jax.nn.sigmoid(r[:, CONV_CH:])

    keep_l = jnp.where(le_ref[i] == 1, 0.0, 1.0)
    keep_r = jnp.where(re_ref[i] == 1, 0.0, 1.0)
    buf_ref[0:HALO, :] = glu(prev_ref[...]) * keep_l
    buf_ref[HALO:HALO + TOK, :] = glu(glu_ref[...])
    buf_ref[HALO + TOK:, :] = glu(next_ref[...]) * keep_r
    acc = jnp.zeros((TOK, CONV_CH), F32)
    off = HALO - CONV_W // 2
    for j in range(CONV_W):
        acc = acc + buf_ref[off + j:off + j + TOK, :] * cw_ref[j:j + 1, :]
    mu = jnp.mean(acc, axis=-1, keepdims=True)
    ac = acc - mu
    y = ac * lax.rsqrt(jnp.mean(ac * ac, axis=-1, keepdims=True) + EPS) * lng_ref[...] + lnb_ref[...]
    conv_out = _silu(y)

    o = of_ref[...] + ob_ref[...]
    gt = gt_ref[...].astype(F32)
    parts = []
    for h in range(RET_HEADS):
        sl = slice(h * RET_DK, (h + 1) * RET_DK)
        oh = o[:, sl]
        oc = oh - jnp.mean(oh, axis=-1, keepdims=True)
        on = oc * lax.rsqrt(jnp.mean(oc * oc, axis=-1, keepdims=True) + EPS)
        parts.append(on * _silu(gt[:, sl]))
    ret_out = jnp.concatenate(parts, axis=1)

    out = _dot(conv_out.astype(BF16), wout_ref[0:CONV_CH, :]) + _dot(ret_out.astype(BF16), wout_ref[CONV_CH:, :])
    out_ref[...] = x_ref[...] + g1_ref[0] * _rms(out, npost_ref[...])


def _even_post(lay, proj, o_f, o_b, x, conv_w, ln_g, ln_b, w_out, n_post, g1):
    t, d = x.shape
    cr = lay.cond_rows(TOK)
    le, re = lay.edges(TOK)
    hb = TOK // HALO
    n_halo = t // HALO
    row = lambda i, *m: (i, 0)
    const = lambda i, *m: (0, 0)
    in_specs = [pl.BlockSpec((TOK, 2 * CONV_CH), row),
                pl.BlockSpec((HALO, 2 * CONV_CH), lambda i, *m: (jnp.maximum(i * hb - 1, 0), 0)),
                pl.BlockSpec((HALO, 2 * CONV_CH), lambda i, *m: (jnp.minimum((i + 1) * hb, n_halo - 1), 0)),
                pl.BlockSpec((TOK, RET_W), lambda i, *m: (i, EVEN_IN // RET_W - 1)),
                pl.BlockSpec((TOK, RET_W), row), pl.BlockSpec((TOK, RET_W), row),
                pl.BlockSpec((TOK, d), row),
                pl.BlockSpec(conv_w.shape, const), pl.BlockSpec((1, CONV_CH), const), pl.BlockSpec((1, CONV_CH), const),
                pl.BlockSpec(w_out.shape, const), pl.BlockSpec((1, d), const),
                pl.BlockSpec((1, 1, d), lambda i, *m: (m[0][i], 0, 0))]
    return pl.pallas_call(
        _even_post_kernel,
        grid_spec=pltpu.PrefetchScalarGridSpec(num_scalar_prefetch=3, grid=(t // TOK,),
                                               in_specs=in_specs, out_specs=pl.BlockSpec((TOK, d), row),
                                               scratch_shapes=[pltpu.VMEM((TOK + 2 * HALO, CONV_CH), F32)]),
        out_shape=jax.ShapeDtypeStruct((t, d), F32),
        compiler_params=_cparams(1),
        name="even_post",
    )(cr, le, re, proj, proj, proj, proj, o_f, o_b, x, conv_w, ln_g.reshape(1, -1), ln_b.reshape(1, -1),
      w_out, n_post.reshape(1, d), g1)


def _odd_pre_kernel(le_ref, re_ref, qkv_ref, prev_ref, next_ref, ab_ref, cw_ref, alog_ref, dtb_ref,
                    qkv_out_ref, gb_out_ref, buf_ref):
    i = pl.program_id(0)
    keep_l = jnp.where(le_ref[i] == 1, 0.0, 1.0)
    keep_r = jnp.where(re_ref[i] == 1, 0.0, 1.0)
    buf_ref[0:HALO, :] = prev_ref[...].astype(F32) * keep_l
    buf_ref[HALO:HALO + TOK, :] = qkv_ref[...].astype(F32)
    buf_ref[HALO + TOK:, :] = next_ref[...].astype(F32) * keep_r
    off = HALO - SHORT_W // 2
    for blk in range(3 * GDN_HEADS):
        sl = slice(blk * GDN_DK, (blk + 1) * GDN_DK)
        acc = jnp.zeros((TOK, GDN_DK), F32)
        for j in range(SHORT_W):
            acc = acc + buf_ref[off + j:off + j + TOK, sl] * cw_ref[j:j + 1, sl]
        y = _silu(acc)
        if blk < 2 * GDN_HEADS:
            y = y * lax.rsqrt(jnp.sum(y * y, axis=-1, keepdims=True) + EPS)
            if blk < GDN_HEADS:
                y = y * (GDN_DK ** -0.5)
        qkv_out_ref[:, sl] = y.astype(qkv_out_ref.dtype)
    ab = ab_ref[...]
    z = ab + dtb_ref[...]
    softplus = jnp.maximum(z, 0.0) + jnp.log(1.0 + jnp.exp(-jnp.abs(z)))
    gate = -jnp.exp(alog_ref[...]) * softplus
    beta = jax.nn.sigmoid(ab)
    lane = lax.broadcasted_iota(I32, ab.shape, 1)
    gb_out_ref[...] = jnp.where(lane < 2 * GDN_HEADS, gate, beta)


def _odd_pre(lay, proj, ab, conv_w, alog_row, dtb_row):
    t = proj.shape[0]
    w = 2 * GDN_KW + GDN_VW
    le, re = lay.edges(TOK)
    hb = TOK // HALO
    n_halo = t // HALO
    row = lambda i, *m: (i, 0)
    const = lambda i, *m: (0, 0)
    in_specs = [pl.BlockSpec((TOK, w), row),
                pl.BlockSpec((HALO, w), lambda i, *m: (jnp.maximum(i * hb - 1, 0), 0)),
                pl.BlockSpec((HALO, w), lambda i, *m: (jnp.minimum((i + 1) * hb, n_halo - 1), 0)),
                pl.BlockSpec((TOK, LANES), row),
                pl.BlockSpec(conv_w.shape, const), pl.BlockSpec((1, LANES), const), pl.BlockSpec((1, LANES), const)]
    out_specs = [pl.BlockSpec((TOK, w), row), pl.BlockSpec((TOK, LANES), row)]
    return pl.pallas_call(
        _odd_pre_kernel,
        grid_spec=pltpu.PrefetchScalarGridSpec(num_scalar_prefetch=2, grid=(t // TOK,),
                                               in_specs=in_specs, out_specs=out_specs,
                                               scratch_shapes=[pltpu.VMEM((TOK + 2 * HALO, w), F32)]),
        out_shape=[jax.ShapeDtypeStruct((t, w), BF16), jax.ShapeDtypeStruct((t, LANES), F32)],
        compiler_params=_cparams(1),
        name="odd_pre",
    )(le, re, proj, proj, proj, ab, conv_w, alog_row, dtb_row)


_GDN_LEVELS = tuple(2 ** p for p in range(int(math.log2(GDN_CHUNK))))


def _gdn_masks(msk_ref, tri_ref):
    c = GDN_CHUNK
    ii = lax.broadcasted_iota(I32, (c, c), 0)
    jj = lax.broadcasted_iota(I32, (c, c), 1)
    one = jnp.ones((c, c), F32)
    zero = jnp.zeros((c, c), F32)
    for rev in (0, 1):
        a, b = (ii, jj) if rev == 0 else (jj, ii)
        base = rev * 9
        msk_ref[base + 0] = jnp.where(a >= b, one, zero)
        msk_ref[base + 1] = jnp.where(a > b, one, zero)
        for l, m in enumerate(_GDN_LEVELS):
            sh = int(math.log2(m))
            ab_, bb_ = a >> sh, b >> sh
            hit = ((ab_ & 1) == 1) & (bb_ == ab_ - 1)
            msk_ref[base + 2 + l] = jnp.where(hit, one, zero)
        tri_ref[rev] = jnp.where(a >= b, one, zero).astype(BF16)


def _gdn_direction(rev, q_ref, k_ref, v_ref, gb_ref, o_ref, st_ref, msk_ref, tri_ref):
    c = GDN_CHUNK
    base = rev * 9
    incl = msk_ref[base + 0]
    strict = msk_ref[base + 1]
    eye = incl - strict
    gb = gb_ref[...]
    g1, g2, g3 = _split3(gb)
    tri = tri_ref[rev]
    gcum = _dot(tri, g1) + _dot(tri, g2) + _dot(tri, g3)
    ones = jnp.ones((c, c), BF16)
    glast = _dot(ones, g1) + _dot(ones, g2) + _dot(ones, g3)
    e_all = jnp.exp(gcum)
    kdec_all = jnp.exp(glast - gcum)
    elast_all = jnp.exp(glast)
    for h in range(GDN_HEADS):
        p = rev * GDN_HEADS + h
        pb = 2 * GDN_HEADS + p
        sl = slice(h * GDN_DK, (h + 1) * GDN_DK)
        q = q_ref[:, sl]
        k = k_ref[:, sl]
        v = v_ref[:, sl].astype(F32)
        kf = k.astype(F32)
        gcb = jnp.broadcast_to(gcum[:, p:p + 1], (c, c))
        diff = gcb - gcb.T
        decay = jnp.exp(jnp.where(incl > 0, diff, NEG_BIG))
        beta = gb[:, pb:pb + 1]
        kb = kf * beta
        a = _dot_nt(kb.astype(BF16), k) * decay * strict
        tinv = eye - a * msk_ref[base + 2]
        for l in range(1, len(_GDN_LEVELS)):
            bt = _dot((a * msk_ref[base + 2 + l]).astype(BF16), tinv.astype(BF16))
            tinv = tinv - _dot(tinv.astype(BF16), bt.astype(BF16))
        rhs = jnp.concatenate([v * beta, kb * e_all[:, p:p + 1]], axis=1)
        sol = _dot(tinv.astype(BF16), rhs.astype(BF16))
        u, w = sol[:, :GDN_DK], sol[:, GDN_DK:]
        st = st_ref[rev, h]
        stb = st.astype(BF16)
        v_new = u - _dot(w.astype(BF16), stb)
        attn = _dot_nt(q, k) * decay
        qs = (q.astype(F32) * e_all[:, p:p + 1]).astype(BF16)
        o_ref[:, sl] = _dot(qs, stb) + _dot(attn.astype(BF16), v_new.astype(BF16))
        kd = (kf * kdec_all[:, p:p + 1]).astype(BF16)
        st_ref[rev, h] = st * elast_all[:, p:p + 1] + _dot_tn(kd, v_new.astype(BF16))


def _gdn_kernel(fb_ref, bb_ref, sq_ref, fi_ref, la_ref,
                qf_ref, kf_ref, vf_ref, gf_ref, qb_ref, kb_ref, vb_ref, gbk_ref, s0_ref,
                of_ref, ob_ref, sout_ref,
                st_ref, msk_ref, tri_ref):
    g = pl.program_id(0)

    @pl.when(g == 0)
    def _():
        _gdn_masks(msk_ref, tri_ref)

    @pl.when(fi_ref[g] == 1)
    def _():
        st_ref[...] = s0_ref[0]

    _gdn_direction(0, qf_ref, kf_ref, vf_ref, gf_ref, of_ref, st_ref, msk_ref, tri_ref)
    _gdn_direction(1, qb_ref, kb_ref, vb_ref, gbk_ref, ob_ref, st_ref, msk_ref, tri_ref)

    @pl.when(la_ref[g] == 1)
    def _():
        sout_ref[0] = st_ref[...]


def _gdn(lay, qkv, gb, s0):
    t = qkv.shape[0]
    c = GDN_CHUNK
    sched = lay.scan_schedule(c)[:5]

    def tok(which, col):
        return pl.BlockSpec((c, GDN_KW), lambda g, *m: (m[which][g], col))

    def gate(which):
        return pl.BlockSpec((c, LANES), lambda g, *m: (m[which][g], 0))

    state = pl.BlockSpec((1, 2, GDN_HEADS, GDN_DK, GDN_DK), lambda g, *m: (m[2][g], 0, 0, 0, 0))
    in_specs = [tok(0, 0), tok(0, 1), tok(0, 2), gate(0), tok(1, 0), tok(1, 1), tok(1, 2), gate(1), state]
    out_specs = [pl.BlockSpec((c, GDN_VW), lambda g, *m: (m[0][g], 0)),
                 pl.BlockSpec((c, GDN_VW), lambda g, *m: (m[1][g], 0)), state]
    out_shape = [jax.ShapeDtypeStruct((t, GDN_VW), F32), jax.ShapeDtypeStruct((t, GDN_VW), F32),
                 jax.ShapeDtypeStruct(s0.shape, F32)]
    scratch = [pltpu.VMEM((2, GDN_HEADS, GDN_DK, GDN_DK), F32),
               pltpu.VMEM((18, c, c), F32), pltpu.VMEM((2, c, c), BF16)]
    return pl.pallas_call(
        _gdn_kernel,
        grid_spec=pltpu.PrefetchScalarGridSpec(num_scalar_prefetch=5, grid=(len(sched[0]),),
                                               in_specs=in_specs, out_specs=out_specs,
                                               scratch_shapes=scratch),
        out_shape=out_shape,
        compiler_params=_cparams(1),
        name="gdn_scan",
    )(*sched, qkv, qkv, qkv, gb, qkv, qkv, qkv, gb, s0)


def _odd_post_kernel(cr_ref, of_ref, ob_ref, z_ref, x_ref, nw_ref, wout_ref, npost_ref, g1_ref, out_ref):
    o = of_ref[...] + ob_ref[...]
    z = z_ref[...].astype(F32)
    nw = nw_ref[...]
    parts = []
    for h in range(GDN_HEADS):
        sl = slice(h * GDN_DK, (h + 1) * GDN_DK)
        parts.append(_rms(o[:, sl], nw) * _silu(z[:, sl]))
    y = jnp.concatenate(parts, axis=1)
    out = _dot(y.astype(BF16), wout_ref[...])
    out_ref[...] = x_ref[...] + g1_ref[0] * _rms(out, npost_ref[...])


def _odd_post(lay, o_f, o_b, proj, x, norm_w, w_out, n_post, g1):
    t, d = x.shape
    cr = lay.cond_rows(TOK)
    row = lambda i, *m: (i, 0)
    const = lambda i, *m: (0, 0)
    in_specs = [pl.BlockSpec((TOK, GDN_VW), row), pl.BlockSpec((TOK, GDN_VW), row),
                pl.BlockSpec((TOK, GDN_VW), lambda i, *m: (i, 3)),
                pl.BlockSpec((TOK, d), row),
                pl.BlockSpec((1, GDN_DK), const), pl.BlockSpec(w_out.shape, const), pl.BlockSpec((1, d), const),
                pl.BlockSpec((1, 1, d), lambda i, *m: (m[0][i], 0, 0))]
    return pl.pallas_call(
        _odd_post_kernel,
        grid_spec=pltpu.PrefetchScalarGridSpec(num_scalar_prefetch=1, grid=(t // TOK,),
                                               in_specs=in_specs, out_specs=pl.BlockSpec((TOK, d), row)),
        out_shape=jax.ShapeDtypeStruct((t, d), F32),
        compiler_params=_cparams(1),
        name="odd_post",
    )(cr, o_f, o_b, proj, x, norm_w.reshape(1, -1), w_out, n_post.reshape(1, d), g1)


SUB = D_MODEL // LANES


def _store_token_major(ref, x):
    n = x.shape[0]
    for s in range(SUB):
        ref[pl.ds(s, n, stride=SUB), :] = x[:, s * LANES:(s + 1) * LANES]


def _load_token_major(ref, n):
    return jnp.concatenate([ref[pl.ds(s, n, stride=SUB), :] for s in range(SUB)], axis=1)


def _router_kernel(cr_ref, x_ref, g_ref, sc_ref, sh_ref, rwt_ref, rb_ref,
                   h_ref, ids_ref, wts_ref, rank_ref, cnt_ref,
                   base_ref, su_ref):
    i = pl.program_id(0)
    tm = TM_ROUTE

    @pl.when(i == 0)
    def _():
        base_ref[...] = jnp.zeros_like(base_ref)
        ii = lax.broadcasted_iota(I32, (tm, tm), 0)
        jj = lax.broadcasted_iota(I32, (tm, tm), 1)
        su_ref[...] = jnp.where(ii < jj, 1.0, 0.0).astype(BF16)

    h = _rms(x_ref[...], g_ref[...]) * (1.0 + sc_ref[0]) + sh_ref[0]
    hb = h.astype(BF16)
    _store_token_major(h_ref, hb.astype(F32))
    h1 = hb
    w1, w2, w3 = _split3(rwt_ref[...])
    logits = _dot_nt(w1, h1) + _dot_nt(w2, h1) + _dot_nt(w3, h1)
    score = jax.nn.sigmoid(logits)
    sel = score + rb_ref[...]

    def row(a, e):
        return a[e:e + 1, :]

    gsum = []
    for gi in range(N_GROUPS):
        a, b, c, d = (row(sel, gi * EXP_PER_GROUP + j) for j in range(EXP_PER_GROUP))
        hi1, lo1 = jnp.maximum(a, b), jnp.minimum(a, b)
        hi2, lo2 = jnp.maximum(c, d), jnp.minimum(c, d)
        gsum.append(jnp.maximum(hi1, hi2) + jnp.maximum(jnp.minimum(hi1, hi2), jnp.maximum(lo1, lo2)))
    best = jnp.zeros_like(gsum[0]).astype(I32)
    cur = gsum[0]
    for gi in range(1, N_GROUPS):
        upd = gsum[gi] > cur
        best = jnp.where(upd, gi, best)
        cur = jnp.where(upd, gsum[gi], cur)

    def pick(arr, j):
        out = row(arr, j)
        for gi in range(1, N_GROUPS):
            out = jnp.where(best == gi, row(arr, gi * EXP_PER_GROUP + j), out)
        return out

    vals = [pick(sel, j) for j in range(EXP_PER_GROUP)]
    scs = [pick(score, j) for j in range(EXP_PER_GROUP)]

    def argmax_first(vs):
        idx = jnp.zeros_like(best)
        m = vs[0]
        for j in range(1, EXP_PER_GROUP):
            upd = vs[j] > m
            idx = jnp.where(upd, j, idx)
            m = jnp.where(upd, vs[j], m)
        return idx

    i1 = argmax_first(vals)
    vals2 = [jnp.where(i1 == j, -jnp.inf, vals[j]) for j in range(EXP_PER_GROUP)]
    i2 = argmax_first(vals2)

    def take(vs, idx):
        out = vs[0]
        for j in range(1, EXP_PER_GROUP):
            out = jnp.where(idx == j, vs[j], out)
        return out

    s1, s2 = take(scs, i1), take(scs, i2)
    tot = s1 + s2
    e1 = best * EXP_PER_GROUP + i1
    e2 = best * EXP_PER_GROUP + i2
    ids_ref[0:1, :] = e1
    ids_ref[1:2, :] = e2
    wts_ref[0:1, :] = s1 / tot
    wts_ref[1:2, :] = s2 / tot

    erow = lax.broadcasted_iota(I32, (N_EXPERTS, tm), 0)
    m1 = jnp.where(erow == e1, 1.0, 0.0)
    m2 = jnp.where(erow == e2, 1.0, 0.0)
    both = m1 + m2
    before = _dot(both.astype(BF16), su_ref[...]) + base_ref[...]
    rank_ref[0:1, :] = jnp.sum(m1 * before, axis=0, keepdims=True).astype(I32)
    rank_ref[1:2, :] = jnp.sum(m2 * before, axis=0, keepdims=True).astype(I32)
    base_ref[...] = base_ref[...] + jnp.sum(both, axis=1, keepdims=True)
    cnt_ref[...] = jnp.broadcast_to(base_ref[...], cnt_ref.shape).astype(I32)


def _router(lay, x, g, sc, sh, rwt, rbias):
    t, d = x.shape
    cr = lay.cond_rows(TM_ROUTE)
    tm = TM_ROUTE
    row = lambda i, cr: (i, 0)
    col = lambda i, cr: (0, i)
    const = lambda i, cr: (0, 0)
    cond = lambda i, cr: (cr[i], 0, 0)
    in_specs = [pl.BlockSpec((tm, d), row), pl.BlockSpec((1, d), const),
                pl.BlockSpec((1, 1, d), cond), pl.BlockSpec((1, 1, d), cond),
                pl.BlockSpec((N_EXPERTS, d), const), pl.BlockSpec((N_EXPERTS, 1), const)]
    out_specs = [pl.BlockSpec((tm * SUB, LANES), row), pl.BlockSpec((2, tm), col), pl.BlockSpec((2, tm), col),
                 pl.BlockSpec((2, tm), col), pl.BlockSpec((N_EXPERTS, LANES), const)]
    out_shape = [jax.ShapeDtypeStruct((t * SUB, LANES), F32), jax.ShapeDtypeStruct((2, t), I32),
                 jax.ShapeDtypeStruct((2, t), F32), jax.ShapeDtypeStruct((2, t), I32),
                 jax.ShapeDtypeStruct((N_EXPERTS, LANES), I32)]
    return pl.pallas_call(
        _router_kernel,
        grid_spec=pltpu.PrefetchScalarGridSpec(num_scalar_prefetch=1, grid=(t // tm,),
                                               in_specs=in_specs, out_specs=out_specs,
                                               scratch_shapes=[pltpu.VMEM((N_EXPERTS, 1), F32),
                                                               pltpu.VMEM((tm, tm), BF16)]),
        out_shape=out_shape,
        compiler_params=_cparams(1),
        name="router",
    )(cr, x, g.reshape(1, d), sc, sh, rwt, rbias.reshape(N_EXPERTS, 1))


def _row_copy(src, dst, sem):
    return pltpu.make_async_copy(src, dst, sem)


def _dispatch_kernel(pos_ref, h_ref, init_ref, sorted_ref, sem):
    del init_ref
    n = TOK

    def issue(r, carry):
        src = h_ref.at[pl.ds(pl.multiple_of(r * SUB, SUB), SUB)]
        for k in range(2):
            dst = pl.multiple_of(pos_ref[0, 0, 2 * r + k] * SUB, SUB)
            _row_copy(src, sorted_ref.at[pl.ds(dst, SUB)], sem).start()
        return carry

    lax.fori_loop(0, n, issue, 0)
    for _ in range(2):
        _row_copy(h_ref, sorted_ref.at[pl.ds(0, n * SUB)], sem).wait()


def _dispatch(h, pos3, n_rows):
    t = h.shape[0] // SUB
    init = jnp.zeros((n_rows * SUB, LANES), h.dtype)
    return pl.pallas_call(
        _dispatch_kernel,
        grid=(t // TOK,),
        in_specs=[pl.BlockSpec((1, 1, 2 * TOK), lambda i: (i, 0, 0), memory_space=pltpu.SMEM),
                  pl.BlockSpec((TOK * SUB, LANES), lambda i: (i, 0)),
                  pl.BlockSpec(memory_space=pl.ANY)],
        out_specs=pl.BlockSpec(memory_space=pl.ANY),
        out_shape=jax.ShapeDtypeStruct(init.shape, init.dtype),
        scratch_shapes=[pltpu.SemaphoreType.DMA(())],
        input_output_aliases={2: 0},
        compiler_params=_cparams(1),
        name="dispatch",
    )(pos3, h, init)


def _expert_kernel(te_ref, tv_ref, h_ref, wg_ref, wu_ref, wd_ref, y_ref):
    i = pl.program_id(0)

    @pl.when(tv_ref[i] == 1)
    def _():
        hb = _load_token_major(h_ref, TM_EXP).astype(BF16)
        a = _dot(hb, wg_ref[0])
        b = _dot(hb, wu_ref[0])
        he = (_silu(a) * b).astype(BF16)
        _store_token_major(y_ref, _dot(he, wd_ref[0]))

    @pl.when(tv_ref[i] == 0)
    def _():
        y_ref[...] = jnp.zeros_like(y_ref)


def _experts(hs, tile_expert, tile_valid, wg, wu, wd):
    d = wg.shape[1]
    r = hs.shape[0] // SUB
    row = lambda i, te, tv: (i, 0)
    wsel = lambda i, te, tv: (te[i], 0, 0)
    return pl.pallas_call(
        _expert_kernel,
        grid_spec=pltpu.PrefetchScalarGridSpec(
            num_scalar_prefetch=2, grid=(r // TM_EXP,),
            in_specs=[pl.BlockSpec((TM_EXP * SUB, LANES), row),
                      pl.BlockSpec((1, d, D_EXPERT), wsel), pl.BlockSpec((1, d, D_EXPERT), wsel),
                      pl.BlockSpec((1, D_EXPERT, d), wsel)],
            out_specs=pl.BlockSpec((TM_EXP * SUB, LANES), row)),
        out_shape=jax.ShapeDtypeStruct(hs.shape, F32),
        compiler_params=_cparams(1),
        name="experts",
    )(tile_expert, tile_valid, hs, wg, wu, wd)


def _combine_kernel(cr_ref, pos_ref, ys_ref, wts_ref, x_ref, npost_ref, g2_ref, out_ref, buf_ref, sem):
    n = TOK

    def issue(r, carry):
        row = pl.multiple_of(r * SUB, SUB)
        for k in range(2):
            src = pl.multiple_of(pos_ref[0, 0, 2 * r + k] * SUB, SUB)
            _row_copy(ys_ref.at[pl.ds(src, SUB)], buf_ref.at[k, pl.ds(row, SUB)], sem).start()
        return carry

    lax.fori_loop(0, n, issue, 0)
    for slot in range(2):
        _row_copy(ys_ref.at[pl.ds(0, n * SUB)], buf_ref.at[slot], sem).wait()

    w = wts_ref[...]
    y = _load_token_major(buf_ref.at[0], n) * w[:, 0:1] + _load_token_major(buf_ref.at[1], n) * w[:, 1:2]
    out_ref[...] = x_ref[...] + g2_ref[0] * _rms(y, npost_ref[...])


def _combine(lay, ys3, pos3, wts, x, n_post, g2):
    t, d = x.shape
    cr = lay.cond_rows(TOK)
    row = lambda i, cr: (i, 0)
    const = lambda i, cr: (0, 0)
    in_specs = [pl.BlockSpec((1, 1, 2 * TOK), lambda i, cr: (i, 0, 0), memory_space=pltpu.SMEM),
                pl.BlockSpec(memory_space=pl.ANY),
                pl.BlockSpec((TOK, 2), row),
                pl.BlockSpec((TOK, d), row),
                pl.BlockSpec((1, d), const),
                pl.BlockSpec((1, 1, d), lambda i, cr: (cr[i], 0, 0))]
    return pl.pallas_call(
        _combine_kernel,
        grid_spec=pltpu.PrefetchScalarGridSpec(
            num_scalar_prefetch=1, grid=(t // TOK,), in_specs=in_specs,
            out_specs=pl.BlockSpec((TOK, d), row),
            scratch_shapes=[pltpu.VMEM((2, TOK * SUB, LANES), F32), pltpu.SemaphoreType.DMA(())]),
        out_shape=jax.ShapeDtypeStruct((t, d), F32),
        compiler_params=_cparams(1),
        name="combine",
    )(cr, pos3, ys3, wts, x, n_post.reshape(1, d), g2)


def _moe(lay, x, g_pre, sc, sh, g2, n_post, rwt, rbias, wg, wu, wd):
    t, d = x.shape
    h, ids, wts, rank, cnt = _router(lay, x, g_pre, sc, sh, rwt, rbias)
    counts = cnt[:, 0]
    padded = ((counts + TM_EXP - 1) // TM_EXP) * TM_EXP
    ends = jnp.cumsum(padded)
    offs = ends - padded
    eids = jnp.arange(N_EXPERTS, dtype=I32)[:, None, None]
    pos = jnp.sum(jnp.where(ids[None] == eids, offs[:, None, None], 0), axis=0) + rank
    pos3 = pos.T.reshape(t // TOK, 1, 2 * TOK).astype(I32)
    n_rows = 2 * t + N_EXPERTS * TM_EXP
    n_tiles = n_rows // TM_EXP
    starts = jnp.arange(n_tiles, dtype=I32) * TM_EXP
    tile_expert = jnp.minimum(jnp.sum((starts[:, None] >= ends[None, :]).astype(I32), axis=1), N_EXPERTS - 1)
    tile_valid = (starts < ends[-1]).astype(I32)
    hs = _dispatch(h, pos3, n_rows)
    ys = _experts(hs, tile_expert.astype(I32), tile_valid, wg, wu, wd)
    return _combine(lay, ys, pos3, wts.T, x, n_post, g2)


def _rope_tables(ld):
    rows = ld // GRID_W
    r = jnp.repeat(jnp.arange(rows, dtype=F32), GRID_W)
    col = jnp.tile(jnp.arange(GRID_W, dtype=F32), rows)
    quarter = RET_DK // 4
    inv = ROPE_BASE ** (-jnp.arange(quarter, dtype=F32) / quarter)
    ang = jnp.concatenate([r[:, None] * inv, col[:, None] * inv], axis=-1)
    cos, sin = jnp.cos(ang), jnp.sin(ang)
    c = jnp.concatenate([cos, cos], axis=-1)
    s = jnp.concatenate([-sin, sin], axis=-1)
    ident_c = jnp.ones((TOK, RET_DK), F32)
    ident_s = jnp.zeros((TOK, RET_DK), F32)
    return jnp.concatenate([c, ident_c], axis=0), jnp.concatenate([s, ident_s], axis=0)


def kernel(x_prompt, x_sample, state_ret, state_gdn, c, c_ctx, w_mod, b_mod, norm_mix_pre, norm_mix_post,
           norm_ffn_pre, norm_ffn_post, ev_w_in, ev_conv_w, ev_conv_ln_g, ev_conv_ln_b, ev_ret_decay, ev_w_out,
           od_w_in, od_conv_w, od_a_log, od_dt_bias, od_norm_w, od_w_out, router_w, router_bias,
           moe_w_gate, moe_w_up, moe_w_down):
    bc, lc, d = x_prompt.shape
    bd, ld, _ = x_sample.shape
    depth = w_mod.shape[0]
    lay = _Layout(bc, lc, bd, ld)
    t = lay.t

    x = jnp.concatenate([x_prompt.reshape(bc * lc, d), x_sample.reshape(bd * ld, d)], axis=0)
    cond = jnp.zeros((COND_PAD, d), F32).at[0].set(c_ctx).at[1:1 + bd].set(c)
    mod = _modulation(cond, w_mod, b_mod)
    mod = mod.reshape(depth, COND_PAD, N_MOD, 1, d).transpose(0, 2, 1, 3, 4)

    rope_c, rope_s = _rope_tables(ld)
    rwt = router_w.T
    ret_states, gdn_states = [], []
    for layer in range(depth):
        sh1, sc1, g1, sh2, sc2, g2 = (mod[layer, j] for j in range(N_MOD))
        i = layer // 2
        if layer % 2 == 0:
            (proj,) = _inproj(lay, x, norm_mix_pre[layer], sc1, sh1, [ev_w_in[i].astype(BF16)], [BF16])
            s0 = jnp.concatenate([jnp.zeros((bc,) + state_ret.shape[2:], F32), state_ret[:, i]], axis=0)
            log_gamma = -jnp.exp(ev_ret_decay[i].astype(F32))
            o_f, o_b, s_out = _retention(lay, proj, log_gamma, rope_c, rope_s, s0)
            ret_states.append(s_out[:bc])
            cw = jnp.zeros((32, CONV_CH), F32).at[:CONV_W].set(ev_conv_w[i])
            x = _even_post(lay, proj, o_f, o_b, x, cw, ev_conv_ln_g[i], ev_conv_ln_b[i],
                           ev_w_out[i].astype(BF16), norm_mix_post[layer], g1)
        else:
            w_in = od_w_in[i]
            n_main = 2 * GDN_KW + 2 * GDN_VW
            w_small = jnp.zeros((d, LANES), F32).at[:, :4 * GDN_HEADS].set(w_in[:, n_main:])
            proj, ab = _inproj(lay, x, norm_mix_pre[layer], sc1, sh1,
                               [w_in[:, :n_main].astype(BF16), w_small.astype(BF16)], [BF16, F32])
            cw = jnp.zeros((8, 2 * GDN_KW + GDN_VW), F32).at[:SHORT_W].set(od_conv_w[i])
            alog_row = jnp.zeros((1, LANES), F32).at[0, :2 * GDN_HEADS].set(od_a_log[i].reshape(-1))
            dtb_row = jnp.zeros((1, LANES), F32).at[0, :2 * GDN_HEADS].set(od_dt_bias[i].reshape(-1))
            qkv, gb = _odd_pre(lay, proj, ab, cw, alog_row, dtb_row)
            s0 = jnp.concatenate([jnp.zeros((bc,) + state_gdn.shape[2:], F32), state_gdn[:, i]], axis=0)
            o_f, o_b, s_out = _gdn(lay, qkv, gb, s0)
            gdn_states.append(s_out[:bc])
            x = _odd_post(lay, o_f, o_b, proj, x, od_norm_w[i], od_w_out[i].astype(BF16),
                          norm_mix_post[layer], g1)
        x = _moe(lay, x, norm_ffn_pre[layer], sc2, sh2, g2, norm_ffn_post[layer], rwt, router_bias,
                 moe_w_gate[layer].astype(BF16), moe_w_up[layer].astype(BF16), moe_w_down[layer].astype(BF16))

    y_prompt = x[:bc * lc].reshape(bc, lc, d)
    y_sample = x[bc * lc:].reshape(bd, ld, d)
    new_ret = jnp.stack(ret_states, axis=1)
    new_gdn = jnp.stack(gdn_states, axis=1)
    return y_prompt, y_sample, new_ret.astype(x_prompt.dtype), new_gdn.astype(x_prompt.dtype)
```

```python
import functools
import math

import jax
import jax.numpy as jnp
import numpy as np
from jax import lax
from jax.experimental import pallas as pl
from jax.experimental.pallas import tpu as pltpu

F32 = jnp.float32
BF16 = jnp.bfloat16
I32 = jnp.int32

D_MODEL = 1024
N_MOD = 6
EPS = 1e-6
GRID_W = 64
CONV_CH = 512
CONV_W = 31
RET_HEADS = 4
RET_DK = 128
RET_W = 512
ROPE_BASE = 10000.0
EVEN_IN = 2 * CONV_CH + 4 * RET_W
GDN_HEADS = 8
GDN_DK = 128
GDN_KW = 1024
GDN_VW = 1024
SHORT_W = 5
N_EXPERTS = 16
N_GROUPS = 4
EXP_PER_GROUP = 4
D_EXPERT = 512

LANES = 128
TOK = 256
TM_PROJ = 512
TM_ROUTE = 512
TM_EXP = 512
GDN_CHUNK = 128
HALO = 16
COND_PAD = 16
VMEM_LIMIT = 56 * 1024 * 1024
NEG_BIG = -1e30


def _cparams(n_axes=1, vmem=VMEM_LIMIT):
    return pltpu.CompilerParams(dimension_semantics=("arbitrary",) * n_axes, vmem_limit_bytes=vmem)


def _silu(x):
    return x * jax.nn.sigmoid(x)


def _rms(x, g):
    return x * lax.rsqrt(jnp.mean(x * x, axis=-1, keepdims=True) + EPS) * g


def _dot(a, b):
    return jnp.dot(a, b, preferred_element_type=F32)


def _dot_nt(a, b):
    return lax.dot_general(a, b, (((1,), (1,)), ((), ())), preferred_element_type=F32)


def _dot_tn(a, b):
    return lax.dot_general(a, b, (((0,), (0,)), ((), ())), preferred_element_type=F32)


def _split3(x):
    x1 = x.astype(BF16)
    r = x - x1.astype(F32)
    x2 = r.astype(BF16)
    x3 = (r - x2.astype(F32)).astype(BF16)
    return x1, x2, x3


class _Layout:
    def __init__(self, bc, lc, bd, ld):
        self.bc, self.lc, self.bd, self.ld = bc, lc, bd, ld
        self.t_ctx = bc * lc
        self.t = bc * lc + bd * ld
        self.n_seq = bc + bd
        assert lc % TOK == 0 and ld % TOK == 0 and self.t_ctx % TM_PROJ == 0 and ld % TM_PROJ == 0
        assert self.t % TM_ROUTE == 0 and lc % GDN_CHUNK == 0 and ld % GDN_CHUNK == 0

    def seq_of_row(self, r):
        if r < self.t_ctx:
            return r // self.lc, r % self.lc, self.lc
        r2 = r - self.t_ctx
        return self.bc + r2 // self.ld, r2 % self.ld, self.ld

    def cond_rows(self, tile):
        out = []
        for i in range(self.t // tile):
            s, _, _ = self.seq_of_row(i * tile)
            out.append(0 if s < self.bc else 1 + s - self.bc)
        return np.asarray(out, np.int32)

    def edges(self, tile):
        left, right = [], []
        for i in range(self.t // tile):
            _, p, l = self.seq_of_row(i * tile)
            left.append(int(p == 0))
            right.append(int(p + tile == l))
        return np.asarray(left, np.int32), np.asarray(right, np.int32)

    def scan_schedule(self, chunk):
        fb, bb, sq, fi, la, rf, rb = [], [], [], [], [], [], []
        ident = self.ld // chunk
        for s in range(self.n_seq):
            if s < self.bc:
                base, n = s * self.lc // chunk, self.lc // chunk
            else:
                base, n = (self.t_ctx + (s - self.bc) * self.ld) // chunk, self.ld // chunk
            for c in range(n):
                fb.append(base + c)
                bb.append(base + n - 1 - c)
                sq.append(s)
                fi.append(int(c == 0))
                la.append(int(c == n - 1))
                rf.append(ident if s < self.bc else c)
                rb.append(ident if s < self.bc else n - 1 - c)
        return [np.asarray(a, np.int32) for a in (fb, bb, sq, fi, la, rf, rb)]


def _mod_kernel(c_ref, w_ref, b_ref, o_ref):
    s = _silu(c_ref[...])
    o_ref[0] = _dot(s.astype(BF16), w_ref[0].astype(BF16)) + b_ref[0]


def _modulation(cond, w_mod, b_mod):
    depth, d, n = w_mod.shape
    nt = n // d
    return pl.pallas_call(
        _mod_kernel,
        grid=(depth, nt),
        in_specs=[pl.BlockSpec((COND_PAD, d), lambda l, j: (0, 0)),
                  pl.BlockSpec((1, d, d), lambda l, j: (l, 0, j)),
                  pl.BlockSpec((1, 1, d), lambda l, j: (l, 0, j))],
        out_specs=pl.BlockSpec((1, COND_PAD, d), lambda l, j: (l, 0, j)),
        out_shape=jax.ShapeDtypeStruct((depth, COND_PAD, n), F32),
        compiler_params=_cparams(2),
        name="modulation",
    )(cond, w_mod, b_mod.reshape(depth, 1, n))


def _inproj_kernel(cr_ref, x_ref, g_ref, sc_ref, sh_ref, *refs):
    n = len(refs) // 2
    h = _rms(x_ref[...], g_ref[...]) * (1.0 + sc_ref[0]) + sh_ref[0]
    hb = h.astype(BF16)
    for w_ref, o_ref in zip(refs[:n], refs[n:]):
        o_ref[...] = _dot(hb, w_ref[...]).astype(o_ref.dtype)


def _inproj(lay, x, g, sc, sh, weights, out_dtypes):
    t, d = x.shape
    cr = lay.cond_rows(TM_PROJ)
    row = lambda i, cr: (i, 0)
    const = lambda i, cr: (0, 0)
    cond = lambda i, cr: (cr[i], 0, 0)
    in_specs = [pl.BlockSpec((TM_PROJ, d), row), pl.BlockSpec((1, d), const),
                pl.BlockSpec((1, 1, d), cond), pl.BlockSpec((1, 1, d), cond)]
    in_specs += [pl.BlockSpec(w.shape, const) for w in weights]
    out_specs = [pl.BlockSpec((TM_PROJ, w.shape[1]), row) for w in weights]
    out_shape = [jax.ShapeDtypeStruct((t, w.shape[1]), dt) for w, dt in zip(weights, out_dtypes)]
    return pl.pallas_call(
        _inproj_kernel,
        grid_spec=pltpu.PrefetchScalarGridSpec(num_scalar_prefetch=1, grid=(t // TM_PROJ,),
                                               in_specs=in_specs, out_specs=out_specs),
        out_shape=out_shape,
        compiler_params=_cparams(1),
        name="inproj",
    )(cr, x, g.reshape(1, d), sc, sh, *weights)


def _rope(x, c, s):
    return x * c + pltpu.roll(x, RET_DK // 2, 1) * s


def _ret_kernel(fb_ref, bb_ref, sq_ref, fi_ref, la_ref, rf_ref, rb_ref,
                lg_ref, qf_ref, kf_ref, vf_ref, qb_ref, kb_ref, vb_ref,
                cf_ref, sf_ref, cb_ref, sb_ref, s0_ref,
                of_ref, ob_ref, sout_ref,
                st_ref, dm_ref, dec_ref):
    g = pl.program_id(0)
    c = TOK

    @pl.when(g == 0)
    def _():
        ii = lax.broadcasted_iota(I32, (c, c), 0)
        jj = lax.broadcasted_iota(I32, (c, c), 1)
        diff = (ii - jj).astype(F32)
        ri = lax.broadcasted_iota(I32, (c, RET_DK), 0).astype(F32)
        for h in range(RET_HEADS):
            lf = lg_ref[0, h]
            lb = lg_ref[1, h]
            low = jnp.exp(lf * jnp.maximum(diff, 0.0))
            up = jnp.exp(lb * jnp.maximum(-diff, 0.0))
            dm_ref[h] = jnp.where(diff > 0, low, jnp.where(diff < 0, up, 2.0))
            dec_ref[0, h] = jnp.exp(lf * (ri + 1.0))
            dec_ref[1, h] = jnp.exp(lf * (c - 1.0 - ri))
            dec_ref[2, h] = jnp.exp(lb * (c - ri))
            dec_ref[3, h] = jnp.exp(lb * ri)

    @pl.when(fi_ref[g] == 1)
    def _():
        st_ref[...] = s0_ref[0]

    scale = RET_DK ** -0.5
    cf, sf, cb, sb = cf_ref[...], sf_ref[...], cb_ref[...], sb_ref[...]
    zero_row = jnp.zeros((1, RET_DK), F32)
    for h in range(RET_HEADS):
        sl = slice(h * RET_DK, (h + 1) * RET_DK)
        lf = lg_ref[0, h]
        lb = lg_ref[1, h]
        q = _rope(qf_ref[:, sl].astype(F32), cf, sf)
        k = _rope(kf_ref[:, sl].astype(F32), cf, sf) * scale
        v = vf_ref[:, sl]
        s = _dot_nt(q.astype(BF16), k.astype(BF16)) * dm_ref[h]
        st = st_ref[0, h]
        o = _dot(s.astype(BF16), v) + _dot((q * dec_ref[0, h]).astype(BF16), st.astype(BF16))
        of_ref[:, sl] = o
        st_ref[0, h] = jnp.exp(zero_row + lf * c) * st + _dot_tn((k * dec_ref[1, h]).astype(BF16), v)
        q = _rope(qb_ref[:, sl].astype(F32), cb, sb)
        k = _rope(kb_ref[:, sl].astype(F32), cb, sb) * scale
        v = vb_ref[:, sl]
        st = st_ref[1, h]
        ob_ref[:, sl] = _dot((q * dec_ref[2, h]).astype(BF16), st.astype(BF16))
        st_ref[1, h] = jnp.exp(zero_row + lb * c) * st + _dot_tn((k * dec_ref[3, h]).astype(BF16), v)

    @pl.when(la_ref[g] == 1)
    def _():
        sout_ref[0] = st_ref[...]


def _retention(lay, proj, log_gamma, rope_c, rope_s, s0):
    t = proj.shape[0]
    sched = lay.scan_schedule(TOK)
    qcol, kcol, vcol = (2 * CONV_CH) // RET_W, (2 * CONV_CH) // RET_W + 1, (2 * CONV_CH) // RET_W + 2

    def tok(which, col):
        return pl.BlockSpec((TOK, RET_W), lambda g, *m: (m[which][g], col))

    def rope(which):
        return pl.BlockSpec((TOK, RET_DK), lambda g, *m: (m[which][g], 0))

    state = pl.BlockSpec((1, 2, RET_HEADS, RET_DK, RET_DK), lambda g, *m: (m[2][g], 0, 0, 0, 0))
    in_specs = [pl.BlockSpec(memory_space=pltpu.SMEM),
                tok(0, qcol), tok(0, kcol), tok(0, vcol), tok(1, qcol), tok(1, kcol), tok(1, vcol),
                rope(5), rope(5), rope(6), rope(6), state]
    out_specs = [pl.BlockSpec((TOK, RET_W), lambda g, *m: (m[0][g], 0)),
                 pl.BlockSpec((TOK, RET_W), lambda g, *m: (m[1][g], 0)), state]
    out_shape = [jax.ShapeDtypeStruct((t, RET_W), F32), jax.ShapeDtypeStruct((t, RET_W), F32),
                 jax.ShapeDtypeStruct(s0.shape, F32)]
    scratch = [pltpu.VMEM((2, RET_HEADS, RET_DK, RET_DK), F32),
               pltpu.VMEM((RET_HEADS, TOK, TOK), F32),
               pltpu.VMEM((4, RET_HEADS, TOK, RET_DK), F32)]
    return pl.pallas_call(
        _ret_kernel,
        grid_spec=pltpu.PrefetchScalarGridSpec(num_scalar_prefetch=7, grid=(len(sched[0]),),
                                               in_specs=in_specs, out_specs=out_specs,
                                               scratch_shapes=scratch),
        out_shape=out_shape,
        compiler_params=_cparams(1),
        name="retention",
    )(*sched, log_gamma, proj, proj, proj, proj, proj, proj, rope_c, rope_s, rope_c, rope_s, s0)


def _even_post_kernel(cr_ref, le_ref, re_ref,
                      glu_ref, prev_ref, next_ref, gt_ref, of_ref, ob_ref, x_ref,
                      cw_ref, lng_ref, lnb_ref, wout_ref, npost_ref, g1_ref,
                      out_ref, buf_ref):
    i = pl.program_id(0)

    def glu(r):
        r = r.astype(F32)
        return r[:, :CONV_CH] * jax.nn.sigmoid(r[:, CONV_CH:])

    keep_l = jnp.where(le_ref[i] == 1, 0.0, 1.0)
    keep_r = jnp.where(re_ref[i] == 1, 0.0, 1.0)
    buf_ref[0:HALO, :] = glu(prev_ref[...]) * keep_l
    buf_ref[HALO:HALO + TOK, :] = glu(glu_ref[...])
    buf_ref[HALO + TOK:, :] = glu(next_ref[...]) * keep_r
    acc = jnp.zeros((TOK, CONV_CH), F32)
    off = HALO - CONV_W // 2
    for j in range(CONV_W):
        acc = acc + buf_ref[off + j:off + j + TOK, :] * cw_ref[j:j + 1, :]
    mu = jnp.mean(acc, axis=-1, keepdims=True)
    ac = acc - mu
    y = ac * lax.rsqrt(jnp.mean(ac * ac, axis=-1, keepdims=True) + EPS) * lng_ref[...] + lnb_ref[...]
    conv_out = _silu(y)

    o = of_ref[...] + ob_ref[...]
    gt = gt_ref[...].astype(F32)
    parts = []
    for h in range(RET_HEADS):
        sl = slice(h * RET_DK, (h + 1) * RET_DK)
        oh = o[:, sl]
        oc = oh - jnp.mean(oh, axis=-1, keepdims=True)
        on = oc * lax.rsqrt(jnp.mean(oc * oc, axis=-1, keepdims=True) + EPS)
        parts.append(on * _silu(gt[:, sl]))
    ret_out = jnp.concatenate(parts, axis=1)

    out = _dot(conv_out.astype(BF16), wout_ref[0:CONV_CH, :]) + _dot(ret_out.astype(BF16), wout_ref[CONV_CH:, :])
    out_ref[...] = x_ref[...] + g1_ref[0] * _rms(out, npost_ref[...])


def _even_post(lay, proj, o_f, o_b, x, conv_w, ln_g, ln_b, w_out, n_post, g1):
    t, d = x.shape
    cr = lay.cond_rows(TOK)
    le, re = lay.edges(TOK)
    hb = TOK // HALO
    n_halo = t // HALO
    row = lambda i, *m: (i, 0)
    const = lambda i, *m: (0, 0)
    in_specs = [pl.BlockSpec((TOK, 2 * CONV_CH), row),
                pl.BlockSpec((HALO, 2 * CONV_CH), lambda i, *m: (jnp.maximum(i * hb - 1, 0), 0)),
                pl.BlockSpec((HALO, 2 * CONV_CH), lambda i, *m: (jnp.minimum((i + 1) * hb, n_halo - 1), 0)),
                pl.BlockSpec((TOK, RET_W), lambda i, *m: (i, EVEN_IN // RET_W - 1)),
                pl.BlockSpec((TOK, RET_W), row), pl.BlockSpec((TOK, RET_W), row),
                pl.BlockSpec((TOK, d), row),
                pl.BlockSpec(conv_w.shape, const), pl.BlockSpec((1, CONV_CH), const), pl.BlockSpec((1, CONV_CH), const),
                pl.BlockSpec(w_out.shape, const), pl.BlockSpec((1, d), const),
                pl.BlockSpec((1, 1, d), lambda i, *m: (m[0][i], 0, 0))]
    return pl.pallas_call(
        _even_post_kernel,
        grid_spec=pltpu.PrefetchScalarGridSpec(num_scalar_prefetch=3, grid=(t // TOK,),
                                               in_specs=in_specs, out_specs=pl.BlockSpec((TOK, d), row),
                                               scratch_shapes=[pltpu.VMEM((TOK + 2 * HALO, CONV_CH), F32)]),
        out_shape=jax.ShapeDtypeStruct((t, d), F32),
        compiler_params=_cparams(1),
        name="even_post",
    )(cr, le, re, proj, proj, proj, proj, o_f, o_b, x, conv_w, ln_g.reshape(1, -1), ln_b.reshape(1, -1),
      w_out, n_post.reshape(1, d), g1)


def _odd_pre_kernel(le_ref, re_ref, qkv_ref, prev_ref, next_ref, ab_ref, cw_ref, alog_ref, dtb_ref,
                    qkv_out_ref, gb_out_ref, buf_ref):
    i = pl.program_id(0)
    keep_l = jnp.where(le_ref[i] == 1, 0.0, 1.0)
    keep_r = jnp.where(re_ref[i] == 1, 0.0, 1.0)
    buf_ref[0:HALO, :] = prev_ref[...].astype(F32) * keep_l
    buf_ref[HALO:HALO + TOK, :] = qkv_ref[...].astype(F32)
    buf_ref[HALO + TOK:, :] = next_ref[...].astype(F32) * keep_r
    off = HALO - SHORT_W // 2
    for blk in range(3 * GDN_HEADS):
        sl = slice(blk * GDN_DK, (blk + 1) * GDN_DK)
        acc = jnp.zeros((TOK, GDN_DK), F32)
        for j in range(SHORT_W):
            acc = acc + buf_ref[off + j:off + j + TOK, sl] * cw_ref[j:j + 1, sl]
        y = _silu(acc)
        if blk < 2 * GDN_HEADS:
            y = y * lax.rsqrt(jnp.sum(y * y, axis=-1, keepdims=True) + EPS)
            if blk < GDN_HEADS:
                y = y * (GDN_DK ** -0.5)
        qkv_out_ref[:, sl] = y.astype(qkv_out_ref.dtype)
    ab = ab_ref[...]
    z = ab + dtb_ref[...]
    softplus = jnp.maximum(z, 0.0) + jnp.log(1.0 + jnp.exp(-jnp.abs(z)))
    gate = -jnp.exp(alog_ref[...]) * softplus
    beta = jax.nn.sigmoid(ab)
    lane = lax.broadcasted_iota(I32, ab.shape, 1)
    gb_out_ref[...] = jnp.where(lane < 2 * GDN_HEADS, gate, beta)


def _odd_pre(lay, proj, ab, conv_w, alog_row, dtb_row):
    t = proj.shape[0]
    w = 2 * GDN_KW + GDN_VW
    le, re = lay.edges(TOK)
    hb = TOK // HALO
    n_halo = t // HALO
    row = lambda i, *m: (i, 0)
    const = lambda i, *m: (0, 0)
    in_specs = [pl.BlockSpec((TOK, w), row),
                pl.BlockSpec((HALO, w), lambda i, *m: (jnp.maximum(i * hb - 1, 0), 0)),
                pl.BlockSpec((HALO, w), lambda i, *m: (jnp.minimum((i + 1) * hb, n_halo - 1), 0)),
                pl.BlockSpec((TOK, LANES), row),
                pl.BlockSpec(conv_w.shape, const), pl.BlockSpec((1, LANES), const), pl.BlockSpec((1, LANES), const)]
    out_specs = [pl.BlockSpec((TOK, w), row), pl.BlockSpec((TOK, LANES), row)]
    return pl.pallas_call(
        _odd_pre_kernel,
        grid_spec=pltpu.PrefetchScalarGridSpec(num_scalar_prefetch=2, grid=(t // TOK,),
                                               in_specs=in_specs, out_specs=out_specs,
                                               scratch_shapes=[pltpu.VMEM((TOK + 2 * HALO, w), F32)]),
        out_shape=[jax.ShapeDtypeStruct((t, w), BF16), jax.ShapeDtypeStruct((t, LANES), F32)],
        compiler_params=_cparams(1),
        name="odd_pre",
    )(le, re, proj, proj, proj, ab, conv_w, alog_row, dtb_row)


_GDN_LEVELS = tuple(2 ** p for p in range(int(math.log2(GDN_CHUNK))))


def _gdn_masks(msk_ref, tri_ref):
    c = GDN_CHUNK
    ii = lax.broadcasted_iota(I32, (c, c), 0)
    jj = lax.broadcasted_iota(I32, (c, c), 1)
    one = jnp.ones((c, c), F32)
    zero = jnp.zeros((c, c), F32)
    for rev in (0, 1):
        a, b = (ii, jj) if rev == 0 else (jj, ii)
        base = rev * 9
        msk_ref[base + 0] = jnp.where(a >= b, one, zero)
        msk_ref[base + 1] = jnp.where(a > b, one, zero)
        for l, m in enumerate(_GDN_LEVELS):
            sh = int(math.log2(m))
            ab_, bb_ = a >> sh, b >> sh
            hit = ((ab_ & 1) == 1) & (bb_ == ab_ - 1)
            msk_ref[base + 2 + l] = jnp.where(hit, one, zero)
        tri_ref[rev] = jnp.where(a >= b, one, zero).astype(BF16)


def _gdn_chunk_step(dirs, st_ref, msk_ref, tri_ref):
    c = GDN_CHUNK
    nh = GDN_HEADS
    probs = [(rev, h) for rev in range(2) for h in range(nh)]
    ones = jnp.ones((c, c), BF16)
    gcum, gcum_t, e_all, kdec_all, elast_all, gbs = [], [], [], [], [], []
    for rev in range(2):
        gb = dirs[rev][3][...]
        g1, g2, g3 = _split3(gb)
        tri = tri_ref[rev]
        gc = _dot(tri, g1) + _dot(tri, g2) + _dot(tri, g3)
        gl = _dot(ones, g1) + _dot(ones, g2) + _dot(ones, g3)
        gbs.append(gb)
        gcum.append(gc)
        gcum_t.append(gc.T)
        e_all.append(jnp.exp(gc))
        kdec_all.append(jnp.exp(gl - gc))
        elast_all.append(jnp.exp(gl))

    def col(rev, h):
        return rev * nh + h

    def hsl(h):
        return slice(h * GDN_DK, (h + 1) * GDN_DK)

    st = [st_ref[rev, h] for rev, h in probs]

    a, attn, kbs = [], [], []
    for rev, h in probs:
        base = rev * 9
        p = col(rev, h)
        q_ref, k_ref = dirs[rev][0], dirs[rev][1]
        k = k_ref[:, hsl(h)]
        gcb = jnp.broadcast_to(gcum[rev][:, p:p + 1], (c, c))
        grb = jnp.broadcast_to(gcum_t[rev][p:p + 1, :], (c, c))
        decay = jnp.exp(jnp.where(msk_ref[base] > 0, gcb - grb, NEG_BIG))
        kb = k.astype(F32) * gbs[rev][:, 2 * nh + p:2 * nh + p + 1]
        kbs.append(kb)
        a.append(_dot_nt(kb.astype(BF16), k) * decay * msk_ref[base + 1])
        attn.append((_dot_nt(q_ref[:, hsl(h)], k) * decay).astype(BF16))

    tinv = []
    for i, (rev, h) in enumerate(probs):
        base = rev * 9
        tinv.append(msk_ref[base] - msk_ref[base + 1] - a[i] * msk_ref[base + 2])
    for l in range(1, len(_GDN_LEVELS)):
        bt = []
        for i, (rev, h) in enumerate(probs):
            bt.append(_dot((a[i] * msk_ref[rev * 9 + 2 + l]).astype(BF16), tinv[i].astype(BF16)))
        for i in range(len(probs)):
            tinv[i] = tinv[i] - _dot(tinv[i].astype(BF16), bt[i].astype(BF16))

    sol = []
    for i, (rev, h) in enumerate(probs):
        p = col(rev, h)
        v = dirs[rev][2][:, hsl(h)].astype(F32)
        beta = gbs[rev][:, 2 * nh + p:2 * nh + p + 1]
        rhs = jnp.concatenate([v * beta, kbs[i] * e_all[rev][:, p:p + 1]], axis=1)
        sol.append(_dot(tinv[i].astype(BF16), rhs.astype(BF16)))
    v_new = []
    for i in range(len(probs)):
        u, w = sol[i][:, :GDN_DK], sol[i][:, GDN_DK:]
        v_new.append((u - _dot(w.astype(BF16), st[i].astype(BF16))).astype(BF16))
    for i, (rev, h) in enumerate(probs):
        p = col(rev, h)
        q = dirs[rev][0][:, hsl(h)].astype(F32)
        qs = (q * e_all[rev][:, p:p + 1]).astype(BF16)
        dirs[rev][4][:, hsl(h)] = _dot(qs, st[i].astype(BF16)) + _dot(attn[i], v_new[i])
    new_st = []
    for i, (rev, h) in enumerate(probs):
        p = col(rev, h)
        kd = (dirs[rev][1][:, hsl(h)].astype(F32) * kdec_all[rev][:, p:p + 1]).astype(BF16)
        new_st.append(st[i] * elast_all[rev][:, p:p + 1] + _dot_tn(kd, v_new[i]))
    for i, (rev, h) in enumerate(probs):
        st_ref[rev, h] = new_st[i]


def _gdn_kernel(fb_ref, bb_ref, sq_ref, fi_ref, la_ref,
                qf_ref, kf_ref, vf_ref, gf_ref, qb_ref, kb_ref, vb_ref, gbk_ref, s0_ref,
                of_ref, ob_ref, sout_ref,
                st_ref, msk_ref, tri_ref):
    g = pl.program_id(0)

    @pl.when(g == 0)
    def _():
        _gdn_masks(msk_ref, tri_ref)

    @pl.when(fi_ref[g] == 1)
    def _():
        st_ref[...] = s0_ref[0]

    _gdn_chunk_step(((qf_ref, kf_ref, vf_ref, gf_ref, of_ref), (qb_ref, kb_ref, vb_ref, gbk_ref, ob_ref)),
                    st_ref, msk_ref, tri_ref)

    @pl.when(la_ref[g] == 1)
    def _():
        sout_ref[0] = st_ref[...]


def _gdn(lay, qkv, gb, s0):
    t = qkv.shape[0]
    c = GDN_CHUNK
    sched = lay.scan_schedule(c)[:5]

    def tok(which, col):
        return pl.BlockSpec((c, GDN_KW), lambda g, *m: (m[which][g], col))

    def gate(which):
        return pl.BlockSpec((c, LANES), lambda g, *m: (m[which][g], 0))

    state = pl.BlockSpec((1, 2, GDN_HEADS, GDN_DK, GDN_DK), lambda g, *m: (m[2][g], 0, 0, 0, 0))
    in_specs = [tok(0, 0), tok(0, 1), tok(0, 2), gate(0), tok(1, 0), tok(1, 1), tok(1, 2), gate(1), state]
    out_specs = [pl.BlockSpec((c, GDN_VW), lambda g, *m: (m[0][g], 0)),
                 pl.BlockSpec((c, GDN_VW), lambda g, *m: (m[1][g], 0)), state]
    out_shape = [jax.ShapeDtypeStruct((t, GDN_VW), F32), jax.ShapeDtypeStruct((t, GDN_VW), F32),
                 jax.ShapeDtypeStruct(s0.shape, F32)]
    scratch = [pltpu.VMEM((2, GDN_HEADS, GDN_DK, GDN_DK), F32),
               pltpu.VMEM((18, c, c), F32), pltpu.VMEM((2, c, c), BF16)]
    return pl.pallas_call(
        _gdn_kernel,
        grid_spec=pltpu.PrefetchScalarGridSpec(num_scalar_prefetch=5, grid=(len(sched[0]),),
                                               in_specs=in_specs, out_specs=out_specs,
                                               scratch_shapes=scratch),
        out_shape=out_shape,
        compiler_params=_cparams(1),
        name="gdn_scan",
    )(*sched, qkv, qkv, qkv, gb, qkv, qkv, qkv, gb, s0)


def _odd_post_kernel(cr_ref, of_ref, ob_ref, z_ref, x_ref, nw_ref, wout_ref, npost_ref, g1_ref, out_ref):
    o = of_ref[...] + ob_ref[...]
    z = z_ref[...].astype(F32)
    nw = nw_ref[...]
    parts = []
    for h in range(GDN_HEADS):
        sl = slice(h * GDN_DK, (h + 1) * GDN_DK)
        parts.append(_rms(o[:, sl], nw) * _silu(z[:, sl]))
    y = jnp.concatenate(parts, axis=1)
    out = _dot(y.astype(BF16), wout_ref[...])
    out_ref[...] = x_ref[...] + g1_ref[0] * _rms(out, npost_ref[...])


def _odd_post(lay, o_f, o_b, proj, x, norm_w, w_out, n_post, g1):
    t, d = x.shape
    cr = lay.cond_rows(TOK)
    row = lambda i, *m: (i, 0)
    const = lambda i, *m: (0, 0)
    in_specs = [pl.BlockSpec((TOK, GDN_VW), row), pl.BlockSpec((TOK, GDN_VW), row),
                pl.BlockSpec((TOK, GDN_VW), lambda i, *m: (i, 3)),
                pl.BlockSpec((TOK, d), row),
                pl.BlockSpec((1, GDN_DK), const), pl.BlockSpec(w_out.shape, const), pl.BlockSpec((1, d), const),
                pl.BlockSpec((1, 1, d), lambda i, *m: (m[0][i], 0, 0))]
    return pl.pallas_call(
        _odd_post_kernel,
        grid_spec=pltpu.PrefetchScalarGridSpec(num_scalar_prefetch=1, grid=(t // TOK,),
                                               in_specs=in_specs, out_specs=pl.BlockSpec((TOK, d), row)),
        out_shape=jax.ShapeDtypeStruct((t, d), F32),
        compiler_params=_cparams(1),
        name="odd_post",
    )(cr, o_f, o_b, proj, x, norm_w.reshape(1, -1), w_out, n_post.reshape(1, d), g1)


SUB = D_MODEL // LANES


def _store_token_major(ref, x):
    n = x.shape[0]
    for s in range(SUB):
        ref[pl.ds(s, n, stride=SUB), :] = x[:, s * LANES:(s + 1) * LANES]


def _load_token_major(ref, n):
    return jnp.concatenate([ref[pl.ds(s, n, stride=SUB), :] for s in range(SUB)], axis=1)


def _router_kernel(cr_ref, x_ref, g_ref, sc_ref, sh_ref, rwt_ref, rb_ref,
                   h_ref, ids_ref, wts_ref, rank_ref, cnt_ref,
                   base_ref, su_ref):
    i = pl.program_id(0)
    tm = TM_ROUTE

    @pl.when(i == 0)
    def _():
        base_ref[...] = jnp.zeros_like(base_ref)
        ii = lax.broadcasted_iota(I32, (tm, tm), 0)
        jj = lax.broadcasted_iota(I32, (tm, tm), 1)
        su_ref[...] = jnp.where(ii < jj, 1.0, 0.0).astype(BF16)

    h = _rms(x_ref[...], g_ref[...]) * (1.0 + sc_ref[0]) + sh_ref[0]
    hb = h.astype(BF16)
    _store_token_major(h_ref, hb.astype(F32))
    h1 = hb
    w1, w2, w3 = _split3(rwt_ref[...])
    logits = _dot_nt(w1, h1) + _dot_nt(w2, h1) + _dot_nt(w3, h1)
    score = jax.nn.sigmoid(logits)
    sel = score + rb_ref[...]

    def row(a, e):
        return a[e:e + 1, :]

    gsum = []
    for gi in range(N_GROUPS):
        a, b, c, d = (row(sel, gi * EXP_PER_GROUP + j) for j in range(EXP_PER_GROUP))
        hi1, lo1 = jnp.maximum(a, b), jnp.minimum(a, b)
        hi2, lo2 = jnp.maximum(c, d), jnp.minimum(c, d)
        gsum.append(jnp.maximum(hi1, hi2) + jnp.maximum(jnp.minimum(hi1, hi2), jnp.maximum(lo1, lo2)))
    best = jnp.zeros_like(gsum[0]).astype(I32)
    cur = gsum[0]
    for gi in range(1, N_GROUPS):
        upd = gsum[gi] > cur
        best = jnp.where(upd, gi, best)
        cur = jnp.where(upd, gsum[gi], cur)

    def pick(arr, j):
        out = row(arr, j)
        for gi in range(1, N_GROUPS):
            out = jnp.where(best == gi, row(arr, gi * EXP_PER_GROUP + j), out)
        return out

    vals = [pick(sel, j) for j in range(EXP_PER_GROUP)]
    scs = [pick(score, j) for j in range(EXP_PER_GROUP)]

    def argmax_first(vs):
        idx = jnp.zeros_like(best)
        m = vs[0]
        for j in range(1, EXP_PER_GROUP):
            upd = vs[j] > m
            idx = jnp.where(upd, j, idx)
            m = jnp.where(upd, vs[j], m)
        return idx

    i1 = argmax_first(vals)
    vals2 = [jnp.where(i1 == j, -jnp.inf, vals[j]) for j in range(EXP_PER_GROUP)]
    i2 = argmax_first(vals2)

    def take(vs, idx):
        out = vs[0]
        for j in range(1, EXP_PER_GROUP):
            out = jnp.where(idx == j, vs[j], out)
        return out

    s1, s2 = take(scs, i1), take(scs, i2)
    tot = s1 + s2
    e1 = best * EXP_PER_GROUP + i1
    e2 = best * EXP_PER_GROUP + i2
    ids_ref[0:1, :] = e1
    ids_ref[1:2, :] = e2
    wts_ref[0:1, :] = s1 / tot
    wts_ref[1:2, :] = s2 / tot

    erow = lax.broadcasted_iota(I32, (N_EXPERTS, tm), 0)
    m1 = jnp.where(erow == e1, 1.0, 0.0)
    m2 = jnp.where(erow == e2, 1.0, 0.0)
    both = m1 + m2
    before = _dot(both.astype(BF16), su_ref[...]) + base_ref[...]
    rank_ref[0:1, :] = jnp.sum(m1 * before, axis=0, keepdims=True).astype(I32)
    rank_ref[1:2, :] = jnp.sum(m2 * before, axis=0, keepdims=True).astype(I32)
    base_ref[...] = base_ref[...] + jnp.sum(both, axis=1, keepdims=True)
    cnt_ref[...] = jnp.broadcast_to(base_ref[...], cnt_ref.shape).astype(I32)


def _router(lay, x, g, sc, sh, rwt, rbias):
    t, d = x.shape
    cr = lay.cond_rows(TM_ROUTE)
    tm = TM_ROUTE
    row = lambda i, cr: (i, 0)
    col = lambda i, cr: (0, i)
    const = lambda i, cr: (0, 0)
    cond = lambda i, cr: (cr[i], 0, 0)
    in_specs = [pl.BlockSpec((tm, d), row), pl.BlockSpec((1, d), const),
                pl.BlockSpec((1, 1, d), cond), pl.BlockSpec((1, 1, d), cond),
                pl.BlockSpec((N_EXPERTS, d), const), pl.BlockSpec((N_EXPERTS, 1), const)]
    out_specs = [pl.BlockSpec((tm * SUB, LANES), row), pl.BlockSpec((2, tm), col), pl.BlockSpec((2, tm), col),
                 pl.BlockSpec((2, tm), col), pl.BlockSpec((N_EXPERTS, LANES), const)]
    out_shape = [jax.ShapeDtypeStruct((t * SUB, LANES), F32), jax.ShapeDtypeStruct((2, t), I32),
                 jax.ShapeDtypeStruct((2, t), F32), jax.ShapeDtypeStruct((2, t), I32),
                 jax.ShapeDtypeStruct((N_EXPERTS, LANES), I32)]
    return pl.pallas_call(
        _router_kernel,
        grid_spec=pltpu.PrefetchScalarGridSpec(num_scalar_prefetch=1, grid=(t // tm,),
                                               in_specs=in_specs, out_specs=out_specs,
                                               scratch_shapes=[pltpu.VMEM((N_EXPERTS, 1), F32),
                                                               pltpu.VMEM((tm, tm), BF16)]),
        out_shape=out_shape,
        compiler_params=_cparams(1),
        name="router",
    )(cr, x, g.reshape(1, d), sc, sh, rwt, rbias.reshape(N_EXPERTS, 1))


def _row_copy(src, dst, sem):
    return pltpu.make_async_copy(src, dst, sem)


def _dispatch_kernel(pos_ref, h_ref, init_ref, sorted_ref, sem):
    del init_ref
    n = TOK

    def issue(r, carry):
        src = h_ref.at[pl.ds(pl.multiple_of(r * SUB, SUB), SUB)]
        for k in range(2):
            dst = pl.multiple_of(pos_ref[0, 0, 2 * r + k] * SUB, SUB)
            _row_copy(src, sorted_ref.at[pl.ds(dst, SUB)], sem).start()
        return carry

    lax.fori_loop(0, n, issue, 0)
    for _ in range(2):
        _row_copy(h_ref, sorted_ref.at[pl.ds(0, n * SUB)], sem).wait()


def _dispatch(h, pos3, n_rows):
    t = h.shape[0] // SUB
    init = jnp.zeros((n_rows * SUB, LANES), h.dtype)
    return pl.pallas_call(
        _dispatch_kernel,
        grid=(t // TOK,),
        in_specs=[pl.BlockSpec((1, 1, 2 * TOK), lambda i: (i, 0, 0), memory_space=pltpu.SMEM),
                  pl.BlockSpec((TOK * SUB, LANES), lambda i: (i, 0)),
                  pl.BlockSpec(memory_space=pl.ANY)],
        out_specs=pl.BlockSpec(memory_space=pl.ANY),
        out_shape=jax.ShapeDtypeStruct(init.shape, init.dtype),
        scratch_shapes=[pltpu.SemaphoreType.DMA(())],
        input_output_aliases={2: 0},
        compiler_params=_cparams(1),
        name="dispatch",
    )(pos3, h, init)


def _expert_kernel(te_ref, tv_ref, h_ref, wg_ref, wu_ref, wd_ref, y_ref):
    i = pl.program_id(0)

    @pl.when(tv_ref[i] == 1)
    def _():
        hb = _load_token_major(h_ref, TM_EXP).astype(BF16)
        a = _dot(hb, wg_ref[0])
        b = _dot(hb, wu_ref[0])
        he = (_silu(a) * b).astype(BF16)
        _store_token_major(y_ref, _dot(he, wd_ref[0]))

    @pl.when(tv_ref[i] == 0)
    def _():
        y_ref[...] = jnp.zeros_like(y_ref)


def _experts(hs, tile_expert, tile_valid, wg, wu, wd):
    d = wg.shape[1]
    r = hs.shape[0] // SUB
    row = lambda i, te, tv: (i, 0)
    wsel = lambda i, te, tv: (te[i], 0, 0)
    return pl.pallas_call(
        _expert_kernel,
        grid_spec=pltpu.PrefetchScalarGridSpec(
            num_scalar_prefetch=2, grid=(r // TM_EXP,),
            in_specs=[pl.BlockSpec((TM_EXP * SUB, LANES), row),
                      pl.BlockSpec((1, d, D_EXPERT), wsel), pl.BlockSpec((1, d, D_EXPERT), wsel),
                      pl.BlockSpec((1, D_EXPERT, d), wsel)],
            out_specs=pl.BlockSpec((TM_EXP * SUB, LANES), row)),
        out_shape=jax.ShapeDtypeStruct(hs.shape, F32),
        compiler_params=_cparams(1),
        name="experts",
    )(tile_expert, tile_valid, hs, wg, wu, wd)


def _combine_kernel(cr_ref, pos_ref, ys_ref, wts_ref, x_ref, npost_ref, g2_ref, out_ref, buf_ref, sem):
    n = TOK

    def issue(r, carry):
        row = pl.multiple_of(r * SUB, SUB)
        for k in range(2):
            src = pl.multiple_of(pos_ref[0, 0, 2 * r + k] * SUB, SUB)
            _row_copy(ys_ref.at[pl.ds(src, SUB)], buf_ref.at[k, pl.ds(row, SUB)], sem).start()
        return carry

    lax.fori_loop(0, n, issue, 0)
    for slot in range(2):
        _row_copy(ys_ref.at[pl.ds(0, n * SUB)], buf_ref.at[slot], sem).wait()

    w = wts_ref[...]
    y = _load_token_major(buf_ref.at[0], n) * w[:, 0:1] + _load_token_major(buf_ref.at[1], n) * w[:, 1:2]
    out_ref[...] = x_ref[...] + g2_ref[0] * _rms(y, npost_ref[...])


def _combine(lay, ys3, pos3, wts, x, n_post, g2):
    t, d = x.shape
    cr = lay.cond_rows(TOK)
    row = lambda i, cr: (i, 0)
    const = lambda i, cr: (0, 0)
    in_specs = [pl.BlockSpec((1, 1, 2 * TOK), lambda i, cr: (i, 0, 0), memory_space=pltpu.SMEM),
                pl.BlockSpec(memory_space=pl.ANY),
                pl.BlockSpec((TOK, 2), row),
                pl.BlockSpec((TOK, d), row),
                pl.BlockSpec((1, d), const),
                pl.BlockSpec((1, 1, d), lambda i, cr: (cr[i], 0, 0))]
    return pl.pallas_call(
        _combine_kernel,
        grid_spec=pltpu.PrefetchScalarGridSpec(
            num_scalar_prefetch=1, grid=(t // TOK,), in_specs=in_specs,
            out_specs=pl.BlockSpec((TOK, d), row),
            scratch_shapes=[pltpu.VMEM((2, TOK * SUB, LANES), F32), pltpu.SemaphoreType.DMA(())]),
        out_shape=jax.ShapeDtypeStruct((t, d), F32),
        compiler_params=_cparams(1),
        name="combine",
    )(cr, pos3, ys3, wts, x, n_post.reshape(1, d), g2)


def _moe(lay, x, g_pre, sc, sh, g2, n_post, rwt, rbias, wg, wu, wd):
    t, d = x.shape
    h, ids, wts, rank, cnt = _router(lay, x, g_pre, sc, sh, rwt, rbias)
    counts = cnt[:, 0]
    padded = ((counts + TM_EXP - 1) // TM_EXP) * TM_EXP
    ends = jnp.cumsum(padded)
    offs = ends - padded
    eids = jnp.arange(N_EXPERTS, dtype=I32)[:, None, None]
    pos = jnp.sum(jnp.where(ids[None] == eids, offs[:, None, None], 0), axis=0) + rank
    pos3 = pos.T.reshape(t // TOK, 1, 2 * TOK).astype(I32)
    n_rows = 2 * t + N_EXPERTS * TM_EXP
    n_tiles = n_rows // TM_EXP
    starts = jnp.arange(n_tiles, dtype=I32) * TM_EXP
    tile_expert = jnp.minimum(jnp.sum((starts[:, None] >= ends[None, :]).astype(I32), axis=1), N_EXPERTS - 1)
    tile_valid = (starts < ends[-1]).astype(I32)
    hs = _dispatch(h, pos3, n_rows)
    ys = _experts(hs, tile_expert.astype(I32), tile_valid, wg, wu, wd)
    return _combine(lay, ys, pos3, wts.T, x, n_post, g2)


def _rope_tables(ld):
    rows = ld // GRID_W
    r = jnp.repeat(jnp.arange(rows, dtype=F32), GRID_W)
    col = jnp.tile(jnp.arange(GRID_W, dtype=F32), rows)
    quarter = RET_DK // 4
    inv = ROPE_BASE ** (-jnp.arange(quarter, dtype=F32) / quarter)
    ang = jnp.concatenate([r[:, None] * inv, col[:, None] * inv], axis=-1)
    cos, sin = jnp.cos(ang), jnp.sin(ang)
    c = jnp.concatenate([cos, cos], axis=-1)
    s = jnp.concatenate([-sin, sin], axis=-1)
    ident_c = jnp.ones((TOK, RET_DK), F32)
    ident_s = jnp.zeros((TOK, RET_DK), F32)
    return jnp.concatenate([c, ident_c], axis=0), jnp.concatenate([s, ident_s], axis=0)


def kernel(x_prompt, x_sample, state_ret, state_gdn, c, c_ctx, w_mod, b_mod, norm_mix_pre, norm_mix_post,
           norm_ffn_pre, norm_ffn_post, ev_w_in, ev_conv_w, ev_conv_ln_g, ev_conv_ln_b, ev_ret_decay, ev_w_out,
           od_w_in, od_conv_w, od_a_log, od_dt_bias, od_norm_w, od_w_out, router_w, router_bias,
           moe_w_gate, moe_w_up, moe_w_down):
    bc, lc, d = x_prompt.shape
    bd, ld, _ = x_sample.shape
    depth = w_mod.shape[0]
    lay = _Layout(bc, lc, bd, ld)
    t = lay.t

    x = jnp.concatenate([x_prompt.reshape(bc * lc, d), x_sample.reshape(bd * ld, d)], axis=0)
    cond = jnp.zeros((COND_PAD, d), F32).at[0].set(c_ctx).at[1:1 + bd].set(c)
    mod = _modulation(cond, w_mod, b_mod)
    mod = mod.reshape(depth, COND_PAD, N_MOD, 1, d).transpose(0, 2, 1, 3, 4)

    rope_c, rope_s = _rope_tables(ld)
    rwt = router_w.T
    ret_states, gdn_states = [], []
    for layer in range(depth):
        sh1, sc1, g1, sh2, sc2, g2 = (mod[layer, j] for j in range(N_MOD))
        i = layer // 2
        if layer % 2 == 0:
            (proj,) = _inproj(lay, x, norm_mix_pre[layer], sc1, sh1, [ev_w_in[i].astype(BF16)], [BF16])
            s0 = jnp.concatenate([jnp.zeros((bc,) + state_ret.shape[2:], F32), state_ret[:, i]], axis=0)
            log_gamma = -jnp.exp(ev_ret_decay[i].astype(F32))
            o_f, o_b, s_out = _retention(lay, proj, log_gamma, rope_c, rope_s, s0)
            ret_states.append(s_out[:bc])
            cw = jnp.zeros((32, CONV_CH), F32).at[:CONV_W].set(ev_conv_w[i])
            x = _even_post(lay, proj, o_f, o_b, x, cw, ev_conv_ln_g[i], ev_conv_ln_b[i],
                           ev_w_out[i].astype(BF16), norm_mix_post[layer], g1)
        else:
            w_in = od_w_in[i]
            n_main = 2 * GDN_KW + 2 * GDN_VW
            w_small = jnp.zeros((d, LANES), F32).at[:, :4 * GDN_HEADS].set(w_in[:, n_main:])
            proj, ab = _inproj(lay, x, norm_mix_pre[layer], sc1, sh1,
                               [w_in[:, :n_main].astype(BF16), w_small.astype(BF16)], [BF16, F32])
            cw = jnp.zeros((8, 2 * GDN_KW + GDN_VW), F32).at[:SHORT_W].set(od_conv_w[i])
            alog_row = jnp.zeros((1, LANES), F32).at[0, :2 * GDN_HEADS].set(od_a_log[i].reshape(-1))
            dtb_row = jnp.zeros((1, LANES), F32).at[0, :2 * GDN_HEADS].set(od_dt_bias[i].reshape(-1))
            qkv, gb = _odd_pre(lay, proj, ab, cw, alog_row, dtb_row)
            s0 = jnp.concatenate([jnp.zeros((bc,) + state_gdn.shape[2:], F32), state_gdn[:, i]], axis=0)
            o_f, o_b, s_out = _gdn(lay, qkv, gb, s0)
            gdn_states.append(s_out[:bc])
            x = _odd_post(lay, o_f, o_b, proj, x, od_norm_w[i], od_w_out[i].astype(BF16),
                          norm_mix_post[layer], g1)
        x = _moe(lay, x, norm_ffn_pre[layer], sc2, sh2, g2, norm_ffn_post[layer], rwt, router_bias,
                 moe_w_gate[layer].astype(BF16), moe_w_up[layer].astype(BF16), moe_w_down[layer].astype(BF16))

    y_prompt = x[:bc * lc].reshape(bc, lc, d)
    y_sample = x[bc * lc:].reshape(bd, ld, d)
    new_ret = jnp.stack(ret_states, axis=1)
    new_gdn = jnp.stack(gdn_states, axis=1)
    return y_prompt, y_sample, new_ret.astype(x_prompt.dtype), new_gdn.astype(x_prompt.dtype)
```

```python
import functools
import math

import jax
import jax.numpy as jnp
import numpy as np
from jax import lax
from jax.experimental import pallas as pl
from jax.experimental.pallas import tpu as pltpu

F32 = jnp.float32
BF16 = jnp.bfloat16
I32 = jnp.int32

D_MODEL = 1024
N_MOD = 6
EPS = 1e-6
GRID_W = 64
CONV_CH = 512
CONV_W = 31
RET_HEADS = 4
RET_DK = 128
RET_W = 512
ROPE_BASE = 10000.0
EVEN_IN = 2 * CONV_CH + 4 * RET_W
GDN_HEADS = 8
GDN_DK = 128
GDN_KW = 1024
GDN_VW = 1024
SHORT_W = 5
N_EXPERTS = 16
N_GROUPS = 4
EXP_PER_GROUP = 4
D_EXPERT = 512

LANES = 128
TOK = 256
TM_PROJ = 512
TM_ROUTE = 512
TM_EXP = 512
GDN_CHUNK = 128
HALO = 16
COND_PAD = 16
VMEM_LIMIT = 56 * 1024 * 1024
NEG_BIG = -1e30


def _cparams(n_axes=1, vmem=VMEM_LIMIT):
    return pltpu.CompilerParams(dimension_semantics=("arbitrary",) * n_axes, vmem_limit_bytes=vmem)


def _silu(x):
    return x * jax.nn.sigmoid(x)


def _rms(x, g):
    return x * lax.rsqrt(jnp.mean(x * x, axis=-1, keepdims=True) + EPS) * g


def _dot(a, b):
    return jnp.dot(a, b, preferred_element_type=F32)


def _dot_nt(a, b):
    return lax.dot_general(a, b, (((1,), (1,)), ((), ())), preferred_element_type=F32)


def _dot_tn(a, b):
    return lax.dot_general(a, b, (((0,), (0,)), ((), ())), preferred_element_type=F32)


def _split3(x):
    x1 = x.astype(BF16)
    r = x - x1.astype(F32)
    x2 = r.astype(BF16)
    x3 = (r - x2.astype(F32)).astype(BF16)
    return x1, x2, x3


class _Layout:
    def __init__(self, bc, lc, bd, ld):
        self.bc, self.lc, self.bd, self.ld = bc, lc, bd, ld
        self.t_ctx = bc * lc
        self.t = bc * lc + bd * ld
        self.n_seq = bc + bd
        assert lc % TOK == 0 and ld % TOK == 0 and self.t_ctx % TM_PROJ == 0 and ld % TM_PROJ == 0
        assert self.t % TM_ROUTE == 0 and lc % GDN_CHUNK == 0 and ld % GDN_CHUNK == 0

    def seq_of_row(self, r):
        if r < self.t_ctx:
            return r // self.lc, r % self.lc, self.lc
        r2 = r - self.t_ctx
        return self.bc + r2 // self.ld, r2 % self.ld, self.ld

    def cond_rows(self, tile):
        out = []
        for i in range(self.t // tile):
            s, _, _ = self.seq_of_row(i * tile)
            out.append(0 if s < self.bc else 1 + s - self.bc)
        return np.asarray(out, np.int32)

    def edges(self, tile):
        left, right = [], []
        for i in range(self.t // tile):
            _, p, l = self.seq_of_row(i * tile)
            left.append(int(p == 0))
            right.append(int(p + tile == l))
        return np.asarray(left, np.int32), np.asarray(right, np.int32)

    def scan_schedule(self, chunk):
        fb, bb, sq, fi, la, rf, rb = [], [], [], [], [], [], []
        ident = self.ld // chunk
        for s in range(self.n_seq):
            if s < self.bc:
                base, n = s * self.lc // chunk, self.lc // chunk
            else:
                base, n = (self.t_ctx + (s - self.bc) * self.ld) // chunk, self.ld // chunk
            for c in range(n):
                fb.append(base + c)
                bb.append(base + n - 1 - c)
                sq.append(s)
                fi.append(int(c == 0))
                la.append(int(c == n - 1))
                rf.append(ident if s < self.bc else c)
                rb.append(ident if s < self.bc else n - 1 - c)
        return [np.asarray(a, np.int32) for a in (fb, bb, sq, fi, la, rf, rb)]


def _mod_kernel(c_ref, w_ref, b_ref, o_ref):
    s = _silu(c_ref[...])
    o_ref[0] = _dot(s.astype(BF16), w_ref[0].astype(BF16)) + b_ref[0]


def _modulation(cond, w_mod, b_mod):
    depth, d, n = w_mod.shape
    nt = n // d
    return pl.pallas_call(
        _mod_kernel,
        grid=(depth, nt),
        in_specs=[pl.BlockSpec((COND_PAD, d), lambda l, j: (0, 0)),
                  pl.BlockSpec((1, d, d), lambda l, j: (l, 0, j)),
                  pl.BlockSpec((1, 1, d), lambda l, j: (l, 0, j))],
        out_specs=pl.BlockSpec((1, COND_PAD, d), lambda l, j: (l, 0, j)),
        out_shape=jax.ShapeDtypeStruct((depth, COND_PAD, n), F32),
        compiler_params=_cparams(2),
        name="modulation",
    )(cond, w_mod, b_mod.reshape(depth, 1, n))


def _two_source(x, tile):
    if isinstance(x, (tuple, list)):
        xa, xb = x
        n_first = xa.shape[0] // tile
    else:
        xa = xb = x
        n_first = x.shape[0] // tile
    d = xa.shape[1]
    specs = [pl.BlockSpec((tile, d), lambda i, *m: (jnp.minimum(i, n_first - 1), 0)),
             pl.BlockSpec((tile, d), lambda i, *m: (jnp.maximum(i - n_first, 0), 0))]
    return xa, xb, n_first, specs


def _read_two_source(xa_ref, xb_ref, n_first):
    return jnp.where(pl.program_id(0) < n_first, xa_ref[...], xb_ref[...])


def _inproj_kernel(cr_ref, xa_ref, xb_ref, g_ref, sc_ref, sh_ref, *refs, n_first):
    n = len(refs) // 2
    x = _read_two_source(xa_ref, xb_ref, n_first)
    h = _rms(x, g_ref[...]) * (1.0 + sc_ref[0]) + sh_ref[0]
    hb = h.astype(BF16)
    for w_ref, o_ref in zip(refs[:n], refs[n:]):
        o_ref[...] = _dot(hb, w_ref[...]).astype(o_ref.dtype)


def _inproj(lay, x, g, sc, sh, weights, out_dtypes):
    t, d = lay.t, g.shape[0]
    cr = lay.cond_rows(TM_PROJ)
    xa, xb, n_first, x_specs = _two_source(x, TM_PROJ)
    row = lambda i, cr: (i, 0)
    const = lambda i, cr: (0, 0)
    cond = lambda i, cr: (cr[i], 0, 0)
    in_specs = x_specs + [pl.BlockSpec((1, d), const), pl.BlockSpec((1, 1, d), cond), pl.BlockSpec((1, 1, d), cond)]
    in_specs += [pl.BlockSpec(w.shape, const) for w in weights]
    out_specs = [pl.BlockSpec((TM_PROJ, w.shape[1]), row) for w in weights]
    out_shape = [jax.ShapeDtypeStruct((t, w.shape[1]), dt) for w, dt in zip(weights, out_dtypes)]
    return pl.pallas_call(
        functools.partial(_inproj_kernel, n_first=n_first),
        grid_spec=pltpu.PrefetchScalarGridSpec(num_scalar_prefetch=1, grid=(t // TM_PROJ,),
                                               in_specs=in_specs, out_specs=out_specs),
        out_shape=out_shape,
        compiler_params=_cparams(1),
        name="inproj",
    )(cr, xa, xb, g.reshape(1, d), sc, sh, *weights)


def _rope(x, c, s):
    return x * c + pltpu.roll(x, RET_DK // 2, 1) * s


def _ret_kernel(fb_ref, bb_ref, sq_ref, fi_ref, la_ref, rf_ref, rb_ref,
                lg_ref, qf_ref, kf_ref, vf_ref, qb_ref, kb_ref, vb_ref,
                cf_ref, sf_ref, cb_ref, sb_ref, s0_ref,
                of_ref, ob_ref, sout_ref,
                st_ref, dm_ref, dec_ref):
    g = pl.program_id(0)
    c = TOK

    @pl.when(g == 0)
    def _():
        ii = lax.broadcasted_iota(I32, (c, c), 0)
        jj = lax.broadcasted_iota(I32, (c, c), 1)
        diff = (ii - jj).astype(F32)
        ri = lax.broadcasted_iota(I32, (c, RET_DK), 0).astype(F32)
        for h in range(RET_HEADS):
            lf = lg_ref[0, h]
            lb = lg_ref[1, h]
            low = jnp.exp(lf * jnp.maximum(diff, 0.0))
            up = jnp.exp(lb * jnp.maximum(-diff, 0.0))
            dm_ref[h] = jnp.where(diff > 0, low, jnp.where(diff < 0, up, 2.0))
            dec_ref[0, h] = jnp.exp(lf * (ri + 1.0))
            dec_ref[1, h] = jnp.exp(lf * (c - 1.0 - ri))
            dec_ref[2, h] = jnp.exp(lb * (c - ri))
            dec_ref[3, h] = jnp.exp(lb * ri)

    @pl.when(fi_ref[g] == 1)
    def _():
        st_ref[...] = s0_ref[0]

    scale = RET_DK ** -0.5
    cf, sf, cb, sb = cf_ref[...], sf_ref[...], cb_ref[...], sb_ref[...]
    zero_row = jnp.zeros((1, RET_DK), F32)
    for h in range(RET_HEADS):
        sl = slice(h * RET_DK, (h + 1) * RET_DK)
        lf = lg_ref[0, h]
        lb = lg_ref[1, h]
        q = _rope(qf_ref[:, sl].astype(F32), cf, sf)
        k = _rope(kf_ref[:, sl].astype(F32), cf, sf) * scale
        v = vf_ref[:, sl]
        s = _dot_nt(q.astype(BF16), k.astype(BF16)) * dm_ref[h]
        st = st_ref[0, h]
        o = _dot(s.astype(BF16), v) + _dot((q * dec_ref[0, h]).astype(BF16), st.astype(BF16))
        of_ref[:, sl] = o
        st_ref[0, h] = jnp.exp(zero_row + lf * c) * st + _dot_tn((k * dec_ref[1, h]).astype(BF16), v)
        q = _rope(qb_ref[:, sl].astype(F32), cb, sb)
        k = _rope(kb_ref[:, sl].astype(F32), cb, sb) * scale
        v = vb_ref[:, sl]
        st = st_ref[1, h]
        ob_ref[:, sl] = _dot((q * dec_ref[2, h]).astype(BF16), st.astype(BF16))
        st_ref[1, h] = jnp.exp(zero_row + lb * c) * st + _dot_tn((k * dec_ref[3, h]).astype(BF16), v)

    @pl.when(la_ref[g] == 1)
    def _():
        sout_ref[0] = st_ref[...]


def _retention(lay, proj, log_gamma, rope_c, rope_s, s0):
    t = proj.shape[0]
    sched = lay.scan_schedule(TOK)
    qcol, kcol, vcol = (2 * CONV_CH) // RET_W, (2 * CONV_CH) // RET_W + 1, (2 * CONV_CH) // RET_W + 2

    def tok(which, col):
        return pl.BlockSpec((TOK, RET_W), lambda g, *m: (m[which][g], col))

    def rope(which):
        return pl.BlockSpec((TOK, RET_DK), lambda g, *m: (m[which][g], 0))

    state = pl.BlockSpec((1, 2, RET_HEADS, RET_DK, RET_DK), lambda g, *m: (m[2][g], 0, 0, 0, 0))
    in_specs = [pl.BlockSpec(memory_space=pltpu.SMEM),
                tok(0, qcol), tok(0, kcol), tok(0, vcol), tok(1, qcol), tok(1, kcol), tok(1, vcol),
                rope(5), rope(5), rope(6), rope(6), state]
    out_specs = [pl.BlockSpec((TOK, RET_W), lambda g, *m: (m[0][g], 0)),
                 pl.BlockSpec((TOK, RET_W), lambda g, *m: (m[1][g], 0)), state]
    out_shape = [jax.ShapeDtypeStruct((t, RET_W), F32), jax.ShapeDtypeStruct((t, RET_W), F32),
                 jax.ShapeDtypeStruct(s0.shape, F32)]
    scratch = [pltpu.VMEM((2, RET_HEADS, RET_DK, RET_DK), F32),
               pltpu.VMEM((RET_HEADS, TOK, TOK), F32),
               pltpu.VMEM((4, RET_HEADS, TOK, RET_DK), F32)]
    return pl.pallas_call(
        _ret_kernel,
        grid_spec=pltpu.PrefetchScalarGridSpec(num_scalar_prefetch=7, grid=(len(sched[0]),),
                                               in_specs=in_specs, out_specs=out_specs,
                                               scratch_shapes=scratch),
        out_shape=out_shape,
        compiler_params=_cparams(1),
        name="retention",
    )(*sched, log_gamma, proj, proj, proj, proj, proj, proj, rope_c, rope_s, rope_c, rope_s, s0)


def _even_post_kernel(cr_ref, le_ref, re_ref,
                      glu_ref, prev_ref, next_ref, gt_ref, of_ref, ob_ref, xa_ref, xb_ref,
                      cw_ref, lng_ref, lnb_ref, wout_ref, npost_ref, g1_ref,
                      out_ref, buf_ref, *, n_first):
    i = pl.program_id(0)

    def glu(r):
        r = r.astype(F32)
        return r[:, :CONV_CH] * jax.nn.sigmoid(r[:, CONV_CH:])

    keep_l = jnp.where(le_ref[i] == 1, 0.0, 1.0)
    keep_r = jnp.where(re_ref[i] == 1, 0.0, 1.0)
    buf_ref[0:HALO, :] = glu(prev_ref[...]) * keep_l
    buf_ref[HALO:HALO + TOK, :] = glu(glu_ref[...])
    buf_ref[HALO + TOK:, :] = glu(next_ref[...]) * keep_r
    off = HALO - CONV_W // 2
    sub = 8
    cols = []
    for cb in range(CONV_CH // LANES):
        ls = slice(cb * LANES, (cb + 1) * LANES)
        acc = jnp.zeros((TOK, LANES), F32)
        for r in range(sub):
            part = None
            for m in range((off + CONV_W - 1) // sub + 1):
                j = sub * m + r - off
                if 0 <= j < CONV_W:
                    term = buf_ref[sub * m:sub * m + TOK + sub, ls] * cw_ref[j:j + 1, ls]
                    part = term if part is None else part + term
            acc = acc + part[r:r + TOK, :]
        cols.append(acc)
    acc = jnp.concatenate(cols, axis=1)
    mu = jnp.mean(acc, axis=-1, keepdims=True)
    ac = acc - mu
    y = ac * lax.rsqrt(jnp.mean(ac * ac, axis=-1, keepdims=True) + EPS) * lng_ref[...] + lnb_ref[...]
    conv_out = _silu(y)

    o = of_ref[...] + ob_ref[...]
    gt = gt_ref[...].astype(F32)
    parts = []
    for h in range(RET_HEADS):
        sl = slice(h * RET_DK, (h + 1) * RET_DK)
        oh = o[:, sl]
        oc = oh - jnp.mean(oh, axis=-1, keepdims=True)
        on = oc * lax.rsqrt(jnp.mean(oc * oc, axis=-1, keepdims=True) + EPS)
        parts.append(on * _silu(gt[:, sl]))
    ret_out = jnp.concatenate(parts, axis=1)

    out = _dot(conv_out.astype(BF16), wout_ref[0:CONV_CH, :]) + _dot(ret_out.astype(BF16), wout_ref[CONV_CH:, :])
    out_ref[...] = _read_two_source(xa_ref, xb_ref, n_first) + g1_ref[0] * _rms(out, npost_ref[...])


def _even_post(lay, proj, o_f, o_b, x, conv_w, ln_g, ln_b, w_out, n_post, g1):
    t, d = lay.t, n_post.shape[0]
    cr = lay.cond_rows(TOK)
    le, re = lay.edges(TOK)
    xa, xb, n_first, x_specs = _two_source(x, TOK)
    hb = TOK // HALO
    n_halo = t // HALO
    row = lambda i, *m: (i, 0)
    const = lambda i, *m: (0, 0)
    in_specs = [pl.BlockSpec((TOK, 2 * CONV_CH), row),
                pl.BlockSpec((HALO, 2 * CONV_CH), lambda i, *m: (jnp.maximum(i * hb - 1, 0), 0)),
                pl.BlockSpec((HALO, 2 * CONV_CH), lambda i, *m: (jnp.minimum((i + 1) * hb, n_halo - 1), 0)),
                pl.BlockSpec((TOK, RET_W), lambda i, *m: (i, EVEN_IN // RET_W - 1)),
                pl.BlockSpec((TOK, RET_W), row), pl.BlockSpec((TOK, RET_W), row)] + x_specs + [
                pl.BlockSpec(conv_w.shape, const), pl.BlockSpec((1, CONV_CH), const), pl.BlockSpec((1, CONV_CH), const),
                pl.BlockSpec(w_out.shape, const), pl.BlockSpec((1, d), const),
                pl.BlockSpec((1, 1, d), lambda i, *m: (m[0][i], 0, 0))]
    return pl.pallas_call(
        functools.partial(_even_post_kernel, n_first=n_first),
        grid_spec=pltpu.PrefetchScalarGridSpec(num_scalar_prefetch=3, grid=(t // TOK,),
                                               in_specs=in_specs, out_specs=pl.BlockSpec((TOK, d), row),
                                               scratch_shapes=[pltpu.VMEM((TOK + 2 * HALO, CONV_CH), F32)]),
        out_shape=jax.ShapeDtypeStruct((t, d), F32),
        compiler_params=_cparams(1),
        name="even_post",
    )(cr, le, re, proj, proj, proj, proj, o_f, o_b, xa, xb, conv_w, ln_g.reshape(1, -1), ln_b.reshape(1, -1),
      w_out, n_post.reshape(1, d), g1)


def _odd_pre_kernel(le_ref, re_ref, qkv_ref, prev_ref, next_ref, ab_ref, cw_ref, alog_ref, dtb_ref,
                    qkv_out_ref, gb_out_ref, buf_ref):
    i = pl.program_id(0)
    keep_l = jnp.where(le_ref[i] == 1, 0.0, 1.0)
    keep_r = jnp.where(re_ref[i] == 1, 0.0, 1.0)
    buf_ref[0:HALO, :] = prev_ref[...].astype(F32) * keep_l
    buf_ref[HALO:HALO + TOK, :] = qkv_ref[...].astype(F32)
    buf_ref[HALO + TOK:, :] = next_ref[...].astype(F32) * keep_r
    off = HALO - SHORT_W // 2
    for blk in range(3 * GDN_HEADS):
        sl = slice(blk * GDN_DK, (blk + 1) * GDN_DK)
        acc = jnp.zeros((TOK, GDN_DK), F32)
        for j in range(SHORT_W):
            acc = acc + buf_ref[off + j:off + j + TOK, sl] * cw_ref[j:j + 1, sl]
        y = _silu(acc)
        if blk < 2 * GDN_HEADS:
            y = y * lax.rsqrt(jnp.sum(y * y, axis=-1, keepdims=True) + EPS)
            if blk < GDN_HEADS:
                y = y * (GDN_DK ** -0.5)
        qkv_out_ref[:, sl] = y.astype(qkv_out_ref.dtype)
    ab = ab_ref[...]
    z = ab + dtb_ref[...]
    softplus = jnp.maximum(z, 0.0) + jnp.log(1.0 + jnp.exp(-jnp.abs(z)))
    gate = -jnp.exp(alog_ref[...]) * softplus
    beta = jax.nn.sigmoid(ab)
    lane = lax.broadcasted_iota(I32, ab.shape, 1)
    gb_out_ref[...] = jnp.where(lane < 2 * GDN_HEADS, gate, beta)


def _odd_pre(lay, proj, ab, conv_w, alog_row, dtb_row):
    t = proj.shape[0]
    w = 2 * GDN_KW + GDN_VW
    le, re = lay.edges(TOK)
    hb = TOK // HALO
    n_halo = t // HALO
    row = lambda i, *m: (i, 0)
    const = lambda i, *m: (0, 0)
    in_specs = [pl.BlockSpec((TOK, w), row),
                pl.BlockSpec((HALO, w), lambda i, *m: (jnp.maximum(i * hb - 1, 0), 0)),
                pl.BlockSpec((HALO, w), lambda i, *m: (jnp.minimum((i + 1) * hb, n_halo - 1), 0)),
                pl.BlockSpec((TOK, LANES), row),
                pl.BlockSpec(conv_w.shape, const), pl.BlockSpec((1, LANES), const), pl.BlockSpec((1, LANES), const)]
    out_specs = [pl.BlockSpec((TOK, w), row), pl.BlockSpec((TOK, LANES), row)]
    return pl.pallas_call(
        _odd_pre_kernel,
        grid_spec=pltpu.PrefetchScalarGridSpec(num_scalar_prefetch=2, grid=(t // TOK,),
                                               in_specs=in_specs, out_specs=out_specs,
                                               scratch_shapes=[pltpu.VMEM((TOK + 2 * HALO, w), F32)]),
        out_shape=[jax.ShapeDtypeStruct((t, w), BF16), jax.ShapeDtypeStruct((t, LANES), F32)],
        compiler_params=_cparams(1),
        name="odd_pre",
    )(le, re, proj, proj, proj, ab, conv_w, alog_row, dtb_row)


_GDN_LEVELS = tuple(2 ** p for p in range(int(math.log2(GDN_CHUNK))))


def _gdn_masks(msk_ref, tri_ref):
    c = GDN_CHUNK
    ii = lax.broadcasted_iota(I32, (c, c), 0)
    jj = lax.broadcasted_iota(I32, (c, c), 1)
    one = jnp.ones((c, c), F32)
    zero = jnp.zeros((c, c), F32)
    for rev in (0, 1):
        a, b = (ii, jj) if rev == 0 else (jj, ii)
        base = rev * 9
        msk_ref[base + 0] = jnp.where(a >= b, one, zero)
        msk_ref[base + 1] = jnp.where(a > b, one, zero)
        for l, m in enumerate(_GDN_LEVELS):
            sh = int(math.log2(m))
            ab_, bb_ = a >> sh, b >> sh
            hit = ((ab_ & 1) == 1) & (bb_ == ab_ - 1)
            msk_ref[base + 2 + l] = jnp.where(hit, one, zero)
        tri_ref[rev] = jnp.where(a >= b, one, zero).astype(BF16)


def _gdn_chunk_step(dirs, st_ref, msk_ref, tri_ref):
    c = GDN_CHUNK
    nh = GDN_HEADS
    probs = [(rev, h) for rev in range(2) for h in range(nh)]
    ones = jnp.ones((c, c), BF16)
    gcum, gcum_t, e_all, kdec_all, elast_all, gbs = [], [], [], [], [], []
    for rev in range(2):
        gb = dirs[rev][3][...]
        g1, g2, g3 = _split3(gb)
        tri = tri_ref[rev]
        gc = _dot(tri, g1) + _dot(tri, g2) + _dot(tri, g3)
        gl = _dot(ones, g1) + _dot(ones, g2) + _dot(ones, g3)
        gbs.append(gb)
        gcum.append(gc)
        gcum_t.append(gc.T)
        e_all.append(jnp.exp(gc))
        kdec_all.append(jnp.exp(gl - gc))
        elast_all.append(jnp.exp(gl))

    def col(rev, h):
        return rev * nh + h

    def hsl(h):
        return slice(h * GDN_DK, (h + 1) * GDN_DK)

    st = [st_ref[rev, h] for rev, h in probs]

    a, attn, kbs = [], [], []
    for rev, h in probs:
        base = rev * 9
        p = col(rev, h)
        q_ref, k_ref = dirs[rev][0], dirs[rev][1]
        k = k_ref[:, hsl(h)]
        gcb = jnp.broadcast_to(gcum[rev][:, p:p + 1], (c, c))
        grb = jnp.broadcast_to(gcum_t[rev][p:p + 1, :], (c, c))
        decay = jnp.exp(jnp.where(msk_ref[base] > 0, gcb - grb, NEG_BIG))
        kb = k.astype(F32) * gbs[rev][:, 2 * nh + p:2 * nh + p + 1]
        kbs.append(kb)
        a.append(_dot_nt(kb.astype(BF16), k) * decay * msk_ref[base + 1])
        attn.append((_dot_nt(q_ref[:, hsl(h)], k) * decay).astype(BF16))

    tinv = []
    for i, (rev, h) in enumerate(probs):
        base = rev * 9
        tinv.append(msk_ref[base] - msk_ref[base + 1] - a[i] * msk_ref[base + 2])
    for l in range(1, len(_GDN_LEVELS)):
        bt = []
        for i, (rev, h) in enumerate(probs):
            bt.append(_dot((a[i] * msk_ref[rev * 9 + 2 + l]).astype(BF16), tinv[i].astype(BF16)))
        for i in range(len(probs)):
            tinv[i] = tinv[i] - _dot(tinv[i].astype(BF16), bt[i].astype(BF16))

    sol = []
    for i, (rev, h) in enumerate(probs):
        p = col(rev, h)
        v = dirs[rev][2][:, hsl(h)].astype(F32)
        beta = gbs[rev][:, 2 * nh + p:2 * nh + p + 1]
        rhs = jnp.concatenate([v * beta, kbs[i] * e_all[rev][:, p:p + 1]], axis=1)
        sol.append(_dot(tinv[i].astype(BF16), rhs.astype(BF16)))
    v_new = []
    for i in range(len(probs)):
        u, w = sol[i][:, :GDN_DK], sol[i][:, GDN_DK:]
        v_new.append((u - _dot(w.astype(BF16), st[i].astype(BF16))).astype(BF16))
    for i, (rev, h) in enumerate(probs):
        p = col(rev, h)
        q = dirs[rev][0][:, hsl(h)].astype(F32)
        qs = (q * e_all[rev][:, p:p + 1]).astype(BF16)
        dirs[rev][4][:, hsl(h)] = _dot(qs, st[i].astype(BF16)) + _dot(attn[i], v_new[i])
    new_st = []
    for i, (rev, h) in enumerate(probs):
        p = col(rev, h)
        kd = (dirs[rev][1][:, hsl(h)].astype(F32) * kdec_all[rev][:, p:p + 1]).astype(BF16)
        new_st.append(st[i] * elast_all[rev][:, p:p + 1] + _dot_tn(kd, v_new[i]))
    for i, (rev, h) in enumerate(probs):
        st_ref[rev, h] = new_st[i]


def _gdn_kernel(fb_ref, bb_ref, sq_ref, fi_ref, la_ref,
                qf_ref, kf_ref, vf_ref, gf_ref, qb_ref, kb_ref, vb_ref, gbk_ref, s0_ref,
                of_ref, ob_ref, sout_ref,
                st_ref, msk_ref, tri_ref):
    g = pl.program_id(0)

    @pl.when(g == 0)
    def _():
        _gdn_masks(msk_ref, tri_ref)

    @pl.when(fi_ref[g] == 1)
    def _():
        st_ref[...] = s0_ref[0]

    _gdn_chunk_step(((qf_ref, kf_ref, vf_ref, gf_ref, of_ref), (qb_ref, kb_ref, vb_ref, gbk_ref, ob_ref)),
                    st_ref, msk_ref, tri_ref)

    @pl.when(la_ref[g] == 1)
    def _():
        sout_ref[0] = st_ref[...]


def _gdn(lay, qkv, gb, s0):
    t = qkv.shape[0]
    c = GDN_CHUNK
    sched = lay.scan_schedule(c)[:5]

    def tok(which, col):
        return pl.BlockSpec((c, GDN_KW), lambda g, *m: (m[which][g], col))

    def gate(which):
        return pl.BlockSpec((c, LANES), lambda g, *m: (m[which][g], 0))

    state = pl.BlockSpec((1, 2, GDN_HEADS, GDN_DK, GDN_DK), lambda g, *m: (m[2][g], 0, 0, 0, 0))
    in_specs = [tok(0, 0), tok(0, 1), tok(0, 2), gate(0), tok(1, 0), tok(1, 1), tok(1, 2), gate(1), state]
    out_specs = [pl.BlockSpec((c, GDN_VW), lambda g, *m: (m[0][g], 0)),
                 pl.BlockSpec((c, GDN_VW), lambda g, *m: (m[1][g], 0)), state]
    out_shape = [jax.ShapeDtypeStruct((t, GDN_VW), F32), jax.ShapeDtypeStruct((t, GDN_VW), F32),
                 jax.ShapeDtypeStruct(s0.shape, F32)]
    scratch = [pltpu.VMEM((2, GDN_HEADS, GDN_DK, GDN_DK), F32),
               pltpu.VMEM((18, c, c), F32), pltpu.VMEM((2, c, c), BF16)]
    return pl.pallas_call(
        _gdn_kernel,
        grid_spec=pltpu.PrefetchScalarGridSpec(num_scalar_prefetch=5, grid=(len(sched[0]),),
                                               in_specs=in_specs, out_specs=out_specs,
                                               scratch_shapes=scratch),
        out_shape=out_shape,
        compiler_params=_cparams(1),
        name="gdn_scan",
    )(*sched, qkv, qkv, qkv, gb, qkv, qkv, qkv, gb, s0)


def _odd_post_kernel(cr_ref, of_ref, ob_ref, z_ref, x_ref, nw_ref, wout_ref, npost_ref, g1_ref, out_ref):
    o = of_ref[...] + ob_ref[...]
    z = z_ref[...].astype(F32)
    nw = nw_ref[...]
    parts = []
    for h in range(GDN_HEADS):
        sl = slice(h * GDN_DK, (h + 1) * GDN_DK)
        parts.append(_rms(o[:, sl], nw) * _silu(z[:, sl]))
    y = jnp.concatenate(parts, axis=1)
    out = _dot(y.astype(BF16), wout_ref[...])
    out_ref[...] = x_ref[...] + g1_ref[0] * _rms(out, npost_ref[...])


def _odd_post(lay, o_f, o_b, proj, x, norm_w, w_out, n_post, g1):
    t, d = x.shape
    cr = lay.cond_rows(TOK)
    row = lambda i, *m: (i, 0)
    const = lambda i, *m: (0, 0)
    in_specs = [pl.BlockSpec((TOK, GDN_VW), row), pl.BlockSpec((TOK, GDN_VW), row),
                pl.BlockSpec((TOK, GDN_VW), lambda i, *m: (i, 3)),
                pl.BlockSpec((TOK, d), row),
                pl.BlockSpec((1, GDN_DK), const), pl.BlockSpec(w_out.shape, const), pl.BlockSpec((1, d), const),
                pl.BlockSpec((1, 1, d), lambda i, *m: (m[0][i], 0, 0))]
    return pl.pallas_call(
        _odd_post_kernel,
        grid_spec=pltpu.PrefetchScalarGridSpec(num_scalar_prefetch=1, grid=(t // TOK,),
                                               in_specs=in_specs, out_specs=pl.BlockSpec((TOK, d), row)),
        out_shape=jax.ShapeDtypeStruct((t, d), F32),
        compiler_params=_cparams(1),
        name="odd_post",
    )(cr, o_f, o_b, proj, x, norm_w.reshape(1, -1), w_out, n_post.reshape(1, d), g1)


SUB = D_MODEL // LANES


def _store_token_major(ref, x):
    n = x.shape[0]
    for s in range(SUB):
        ref[pl.ds(s, n, stride=SUB), :] = x[:, s * LANES:(s + 1) * LANES]


def _load_token_major(ref, n):
    return jnp.concatenate([ref[pl.ds(s, n, stride=SUB), :] for s in range(SUB)], axis=1)


def _router_kernel(cr_ref, x_ref, g_ref, sc_ref, sh_ref, rwt_ref, rb_ref,
                   h_ref, ids_ref, wts_ref, rank_ref, cnt_ref,
                   base_ref, su_ref):
    i = pl.program_id(0)
    tm = TM_ROUTE

    @pl.when(i == 0)
    def _():
        base_ref[...] = jnp.zeros_like(base_ref)
        ii = lax.broadcasted_iota(I32, (tm, tm), 0)
        jj = lax.broadcasted_iota(I32, (tm, tm), 1)
        su_ref[...] = jnp.where(ii < jj, 1.0, 0.0).astype(BF16)

    h = _rms(x_ref[...], g_ref[...]) * (1.0 + sc_ref[0]) + sh_ref[0]
    hb = h.astype(BF16)
    _store_token_major(h_ref, hb.astype(F32))
    h1 = hb
    w1, w2, w3 = _split3(rwt_ref[...])
    logits = _dot_nt(w1, h1) + _dot_nt(w2, h1) + _dot_nt(w3, h1)
    score = jax.nn.sigmoid(logits)
    sel = score + rb_ref[...]

    def row(a, e):
        return a[e:e + 1, :]

    gsum = []
    for gi in range(N_GROUPS):
        a, b, c, d = (row(sel, gi * EXP_PER_GROUP + j) for j in range(EXP_PER_GROUP))
        hi1, lo1 = jnp.maximum(a, b), jnp.minimum(a, b)
        hi2, lo2 = jnp.maximum(c, d), jnp.minimum(c, d)
        gsum.append(jnp.maximum(hi1, hi2) + jnp.maximum(jnp.minimum(hi1, hi2), jnp.maximum(lo1, lo2)))
    best = jnp.zeros_like(gsum[0]).astype(I32)
    cur = gsum[0]
    for gi in range(1, N_GROUPS):
        upd = gsum[gi] > cur
        best = jnp.where(upd, gi, best)
        cur = jnp.where(upd, gsum[gi], cur)

    def pick(arr, j):
        out = row(arr, j)
        for gi in range(1, N_GROUPS):
            out = jnp.where(best == gi, row(arr, gi * EXP_PER_GROUP + j), out)
        return out

    vals = [pick(sel, j) for j in range(EXP_PER_GROUP)]
    scs = [pick(score, j) for j in range(EXP_PER_GROUP)]

    def argmax_first(vs):
        idx = jnp.zeros_like(best)
        m = vs[0]
        for j in range(1, EXP_PER_GROUP):
            upd = vs[j] > m
            idx = jnp.where(upd, j, idx)
            m = jnp.where(upd, vs[j], m)
        return idx

    i1 = argmax_first(vals)
    vals2 = [jnp.where(i1 == j, -jnp.inf, vals[j]) for j in range(EXP_PER_GROUP)]
    i2 = argmax_first(vals2)

    def take(vs, idx):
        out = vs[0]
        for j in range(1, EXP_PER_GROUP):
            out = jnp.where(idx == j, vs[j], out)
        return out

    s1, s2 = take(scs, i1), take(scs, i2)
    tot = s1 + s2
    e1 = best * EXP_PER_GROUP + i1
    e2 = best * EXP_PER_GROUP + i2
    ids_ref[0:1, :] = e1
    ids_ref[1:2, :] = e2
    wts_ref[0:1, :] = s1 / tot
    wts_ref[1:2, :] = s2 / tot

    erow = lax.broadcasted_iota(I32, (N_EXPERTS, tm), 0)
    m1 = jnp.where(erow == e1, 1.0, 0.0)
    m2 = jnp.where(erow == e2, 1.0, 0.0)
    both = m1 + m2
    before = _dot(both.astype(BF16), su_ref[...]) + base_ref[...]
    rank_ref[0:1, :] = jnp.sum(m1 * before, axis=0, keepdims=True).astype(I32)
    rank_ref[1:2, :] = jnp.sum(m2 * before, axis=0, keepdims=True).astype(I32)
    base_ref[...] = base_ref[...] + jnp.sum(both, axis=1, keepdims=True)
    cnt_ref[...] = jnp.broadcast_to(base_ref[...], cnt_ref.shape).astype(I32)


def _router(lay, x, g, sc, sh, rwt, rbias):
    t, d = x.shape
    cr = lay.cond_rows(TM_ROUTE)
    tm = TM_ROUTE
    row = lambda i, cr: (i, 0)
    col = lambda i, cr: (0, i)
    const = lambda i, cr: (0, 0)
    cond = lambda i, cr: (cr[i], 0, 0)
    in_specs = [pl.BlockSpec((tm, d), row), pl.BlockSpec((1, d), const),
                pl.BlockSpec((1, 1, d), cond), pl.BlockSpec((1, 1, d), cond),
                pl.BlockSpec((N_EXPERTS, d), const), pl.BlockSpec((N_EXPERTS, 1), const)]
    out_specs = [pl.BlockSpec((tm * SUB, LANES), row), pl.BlockSpec((2, tm), col), pl.BlockSpec((2, tm), col),
                 pl.BlockSpec((2, tm), col), pl.BlockSpec((N_EXPERTS, LANES), const)]
    out_shape = [jax.ShapeDtypeStruct((t * SUB, LANES), F32), jax.ShapeDtypeStruct((2, t), I32),
                 jax.ShapeDtypeStruct((2, t), F32), jax.ShapeDtypeStruct((2, t), I32),
                 jax.ShapeDtypeStruct((N_EXPERTS, LANES), I32)]
    return pl.pallas_call(
        _router_kernel,
        grid_spec=pltpu.PrefetchScalarGridSpec(num_scalar_prefetch=1, grid=(t // tm,),
                                               in_specs=in_specs, out_specs=out_specs,
                                               scratch_shapes=[pltpu.VMEM((N_EXPERTS, 1), F32),
                                                               pltpu.VMEM((tm, tm), BF16)]),
        out_shape=out_shape,
        compiler_params=_cparams(1),
        name="router",
    )(cr, x, g.reshape(1, d), sc, sh, rwt, rbias.reshape(N_EXPERTS, 1))


def _row_copy(src, dst, sem):
    return pltpu.make_async_copy(src, dst, sem)


def _dispatch_kernel(ends_ref, padded_ref, pos_ref, h_ref, sorted_ref, zero_ref, sem, zsem):
    n = TOK

    @pl.when(pl.program_id(0) == 0)
    def _():
        zero_ref[...] = jnp.zeros_like(zero_ref)
        for e in range(N_EXPERTS):
            @pl.when(padded_ref[e] > 0)
            def _():
                start = pl.multiple_of((ends_ref[e] - TM_EXP) * SUB, SUB)
                fill = _row_copy(zero_ref, sorted_ref.at[pl.ds(start, TM_EXP * SUB)], zsem)
                fill.start()
                fill.wait()

    def issue(r, carry):
        src = h_ref.at[pl.ds(pl.multiple_of(r * SUB, SUB), SUB)]
        for k in range(2):
            dst = pl.multiple_of(pos_ref[0, 0, 2 * r + k] * SUB, SUB)
            _row_copy(src, sorted_ref.at[pl.ds(dst, SUB)], sem).start(priority=k)
        return carry

    lax.fori_loop(0, n, issue, 0, unroll=8)
    for _ in range(2):
        _row_copy(h_ref, sorted_ref.at[pl.ds(0, n * SUB)], sem).wait()


def _dispatch(h, pos3, ends, padded, n_rows):
    t = h.shape[0] // SUB
    return pl.pallas_call(
        _dispatch_kernel,
        grid_spec=pltpu.PrefetchScalarGridSpec(
            num_scalar_prefetch=2, grid=(t // TOK,),
            in_specs=[pl.BlockSpec((1, 1, 2 * TOK), lambda i, *m: (i, 0, 0), memory_space=pltpu.SMEM),
                      pl.BlockSpec((TOK * SUB, LANES), lambda i, *m: (i, 0))],
            out_specs=pl.BlockSpec(memory_space=pl.ANY),
            scratch_shapes=[pltpu.VMEM((TM_EXP * SUB, LANES), h.dtype),
                            pltpu.SemaphoreType.DMA(()), pltpu.SemaphoreType.DMA(())]),
        out_shape=jax.ShapeDtypeStruct((n_rows * SUB, LANES), h.dtype),
        compiler_params=_cparams(1),
        name="dispatch",
    )(ends, padded, pos3, h)


def _expert_kernel(te_ref, tv_ref, h_ref, wg_ref, wu_ref, wd_ref, y_ref, wgb_ref, wub_ref, wdb_ref):
    i = pl.program_id(0)

    @pl.when((i == 0) | (te_ref[i] != te_ref[jnp.maximum(i - 1, 0)]))
    def _():
        wgb_ref[...] = wg_ref[0, 0].astype(BF16)
        wub_ref[...] = wu_ref[0, 0].astype(BF16)
        wdb_ref[...] = wd_ref[0, 0].astype(BF16)

    @pl.when(tv_ref[i] == 1)
    def _():
        hb = _load_token_major(h_ref, TM_EXP).astype(BF16)
        a = _dot(hb, wgb_ref[...])
        b = _dot(hb, wub_ref[...])
        he = (_silu(a) * b).astype(BF16)
        _store_token_major(y_ref, _dot(he, wdb_ref[...]))

    @pl.when(tv_ref[i] == 0)
    def _():
        y_ref[...] = jnp.zeros_like(y_ref)


def _experts(hs, tile_expert, tile_valid, wg, wu, wd, layer):
    d = wg.shape[2]
    r = hs.shape[0] // SUB
    row = lambda i, te, tv: (i, 0)
    wsel = lambda i, te, tv: (layer, te[i], 0, 0)
    return pl.pallas_call(
        _expert_kernel,
        grid_spec=pltpu.PrefetchScalarGridSpec(
            num_scalar_prefetch=2, grid=(r // TM_EXP,),
            in_specs=[pl.BlockSpec((TM_EXP * SUB, LANES), lambda i, te, tv: (i * tv[i], 0)),
                      pl.BlockSpec((1, 1, d, D_EXPERT), wsel), pl.BlockSpec((1, 1, d, D_EXPERT), wsel),
                      pl.BlockSpec((1, 1, D_EXPERT, d), wsel)],
            out_specs=pl.BlockSpec((TM_EXP * SUB, LANES), row),
            scratch_shapes=[pltpu.VMEM((d, D_EXPERT), BF16), pltpu.VMEM((d, D_EXPERT), BF16),
                            pltpu.VMEM((D_EXPERT, d), BF16)]),
        out_shape=jax.ShapeDtypeStruct(hs.shape, F32),
        compiler_params=_cparams(1),
        name="experts",
    )(tile_expert, tile_valid, hs, wg, wu, wd)


def _combine_kernel(cr_ref, pos_ref, ys_ref, wts_ref, x_ref, npost_ref, g2_ref, *rest, n_first):
    n = TOK
    buf_ref, sem = rest[-2:]
    outs = rest[:-2]

    def issue(r, carry):
        row = pl.multiple_of(r * SUB, SUB)
        for k in range(2):
            src = pl.multiple_of(pos_ref[0, 0, 2 * r + k] * SUB, SUB)
            _row_copy(ys_ref.at[pl.ds(src, SUB)], buf_ref.at[k, pl.ds(row, SUB)], sem).start(priority=k)
        return carry

    lax.fori_loop(0, n, issue, 0, unroll=8)
    for slot in range(2):
        _row_copy(ys_ref.at[pl.ds(0, n * SUB)], buf_ref.at[slot], sem).wait()

    w = wts_ref[...]
    y = _load_token_major(buf_ref.at[0], n) * w[:, 0:1] + _load_token_major(buf_ref.at[1], n) * w[:, 1:2]
    res = x_ref[...] + g2_ref[0] * _rms(y, npost_ref[...])
    if n_first is None:
        outs[0][...] = res
    else:
        i = pl.program_id(0)

        @pl.when(i < n_first)
        def _():
            outs[0][...] = res

        @pl.when(i >= n_first)
        def _():
            outs[1][...] = res


def _combine(lay, ys3, pos3, wts, x, n_post, g2, split):
    t, d = x.shape
    cr = lay.cond_rows(TOK)
    row = lambda i, cr: (i, 0)
    const = lambda i, cr: (0, 0)
    in_specs = [pl.BlockSpec((1, 1, 2 * TOK), lambda i, cr: (i, 0, 0), memory_space=pltpu.SMEM),
                pl.BlockSpec(memory_space=pl.ANY),
                pl.BlockSpec((TOK, 2), row),
                pl.BlockSpec((TOK, d), row),
                pl.BlockSpec((1, d), const),
                pl.BlockSpec((1, 1, d), lambda i, cr: (cr[i], 0, 0))]
    if split:
        n_first = lay.t_ctx // TOK
        out_specs = [pl.BlockSpec((TOK, d), lambda i, cr: (jnp.minimum(i, n_first - 1), 0)),
                     pl.BlockSpec((TOK, d), lambda i, cr: (jnp.maximum(i - n_first, 0), 0))]
        out_shape = [jax.ShapeDtypeStruct((lay.t_ctx, d), F32), jax.ShapeDtypeStruct((t - lay.t_ctx, d), F32)]
    else:
        n_first = None
        out_specs = [pl.BlockSpec((TOK, d), row)]
        out_shape = [jax.ShapeDtypeStruct((t, d), F32)]
    return pl.pallas_call(
        functools.partial(_combine_kernel, n_first=n_first),
        grid_spec=pltpu.PrefetchScalarGridSpec(
            num_scalar_prefetch=1, grid=(t // TOK,), in_specs=in_specs, out_specs=out_specs,
            scratch_shapes=[pltpu.VMEM((2, TOK * SUB, LANES), F32), pltpu.SemaphoreType.DMA(())]),
        out_shape=out_shape,
        compiler_params=_cparams(1),
        name="combine",
    )(cr, pos3, ys3, wts, x, n_post.reshape(1, d), g2)


def _moe(lay, x, g_pre, sc, sh, g2, n_post, rwt, rbias, wg, wu, wd, layer, split):
    t, d = x.shape
    h, ids, wts, rank, cnt = _router(lay, x, g_pre, sc, sh, rwt, rbias)
    counts = cnt[:, 0]
    padded = ((counts + TM_EXP - 1) // TM_EXP) * TM_EXP
    ends = jnp.cumsum(padded)
    offs = ends - padded
    eids = jnp.arange(N_EXPERTS, dtype=I32)[:, None, None]
    pos = jnp.sum(jnp.where(ids[None] == eids, offs[:, None, None], 0), axis=0) + rank
    pos3 = pos.T.reshape(t // TOK, 1, 2 * TOK).astype(I32)
    n_rows = 2 * t + N_EXPERTS * TM_EXP
    n_tiles = n_rows // TM_EXP
    starts = jnp.arange(n_tiles, dtype=I32) * TM_EXP
    tile_expert = jnp.minimum(jnp.sum((starts[:, None] >= ends[None, :]).astype(I32), axis=1), N_EXPERTS - 1)
    tile_valid = (starts < ends[-1]).astype(I32)
    hs = _dispatch(h, pos3, ends.astype(I32), padded.astype(I32), n_rows)
    ys = _experts(hs, tile_expert.astype(I32), tile_valid, wg, wu, wd, layer)
    return _combine(lay, ys, pos3, wts.T, x, n_post, g2, split)


def _rope_tables(ld):
    rows = ld // GRID_W
    r = jnp.repeat(jnp.arange(rows, dtype=F32), GRID_W)
    col = jnp.tile(jnp.arange(GRID_W, dtype=F32), rows)
    quarter = RET_DK // 4
    inv = ROPE_BASE ** (-jnp.arange(quarter, dtype=F32) / quarter)
    ang = jnp.concatenate([r[:, None] * inv, col[:, None] * inv], axis=-1)
    cos, sin = jnp.cos(ang), jnp.sin(ang)
    c = jnp.concatenate([cos, cos], axis=-1)
    s = jnp.concatenate([-sin, sin], axis=-1)
    ident_c = jnp.ones((TOK, RET_DK), F32)
    ident_s = jnp.zeros((TOK, RET_DK), F32)
    return jnp.concatenate([c, ident_c], axis=0), jnp.concatenate([s, ident_s], axis=0)


def kernel(x_prompt, x_sample, state_ret, state_gdn, c, c_ctx, w_mod, b_mod, norm_mix_pre, norm_mix_post,
           norm_ffn_pre, norm_ffn_post, ev_w_in, ev_conv_w, ev_conv_ln_g, ev_conv_ln_b, ev_ret_decay, ev_w_out,
           od_w_in, od_conv_w, od_a_log, od_dt_bias, od_norm_w, od_w_out, router_w, router_bias,
           moe_w_gate, moe_w_up, moe_w_down):
    bc, lc, d = x_prompt.shape
    bd, ld, _ = x_sample.shape
    depth = w_mod.shape[0]
    lay = _Layout(bc, lc, bd, ld)
    t = lay.t

    x = (x_prompt.reshape(bc * lc, d), x_sample.reshape(bd * ld, d))
    cond = jnp.zeros((COND_PAD, d), F32).at[0].set(c_ctx).at[1:1 + bd].set(c)
    mod = _modulation(cond, w_mod, b_mod)
    mod = mod.reshape(depth, COND_PAD, N_MOD, 1, d).transpose(0, 2, 1, 3, 4)

    rope_c, rope_s = _rope_tables(ld)
    rwt = router_w.T
    ret_states, gdn_states = [], []
    for layer in range(depth):
        sh1, sc1, g1, sh2, sc2, g2 = (mod[layer, j] for j in range(N_MOD))
        i = layer // 2
        if layer % 2 == 0:
            (proj,) = _inproj(lay, x, norm_mix_pre[layer], sc1, sh1, [ev_w_in[i].astype(BF16)], [BF16])
            s0 = jnp.concatenate([jnp.zeros((bc,) + state_ret.shape[2:], F32), state_ret[:, i]], axis=0)
            log_gamma = -jnp.exp(ev_ret_decay[i].astype(F32))
            o_f, o_b, s_out = _retention(lay, proj, log_gamma, rope_c, rope_s, s0)
            ret_states.append(s_out[:bc])
            cw = jnp.zeros((32, CONV_CH), F32).at[:CONV_W].set(ev_conv_w[i])
            x = _even_post(lay, proj, o_f, o_b, x, cw, ev_conv_ln_g[i], ev_conv_ln_b[i],
                           ev_w_out[i].astype(BF16), norm_mix_post[layer], g1)
        else:
            w_in = od_w_in[i]
            n_main = 2 * GDN_KW + 2 * GDN_VW
            w_small = jnp.zeros((d, LANES), F32).at[:, :4 * GDN_HEADS].set(w_in[:, n_main:])
            proj, ab = _inproj(lay, x, norm_mix_pre[layer], sc1, sh1,
                               [w_in[:, :n_main].astype(BF16), w_small.astype(BF16)], [BF16, F32])
            cw = jnp.zeros((8, 2 * GDN_KW + GDN_VW), F32).at[:SHORT_W].set(od_conv_w[i])
            alog_row = jnp.zeros((1, LANES), F32).at[0, :2 * GDN_HEADS].set(od_a_log[i].reshape(-1))
            dtb_row = jnp.zeros((1, LANES), F32).at[0, :2 * GDN_HEADS].set(od_dt_bias[i].reshape(-1))
            qkv, gb = _odd_pre(lay, proj, ab, cw, alog_row, dtb_row)
            s0 = jnp.concatenate([jnp.zeros((bc,) + state_gdn.shape[2:], F32), state_gdn[:, i]], axis=0)
            o_f, o_b, s_out = _gdn(lay, qkv, gb, s0)
            gdn_states.append(s_out[:bc])
            x = _odd_post(lay, o_f, o_b, proj, x, od_norm_w[i], od_w_out[i].astype(BF16),
                          norm_mix_post[layer], g1)
        outs = _moe(lay, x, norm_ffn_pre[layer], sc2, sh2, g2, norm_ffn_post[layer], rwt, router_bias,
                    moe_w_gate, moe_w_up, moe_w_down, layer, split=(layer == depth - 1))
        x = outs[0]

    y_prompt = outs[0].reshape(bc, lc, d)
    y_sample = outs[1].reshape(bd, ld, d)
    new_ret = jnp.stack(ret_states, axis=1)
    new_gdn = jnp.stack(gdn_states, axis=1)
    return y_prompt, y_sample, new_ret.astype(x_prompt.dtype), new_gdn.astype(x_prompt.dtype)
```

```python
import functools
import math

import jax
import jax.numpy as jnp
import numpy as np
from jax import lax
from jax.experimental import pallas as pl
from jax.experimental.pallas import tpu as pltpu

F32 = jnp.float32
BF16 = jnp.bfloat16
I32 = jnp.int32

D_MODEL = 1024
N_MOD = 6
EPS = 1e-6
GRID_W = 64
CONV_CH = 512
CONV_W = 31
RET_HEADS = 4
RET_DK = 128
RET_W = 512
ROPE_BASE = 10000.0
EVEN_IN = 2 * CONV_CH + 4 * RET_W
GDN_HEADS = 8
GDN_DK = 128
GDN_KW = 1024
GDN_VW = 1024
SHORT_W = 5
N_EXPERTS = 16
N_GROUPS = 4
EXP_PER_GROUP = 4
D_EXPERT = 512

LANES = 128
TOK = 256
TM_PROJ = 512
TM_ROUTE = 512
TM_EXP = 256
GDN_CHUNK = 128
HALO = 16
COND_PAD = 16
VMEM_LIMIT = 56 * 1024 * 1024
NEG_BIG = -1e30


def _cparams(n_axes=1, vmem=VMEM_LIMIT):
    return pltpu.CompilerParams(dimension_semantics=("arbitrary",) * n_axes, vmem_limit_bytes=vmem)


def _silu(x):
    return x * jax.nn.sigmoid(x)


def _rms(x, g):
    return x * lax.rsqrt(jnp.mean(x * x, axis=-1, keepdims=True) + EPS) * g


def _dot(a, b):
    return jnp.dot(a, b, preferred_element_type=F32)


def _dot_nt(a, b):
    return lax.dot_general(a, b, (((1,), (1,)), ((), ())), preferred_element_type=F32)


def _dot_tn(a, b):
    return lax.dot_general(a, b, (((0,), (0,)), ((), ())), preferred_element_type=F32)


def _split3(x):
    x1 = x.astype(BF16)
    r = x - x1.astype(F32)
    x2 = r.astype(BF16)
    x3 = (r - x2.astype(F32)).astype(BF16)
    return x1, x2, x3


class _Layout:
    def __init__(self, bc, lc, bd, ld):
        self.bc, self.lc, self.bd, self.ld = bc, lc, bd, ld
        self.t_ctx = bc * lc
        self.t = bc * lc + bd * ld
        self.n_seq = bc + bd
        assert lc % TOK == 0 and ld % TOK == 0 and self.t_ctx % TM_PROJ == 0 and ld % TM_PROJ == 0
        assert self.t % TM_ROUTE == 0 and lc % GDN_CHUNK == 0 and ld % GDN_CHUNK == 0

    def seq_of_row(self, r):
        if r < self.t_ctx:
            return r // self.lc, r % self.lc, self.lc
        r2 = r - self.t_ctx
        return self.bc + r2 // self.ld, r2 % self.ld, self.ld

    def cond_rows(self, tile):
        out = []
        for i in range(self.t // tile):
            s, _, _ = self.seq_of_row(i * tile)
            out.append(0 if s < self.bc else 1 + s - self.bc)
        return np.asarray(out, np.int32)

    def edges(self, tile):
        left, right = [], []
        for i in range(self.t // tile):
            _, p, l = self.seq_of_row(i * tile)
            left.append(int(p == 0))
            right.append(int(p + tile == l))
        return np.asarray(left, np.int32), np.asarray(right, np.int32)

    def scan_schedule(self, chunk):
        fb, bb, sq, fi, la, rf, rb = [], [], [], [], [], [], []
        ident = self.ld // chunk
        for s in range(self.n_seq):
            if s < self.bc:
                base, n = s * self.lc // chunk, self.lc // chunk
            else:
                base, n = (self.t_ctx + (s - self.bc) * self.ld) // chunk, self.ld // chunk
            for c in range(n):
                fb.append(base + c)
                bb.append(base + n - 1 - c)
                sq.append(s)
                fi.append(int(c == 0))
                la.append(int(c == n - 1))
                rf.append(ident if s < self.bc else c)
                rb.append(ident if s < self.bc else n - 1 - c)
        return [np.asarray(a, np.int32) for a in (fb, bb, sq, fi, la, rf, rb)]


def _mod_kernel(c_ref, w_ref, b_ref, o_ref):
    s = _silu(c_ref[...])
    o_ref[0] = _dot(s.astype(BF16), w_ref[0].astype(BF16)) + b_ref[0]


def _modulation(cond, w_mod, b_mod):
    depth, d, n = w_mod.shape
    nt = n // d
    return pl.pallas_call(
        _mod_kernel,
        grid=(depth, nt),
        in_specs=[pl.BlockSpec((COND_PAD, d), lambda l, j: (0, 0)),
                  pl.BlockSpec((1, d, d), lambda l, j: (l, 0, j)),
                  pl.BlockSpec((1, 1, d), lambda l, j: (l, 0, j))],
        out_specs=pl.BlockSpec((1, COND_PAD, d), lambda l, j: (l, 0, j)),
        out_shape=jax.ShapeDtypeStruct((depth, COND_PAD, n), F32),
        compiler_params=_cparams(2),
        name="modulation",
    )(cond, w_mod, b_mod.reshape(depth, 1, n))


def _two_source(x, tile):
    if isinstance(x, (tuple, list)):
        xa, xb = x
        n_first = xa.shape[0] // tile
    else:
        xa = xb = x
        n_first = x.shape[0] // tile
    d = xa.shape[1]
    specs = [pl.BlockSpec((tile, d), lambda i, *m: (jnp.minimum(i, n_first - 1), 0)),
             pl.BlockSpec((tile, d), lambda i, *m: (jnp.maximum(i - n_first, 0), 0))]
    return xa, xb, n_first, specs


def _read_two_source(xa_ref, xb_ref, n_first):
    return jnp.where(pl.program_id(0) < n_first, xa_ref[...], xb_ref[...])


def _inproj_kernel(cr_ref, xa_ref, xb_ref, g_ref, sc_ref, sh_ref, *refs, n_first):
    n = len(refs) // 2
    x = _read_two_source(xa_ref, xb_ref, n_first)
    h = _rms(x, g_ref[...]) * (1.0 + sc_ref[0]) + sh_ref[0]
    hb = h.astype(BF16)
    for w_ref, o_ref in zip(refs[:n], refs[n:]):
        o_ref[...] = _dot(hb, w_ref[...]).astype(o_ref.dtype)


def _inproj(lay, x, g, sc, sh, weights, out_dtypes):
    t, d = lay.t, g.shape[0]
    cr = lay.cond_rows(TM_PROJ)
    xa, xb, n_first, x_specs = _two_source(x, TM_PROJ)
    row = lambda i, cr: (i, 0)
    const = lambda i, cr: (0, 0)
    cond = lambda i, cr: (cr[i], 0, 0)
    in_specs = x_specs + [pl.BlockSpec((1, d), const), pl.BlockSpec((1, 1, d), cond), pl.BlockSpec((1, 1, d), cond)]
    in_specs += [pl.BlockSpec(w.shape, const) for w in weights]
    out_specs = [pl.BlockSpec((TM_PROJ, w.shape[1]), row) for w in weights]
    out_shape = [jax.ShapeDtypeStruct((t, w.shape[1]), dt) for w, dt in zip(weights, out_dtypes)]
    return pl.pallas_call(
        functools.partial(_inproj_kernel, n_first=n_first),
        grid_spec=pltpu.PrefetchScalarGridSpec(num_scalar_prefetch=1, grid=(t // TM_PROJ,),
                                               in_specs=in_specs, out_specs=out_specs),
        out_shape=out_shape,
        compiler_params=_cparams(1),
        name="inproj",
    )(cr, xa, xb, g.reshape(1, d), sc, sh, *weights)


def _rope(x, c, s):
    return x * c + pltpu.roll(x, RET_DK // 2, 1) * s


def _ret_kernel(fb_ref, bb_ref, sq_ref, fi_ref, la_ref, rf_ref, rb_ref,
                lg_ref, qf_ref, kf_ref, vf_ref, qb_ref, kb_ref, vb_ref,
                cf_ref, sf_ref, cb_ref, sb_ref, s0_ref,
                of_ref, ob_ref, sout_ref,
                st_ref, dm_ref, dec_ref):
    g = pl.program_id(0)
    c = TOK

    @pl.when(g == 0)
    def _():
        ii = lax.broadcasted_iota(I32, (c, c), 0)
        jj = lax.broadcasted_iota(I32, (c, c), 1)
        diff = (ii - jj).astype(F32)
        ri = lax.broadcasted_iota(I32, (c, RET_DK), 0).astype(F32)
        for h in range(RET_HEADS):
            lf = lg_ref[0, h]
            lb = lg_ref[1, h]
            low = jnp.exp(lf * jnp.maximum(diff, 0.0))
            up = jnp.exp(lb * jnp.maximum(-diff, 0.0))
            dm_ref[h] = jnp.where(diff > 0, low, jnp.where(diff < 0, up, 2.0))
            dec_ref[0, h] = jnp.exp(lf * (ri + 1.0))
            dec_ref[1, h] = jnp.exp(lf * (c - 1.0 - ri))
            dec_ref[2, h] = jnp.exp(lb * (c - ri))
            dec_ref[3, h] = jnp.exp(lb * ri)

    @pl.when(fi_ref[g] == 1)
    def _():
        st_ref[...] = s0_ref[0]

    scale = RET_DK ** -0.5
    cf, sf, cb, sb = cf_ref[...], sf_ref[...], cb_ref[...], sb_ref[...]
    zero_row = jnp.zeros((1, RET_DK), F32)
    for h in range(RET_HEADS):
        sl = slice(h * RET_DK, (h + 1) * RET_DK)
        lf = lg_ref[0, h]
        lb = lg_ref[1, h]
        q = _rope(qf_ref[:, sl].astype(F32), cf, sf)
        k = _rope(kf_ref[:, sl].astype(F32), cf, sf) * scale
        v = vf_ref[:, sl]
        s = _dot_nt(q.astype(BF16), k.astype(BF16)) * dm_ref[h]
        st = st_ref[0, h]
        o = _dot(s.astype(BF16), v) + _dot((q * dec_ref[0, h]).astype(BF16), st.astype(BF16))
        of_ref[:, sl] = o
        st_ref[0, h] = jnp.exp(zero_row + lf * c) * st + _dot_tn((k * dec_ref[1, h]).astype(BF16), v)
        q = _rope(qb_ref[:, sl].astype(F32), cb, sb)
        k = _rope(kb_ref[:, sl].astype(F32), cb, sb) * scale
        v = vb_ref[:, sl]
        st = st_ref[1, h]
        ob_ref[:, sl] = _dot((q * dec_ref[2, h]).astype(BF16), st.astype(BF16))
        st_ref[1, h] = jnp.exp(zero_row + lb * c) * st + _dot_tn((k * dec_ref[3, h]).astype(BF16), v)

    @pl.when(la_ref[g] == 1)
    def _():
        sout_ref[0] = st_ref[...]


def _retention(lay, proj, log_gamma, rope_c, rope_s, s0):
    t = proj.shape[0]
    sched = lay.scan_schedule(TOK)
    qcol, kcol, vcol = (2 * CONV_CH) // RET_W, (2 * CONV_CH) // RET_W + 1, (2 * CONV_CH) // RET_W + 2

    def tok(which, col):
        return pl.BlockSpec((TOK, RET_W), lambda g, *m: (m[which][g], col))

    def rope(which):
        return pl.BlockSpec((TOK, RET_DK), lambda g, *m: (m[which][g], 0))

    state = pl.BlockSpec((1, 2, RET_HEADS, RET_DK, RET_DK), lambda g, *m: (m[2][g], 0, 0, 0, 0))
    in_specs = [pl.BlockSpec(memory_space=pltpu.SMEM),
                tok(0, qcol), tok(0, kcol), tok(0, vcol), tok(1, qcol), tok(1, kcol), tok(1, vcol),
                rope(5), rope(5), rope(6), rope(6), state]
    out_specs = [pl.BlockSpec((TOK, RET_W), lambda g, *m: (m[0][g], 0)),
                 pl.BlockSpec((TOK, RET_W), lambda g, *m: (m[1][g], 0)), state]
    out_shape = [jax.ShapeDtypeStruct((t, RET_W), F32), jax.ShapeDtypeStruct((t, RET_W), F32),
                 jax.ShapeDtypeStruct(s0.shape, F32)]
    scratch = [pltpu.VMEM((2, RET_HEADS, RET_DK, RET_DK), F32),
               pltpu.VMEM((RET_HEADS, TOK, TOK), F32),
               pltpu.VMEM((4, RET_HEADS, TOK, RET_DK), F32)]
    return pl.pallas_call(
        _ret_kernel,
        grid_spec=pltpu.PrefetchScalarGridSpec(num_scalar_prefetch=7, grid=(len(sched[0]),),
                                               in_specs=in_specs, out_specs=out_specs,
                                               scratch_shapes=scratch),
        out_shape=out_shape,
        compiler_params=_cparams(1),
        name="retention",
    )(*sched, log_gamma, proj, proj, proj, proj, proj, proj, rope_c, rope_s, rope_c, rope_s, s0)


def _even_post_kernel(cr_ref, le_ref, re_ref,
                      glu_ref, prev_ref, next_ref, gt_ref, of_ref, ob_ref, xa_ref, xb_ref,
                      cw_ref, lng_ref, lnb_ref, wout_ref, npost_ref, g1_ref,
                      out_ref, buf_ref, *, n_first):
    i = pl.program_id(0)

    def glu(r):
        r = r.astype(F32)
        return r[:, :CONV_CH] * jax.nn.sigmoid(r[:, CONV_CH:])

    keep_l = jnp.where(le_ref[i] == 1, 0.0, 1.0)
    keep_r = jnp.where(re_ref[i] == 1, 0.0, 1.0)
    buf_ref[0:HALO, :] = glu(prev_ref[...]) * keep_l
    buf_ref[HALO:HALO + TOK, :] = glu(glu_ref[...])
    buf_ref[HALO + TOK:, :] = glu(next_ref[...]) * keep_r
    off = HALO - CONV_W // 2
    sub = 8
    cols = []
    for cb in range(CONV_CH // LANES):
        ls = slice(cb * LANES, (cb + 1) * LANES)
        acc = jnp.zeros((TOK, LANES), F32)
        for r in range(sub):
            part = None
            for m in range((off + CONV_W - 1) // sub + 1):
                j = sub * m + r - off
                if 0 <= j < CONV_W:
                    term = buf_ref[sub * m:sub * m + TOK + sub, ls] * cw_ref[j:j + 1, ls]
                    part = term if part is None else part + term
            acc = acc + part[r:r + TOK, :]
        cols.append(acc)
    acc = jnp.concatenate(cols, axis=1)
    mu = jnp.mean(acc, axis=-1, keepdims=True)
    ac = acc - mu
    y = ac * lax.rsqrt(jnp.mean(ac * ac, axis=-1, keepdims=True) + EPS) * lng_ref[...] + lnb_ref[...]
    conv_out = _silu(y)

    o = of_ref[...] + ob_ref[...]
    gt = gt_ref[...].astype(F32)
    parts = []
    for h in range(RET_HEADS):
        sl = slice(h * RET_DK, (h + 1) * RET_DK)
        oh = o[:, sl]
        oc = oh - jnp.mean(oh, axis=-1, keepdims=True)
        on = oc * lax.rsqrt(jnp.mean(oc * oc, axis=-1, keepdims=True) + EPS)
        parts.append(on * _silu(gt[:, sl]))
    ret_out = jnp.concatenate(parts, axis=1)

    out = _dot(conv_out.astype(BF16), wout_ref[0:CONV_CH, :]) + _dot(ret_out.astype(BF16), wout_ref[CONV_CH:, :])
    out_ref[...] = _read_two_source(xa_ref, xb_ref, n_first) + g1_ref[0] * _rms(out, npost_ref[...])


def _even_post(lay, proj, o_f, o_b, x, conv_w, ln_g, ln_b, w_out, n_post, g1):
    t, d = lay.t, n_post.shape[0]
    cr = lay.cond_rows(TOK)
    le, re = lay.edges(TOK)
    xa, xb, n_first, x_specs = _two_source(x, TOK)
    hb = TOK // HALO
    n_halo = t // HALO
    row = lambda i, *m: (i, 0)
    const = lambda i, *m: (0, 0)
    in_specs = [pl.BlockSpec((TOK, 2 * CONV_CH), row),
                pl.BlockSpec((HALO, 2 * CONV_CH), lambda i, *m: (jnp.maximum(i * hb - 1, 0), 0)),
                pl.BlockSpec((HALO, 2 * CONV_CH), lambda i, *m: (jnp.minimum((i + 1) * hb, n_halo - 1), 0)),
                pl.BlockSpec((TOK, RET_W), lambda i, *m: (i, EVEN_IN // RET_W - 1)),
                pl.BlockSpec((TOK, RET_W), row), pl.BlockSpec((TOK, RET_W), row)] + x_specs + [
                pl.BlockSpec(conv_w.shape, const), pl.BlockSpec((1, CONV_CH), const), pl.BlockSpec((1, CONV_CH), const),
                pl.BlockSpec(w_out.shape, const), pl.BlockSpec((1, d), const),
                pl.BlockSpec((1, 1, d), lambda i, *m: (m[0][i], 0, 0))]
    return pl.pallas_call(
        functools.partial(_even_post_kernel, n_first=n_first),
        grid_spec=pltpu.PrefetchScalarGridSpec(num_scalar_prefetch=3, grid=(t // TOK,),
                                               in_specs=in_specs, out_specs=pl.BlockSpec((TOK, d), row),
                                               scratch_shapes=[pltpu.VMEM((TOK + 2 * HALO, CONV_CH), F32)]),
        out_shape=jax.ShapeDtypeStruct((t, d), F32),
        compiler_params=_cparams(1),
        name="even_post",
    )(cr, le, re, proj, proj, proj, proj, o_f, o_b, xa, xb, conv_w, ln_g.reshape(1, -1), ln_b.reshape(1, -1),
      w_out, n_post.reshape(1, d), g1)


def _odd_pre_kernel(le_ref, re_ref, qkv_ref, prev_ref, next_ref, ab_ref, cw_ref, alog_ref, dtb_ref,
                    qkv_out_ref, gb_out_ref, buf_ref):
    i = pl.program_id(0)
    keep_l = jnp.where(le_ref[i] == 1, 0.0, 1.0)
    keep_r = jnp.where(re_ref[i] == 1, 0.0, 1.0)
    buf_ref[0:HALO, :] = prev_ref[...].astype(F32) * keep_l
    buf_ref[HALO:HALO + TOK, :] = qkv_ref[...].astype(F32)
    buf_ref[HALO + TOK:, :] = next_ref[...].astype(F32) * keep_r
    off = HALO - SHORT_W // 2
    for blk in range(3 * GDN_HEADS):
        sl = slice(blk * GDN_DK, (blk + 1) * GDN_DK)
        acc = jnp.zeros((TOK, GDN_DK), F32)
        for j in range(SHORT_W):
            acc = acc + buf_ref[off + j:off + j + TOK, sl] * cw_ref[j:j + 1, sl]
        y = _silu(acc)
        if blk < 2 * GDN_HEADS:
            y = y * lax.rsqrt(jnp.sum(y * y, axis=-1, keepdims=True) + EPS)
            if blk < GDN_HEADS:
                y = y * (GDN_DK ** -0.5)
        qkv_out_ref[:, sl] = y.astype(qkv_out_ref.dtype)
    ab = ab_ref[...]
    z = ab + dtb_ref[...]
    softplus = jnp.maximum(z, 0.0) + jnp.log(1.0 + jnp.exp(-jnp.abs(z)))
    gate = -jnp.exp(alog_ref[...]) * softplus
    beta = jax.nn.sigmoid(ab)
    lane = lax.broadcasted_iota(I32, ab.shape, 1)
    gb_out_ref[...] = jnp.where(lane < 2 * GDN_HEADS, gate, beta)


def _odd_pre(lay, proj, ab, conv_w, alog_row, dtb_row):
    t = proj.shape[0]
    w = 2 * GDN_KW + GDN_VW
    le, re = lay.edges(TOK)
    hb = TOK // HALO
    n_halo = t // HALO
    row = lambda i, *m: (i, 0)
    const = lambda i, *m: (0, 0)
    in_specs = [pl.BlockSpec((TOK, w), row),
                pl.BlockSpec((HALO, w), lambda i, *m: (jnp.maximum(i * hb - 1, 0), 0)),
                pl.BlockSpec((HALO, w), lambda i, *m: (jnp.minimum((i + 1) * hb, n_halo - 1), 0)),
                pl.BlockSpec((TOK, LANES), row),
                pl.BlockSpec(conv_w.shape, const), pl.BlockSpec((1, LANES), const), pl.BlockSpec((1, LANES), const)]
    out_specs = [pl.BlockSpec((TOK, w), row), pl.BlockSpec((TOK, LANES), row)]
    return pl.pallas_call(
        _odd_pre_kernel,
        grid_spec=pltpu.PrefetchScalarGridSpec(num_scalar_prefetch=2, grid=(t // TOK,),
                                               in_specs=in_specs, out_specs=out_specs,
                                               scratch_shapes=[pltpu.VMEM((TOK + 2 * HALO, w), F32)]),
        out_shape=[jax.ShapeDtypeStruct((t, w), BF16), jax.ShapeDtypeStruct((t, LANES), F32)],
        compiler_params=_cparams(1),
        name="odd_pre",
    )(le, re, proj, proj, proj, ab, conv_w, alog_row, dtb_row)


_GDN_LEVELS = tuple(2 ** p for p in range(int(math.log2(GDN_CHUNK))))


def _gdn_masks(msk_ref, tri_ref):
    c = GDN_CHUNK
    ii = lax.broadcasted_iota(I32, (c, c), 0)
    jj = lax.broadcasted_iota(I32, (c, c), 1)
    one = jnp.ones((c, c), F32)
    zero = jnp.zeros((c, c), F32)
    for rev in (0, 1):
        a, b = (ii, jj) if rev == 0 else (jj, ii)
        base = rev * 9
        msk_ref[base + 0] = jnp.where(a >= b, one, zero)
        msk_ref[base + 1] = jnp.where(a > b, one, zero)
        for l, m in enumerate(_GDN_LEVELS):
            sh = int(math.log2(m))
            ab_, bb_ = a >> sh, b >> sh
            hit = ((ab_ & 1) == 1) & (bb_ == ab_ - 1)
            msk_ref[base + 2 + l] = jnp.where(hit, one, zero)
        tri_ref[rev] = jnp.where(a >= b, one, zero).astype(BF16)


def _gdn_chunk_step(dirs, st_ref, msk_ref, tri_ref):
    c = GDN_CHUNK
    nh = GDN_HEADS
    probs = [(rev, h) for rev in range(2) for h in range(nh)]
    ones = jnp.ones((c, c), BF16)
    gcum, gcum_t, e_all, kdec_all, elast_all, gbs = [], [], [], [], [], []
    for rev in range(2):
        gb = dirs[rev][3][...]
        g1, g2, g3 = _split3(gb)
        tri = tri_ref[rev]
        gc = _dot(tri, g1) + _dot(tri, g2) + _dot(tri, g3)
        gl = _dot(ones, g1) + _dot(ones, g2) + _dot(ones, g3)
        gbs.append(gb)
        gcum.append(gc)
        gcum_t.append(gc.T)
        e_all.append(jnp.exp(gc))
        kdec_all.append(jnp.exp(gl - gc))
        elast_all.append(jnp.exp(gl))

    def col(rev, h):
        return rev * nh + h

    def hsl(h):
        return slice(h * GDN_DK, (h + 1) * GDN_DK)

    st = [st_ref[rev, h] for rev, h in probs]

    a, attn, kbs = [], [], []
    for rev, h in probs:
        base = rev * 9
        p = col(rev, h)
        q_ref, k_ref = dirs[rev][0], dirs[rev][1]
        k = k_ref[:, hsl(h)]
        gcb = jnp.broadcast_to(gcum[rev][:, p:p + 1], (c, c))
        grb = jnp.broadcast_to(gcum_t[rev][p:p + 1, :], (c, c))
        decay = jnp.exp(jnp.where(msk_ref[base] > 0, gcb - grb, NEG_BIG))
        kb = k.astype(F32) * gbs[rev][:, 2 * nh + p:2 * nh + p + 1]
        kbs.append(kb)
        a.append(_dot_nt(kb.astype(BF16), k) * decay * msk_ref[base + 1])
        attn.append((_dot_nt(q_ref[:, hsl(h)], k) * decay).astype(BF16))

    tinv = []
    for i, (rev, h) in enumerate(probs):
        base = rev * 9
        tinv.append(msk_ref[base] - msk_ref[base + 1] - a[i] * msk_ref[base + 2])
    for l in range(1, len(_GDN_LEVELS)):
        bt = []
        for i, (rev, h) in enumerate(probs):
            bt.append(_dot((a[i] * msk_ref[rev * 9 + 2 + l]).astype(BF16), tinv[i].astype(BF16)))
        for i in range(len(probs)):
            tinv[i] = tinv[i] - _dot(tinv[i].astype(BF16), bt[i].astype(BF16))

    sol = []
    for i, (rev, h) in enumerate(probs):
        p = col(rev, h)
        v = dirs[rev][2][:, hsl(h)].astype(F32)
        beta = gbs[rev][:, 2 * nh + p:2 * nh + p + 1]
        rhs = jnp.concatenate([v * beta, kbs[i] * e_all[rev][:, p:p + 1]], axis=1)
        sol.append(_dot(tinv[i].astype(BF16), rhs.astype(BF16)))
    v_new = []
    for i in range(len(probs)):
        u, w = sol[i][:, :GDN_DK], sol[i][:, GDN_DK:]
        v_new.append((u - _dot(w.astype(BF16), st[i].astype(BF16))).astype(BF16))
    for i, (rev, h) in enumerate(probs):
        p = col(rev, h)
        q = dirs[rev][0][:, hsl(h)].astype(F32)
        qs = (q * e_all[rev][:, p:p + 1]).astype(BF16)
        dirs[rev][4][:, hsl(h)] = _dot(qs, st[i].astype(BF16)) + _dot(attn[i], v_new[i])
    new_st = []
    for i, (rev, h) in enumerate(probs):
        p = col(rev, h)
        kd = (dirs[rev][1][:, hsl(h)].astype(F32) * kdec_all[rev][:, p:p + 1]).astype(BF16)
        new_st.append(st[i] * elast_all[rev][:, p:p + 1] + _dot_tn(kd, v_new[i]))
    for i, (rev, h) in enumerate(probs):
        st_ref[rev, h] = new_st[i]


def _gdn_kernel(fb_ref, bb_ref, sq_ref, fi_ref, la_ref,
                qf_ref, kf_ref, vf_ref, gf_ref, qb_ref, kb_ref, vb_ref, gbk_ref, s0_ref,
                of_ref, ob_ref, sout_ref,
                st_ref, msk_ref, tri_ref):
    g = pl.program_id(0)

    @pl.when(g == 0)
    def _():
        _gdn_masks(msk_ref, tri_ref)

    @pl.when(fi_ref[g] == 1)
    def _():
        st_ref[...] = s0_ref[0]

    _gdn_chunk_step(((qf_ref, kf_ref, vf_ref, gf_ref, of_ref), (qb_ref, kb_ref, vb_ref, gbk_ref, ob_ref)),
                    st_ref, msk_ref, tri_ref)

    @pl.when(la_ref[g] == 1)
    def _():
        sout_ref[0] = st_ref[...]


def _gdn(lay, qkv, gb, s0):
    t = qkv.shape[0]
    c = GDN_CHUNK
    sched = lay.scan_schedule(c)[:5]

    def tok(which, col):
        return pl.BlockSpec((c, GDN_KW), lambda g, *m: (m[which][g], col))

    def gate(which):
        return pl.BlockSpec((c, LANES), lambda g, *m: (m[which][g], 0))

    state = pl.BlockSpec((1, 2, GDN_HEADS, GDN_DK, GDN_DK), lambda g, *m: (m[2][g], 0, 0, 0, 0))
    in_specs = [tok(0, 0), tok(0, 1), tok(0, 2), gate(0), tok(1, 0), tok(1, 1), tok(1, 2), gate(1), state]
    out_specs = [pl.BlockSpec((c, GDN_VW), lambda g, *m: (m[0][g], 0)),
                 pl.BlockSpec((c, GDN_VW), lambda g, *m: (m[1][g], 0)), state]
    out_shape = [jax.ShapeDtypeStruct((t, GDN_VW), F32), jax.ShapeDtypeStruct((t, GDN_VW), F32),
                 jax.ShapeDtypeStruct(s0.shape, F32)]
    scratch = [pltpu.VMEM((2, GDN_HEADS, GDN_DK, GDN_DK), F32),
               pltpu.VMEM((18, c, c), F32), pltpu.VMEM((2, c, c), BF16)]
    return pl.pallas_call(
        _gdn_kernel,
        grid_spec=pltpu.PrefetchScalarGridSpec(num_scalar_prefetch=5, grid=(len(sched[0]),),
                                               in_specs=in_specs, out_specs=out_specs,
                                               scratch_shapes=scratch),
        out_shape=out_shape,
        compiler_params=_cparams(1),
        name="gdn_scan",
    )(*sched, qkv, qkv, qkv, gb, qkv, qkv, qkv, gb, s0)


def _odd_post_kernel(cr_ref, of_ref, ob_ref, z_ref, x_ref, nw_ref, wout_ref, npost_ref, g1_ref, out_ref):
    o = of_ref[...] + ob_ref[...]
    z = z_ref[...].astype(F32)
    nw = nw_ref[...]
    parts = []
    for h in range(GDN_HEADS):
        sl = slice(h * GDN_DK, (h + 1) * GDN_DK)
        parts.append(_rms(o[:, sl], nw) * _silu(z[:, sl]))
    y = jnp.concatenate(parts, axis=1)
    out = _dot(y.astype(BF16), wout_ref[...])
    out_ref[...] = x_ref[...] + g1_ref[0] * _rms(out, npost_ref[...])


def _odd_post(lay, o_f, o_b, proj, x, norm_w, w_out, n_post, g1):
    t, d = x.shape
    cr = lay.cond_rows(TOK)
    row = lambda i, *m: (i, 0)
    const = lambda i, *m: (0, 0)
    in_specs = [pl.BlockSpec((TOK, GDN_VW), row), pl.BlockSpec((TOK, GDN_VW), row),
                pl.BlockSpec((TOK, GDN_VW), lambda i, *m: (i, 3)),
                pl.BlockSpec((TOK, d), row),
                pl.BlockSpec((1, GDN_DK), const), pl.BlockSpec(w_out.shape, const), pl.BlockSpec((1, d), const),
                pl.BlockSpec((1, 1, d), lambda i, *m: (m[0][i], 0, 0))]
    return pl.pallas_call(
        _odd_post_kernel,
        grid_spec=pltpu.PrefetchScalarGridSpec(num_scalar_prefetch=1, grid=(t // TOK,),
                                               in_specs=in_specs, out_specs=pl.BlockSpec((TOK, d), row)),
        out_shape=jax.ShapeDtypeStruct((t, d), F32),
        compiler_params=_cparams(1),
        name="odd_post",
    )(cr, o_f, o_b, proj, x, norm_w.reshape(1, -1), w_out, n_post.reshape(1, d), g1)


SUB = D_MODEL // LANES
N_PAIRS = EXP_PER_GROUP * (EXP_PER_GROUP - 1) // 2
N_CLASSES = N_GROUPS * N_PAIRS
CLS_PAD = 32
PAIR_LO = (0, 0, 0, 1, 1, 2)
PAIR_HI = (1, 2, 3, 2, 3, 3)


def _store_token_major(ref, x):
    n = x.shape[0]
    for s in range(SUB):
        ref[pl.ds(s, n, stride=SUB), :] = x[:, s * LANES:(s + 1) * LANES]


def _load_token_major(ref, n):
    return jnp.concatenate([ref[pl.ds(s, n, stride=SUB), :] for s in range(SUB)], axis=1)


def _router_kernel(cr_ref, x_ref, g_ref, sc_ref, sh_ref, rwt_ref, rb_ref,
                   h_ref, route_ref, wts_ref, cnt_ref,
                   base_ref, su_ref):
    i = pl.program_id(0)
    tm = TM_ROUTE

    @pl.when(i == 0)
    def _():
        base_ref[...] = jnp.zeros_like(base_ref)
        ii = lax.broadcasted_iota(I32, (tm, tm), 0)
        jj = lax.broadcasted_iota(I32, (tm, tm), 1)
        su_ref[...] = jnp.where(ii < jj, 1.0, 0.0).astype(BF16)

    h = _rms(x_ref[...], g_ref[...]) * (1.0 + sc_ref[0]) + sh_ref[0]
    hb = h.astype(BF16)
    _store_token_major(h_ref, hb.astype(F32))
    h1 = hb
    w1, w2, w3 = _split3(rwt_ref[...])
    logits = _dot_nt(w1, h1) + _dot_nt(w2, h1) + _dot_nt(w3, h1)
    score = jax.nn.sigmoid(logits)
    sel = score + rb_ref[...]

    def row(a, e):
        return a[e:e + 1, :]

    gsum = []
    for gi in range(N_GROUPS):
        a, b, c, d = (row(sel, gi * EXP_PER_GROUP + j) for j in range(EXP_PER_GROUP))
        hi1, lo1 = jnp.maximum(a, b), jnp.minimum(a, b)
        hi2, lo2 = jnp.maximum(c, d), jnp.minimum(c, d)
        gsum.append(jnp.maximum(hi1, hi2) + jnp.maximum(jnp.minimum(hi1, hi2), jnp.maximum(lo1, lo2)))
    best = jnp.zeros_like(gsum[0]).astype(I32)
    cur = gsum[0]
    for gi in range(1, N_GROUPS):
        upd = gsum[gi] > cur
        best = jnp.where(upd, gi, best)
        cur = jnp.where(upd, gsum[gi], cur)

    def pick(arr, j):
        out = row(arr, j)
        for gi in range(1, N_GROUPS):
            out = jnp.where(best == gi, row(arr, gi * EXP_PER_GROUP + j), out)
        return out

    vals = [pick(sel, j) for j in range(EXP_PER_GROUP)]
    scs = [pick(score, j) for j in range(EXP_PER_GROUP)]

    def argmax_first(vs):
        idx = jnp.zeros_like(best)
        m = vs[0]
        for j in range(1, EXP_PER_GROUP):
            upd = vs[j] > m
            idx = jnp.where(upd, j, idx)
            m = jnp.where(upd, vs[j], m)
        return idx

    i1 = argmax_first(vals)
    vals2 = [jnp.where(i1 == j, -jnp.inf, vals[j]) for j in range(EXP_PER_GROUP)]
    i2 = argmax_first(vals2)

    def take(vs, idx):
        out = vs[0]
        for j in range(1, EXP_PER_GROUP):
            out = jnp.where(idx == j, vs[j], out)
        return out

    s1, s2 = take(scs, i1), take(scs, i2)
    tot = s1 + s2
    first_low = i1 < i2
    lo = jnp.minimum(i1, i2)
    hi = jnp.maximum(i1, i2)
    pair = jnp.where(lo == 0, hi - 1, jnp.where(lo == 1, hi + 1, N_PAIRS - 1))
    cls = best * N_PAIRS + pair
    wts_ref[0:1, :] = jnp.where(first_low, s1, s2) / tot
    wts_ref[1:2, :] = jnp.where(first_low, s2, s1) / tot

    crow = lax.broadcasted_iota(I32, (CLS_PAD, tm), 0)
    onehot = jnp.where(crow == cls, 1.0, 0.0)
    before = _dot(onehot.astype(BF16), su_ref[...]) + base_ref[...]
    route_ref[0:1, :] = cls
    route_ref[1:2, :] = jnp.sum(onehot * before, axis=0, keepdims=True).astype(I32)
    base_ref[...] = base_ref[...] + jnp.sum(onehot, axis=1, keepdims=True)
    cnt_ref[...] = jnp.broadcast_to(base_ref[...], cnt_ref.shape).astype(I32)


def _router(lay, x, g, sc, sh, rwt, rbias):
    t, d = x.shape
    cr = lay.cond_rows(TM_ROUTE)
    tm = TM_ROUTE
    row = lambda i, cr: (i, 0)
    col = lambda i, cr: (0, i)
    const = lambda i, cr: (0, 0)
    cond = lambda i, cr: (cr[i], 0, 0)
    in_specs = [pl.BlockSpec((tm, d), row), pl.BlockSpec((1, d), const),
                pl.BlockSpec((1, 1, d), cond), pl.BlockSpec((1, 1, d), cond),
                pl.BlockSpec((N_EXPERTS, d), const), pl.BlockSpec((N_EXPERTS, 1), const)]
    out_specs = [pl.BlockSpec((tm * SUB, LANES), row), pl.BlockSpec((2, tm), col), pl.BlockSpec((2, tm), col),
                 pl.BlockSpec((CLS_PAD, LANES), const)]
    out_shape = [jax.ShapeDtypeStruct((t * SUB, LANES), F32), jax.ShapeDtypeStruct((2, t), I32),
                 jax.ShapeDtypeStruct((2, t), F32), jax.ShapeDtypeStruct((CLS_PAD, LANES), I32)]
    return pl.pallas_call(
        _router_kernel,
        grid_spec=pltpu.PrefetchScalarGridSpec(num_scalar_prefetch=1, grid=(t // tm,),
                                               in_specs=in_specs, out_specs=out_specs,
                                               scratch_shapes=[pltpu.VMEM((CLS_PAD, 1), F32),
                                                               pltpu.VMEM((tm, tm), BF16)]),
        out_shape=out_shape,
        compiler_params=_cparams(1),
        name="router",
    )(cr, x, g.reshape(1, d), sc, sh, rwt, rbias.reshape(N_EXPERTS, 1))


def _row_copy(src, dst, sem):
    return pltpu.make_async_copy(src, dst, sem)


def _dispatch_kernel(ends_ref, padded_ref, pos_ref, h_ref, sorted_ref, zero_ref, sem, zsem, *, n_tiles):
    n = TOK

    @pl.when(pl.program_id(0) == 0)
    def _():
        zero_ref[...] = jnp.zeros_like(zero_ref)

        def fill_tile(start_row):
            start = pl.multiple_of(start_row * SUB, SUB)
            fill = _row_copy(zero_ref, sorted_ref.at[pl.ds(start, TM_EXP * SUB)], zsem)
            fill.start()
            fill.wait()

        for c in range(N_CLASSES):
            @pl.when(padded_ref[c] > 0)
            def _():
                fill_tile(ends_ref[c] - TM_EXP)
        for j in range(n_tiles - N_CLASSES, n_tiles):
            @pl.when(j * TM_EXP >= ends_ref[N_CLASSES - 1])
            def _():
                fill_tile(j * TM_EXP)

    def issue(r, carry):
        for k in range(2):
            tok = 2 * r + k
            src = h_ref.at[pl.ds(pl.multiple_of(tok * SUB, SUB), SUB)]
            dst = pl.multiple_of(pos_ref[0, 0, tok] * SUB, SUB)
            _row_copy(src, sorted_ref.at[pl.ds(dst, SUB)], sem).start(priority=k)
        return carry

    lax.fori_loop(0, n // 2, issue, 0, unroll=8)
    _row_copy(h_ref, sorted_ref.at[pl.ds(0, n * SUB)], sem).wait()


def _dispatch(h, pos3, ends, padded, n_rows):
    t = h.shape[0] // SUB
    return pl.pallas_call(
        functools.partial(_dispatch_kernel, n_tiles=n_rows // TM_EXP),
        grid_spec=pltpu.PrefetchScalarGridSpec(
            num_scalar_prefetch=2, grid=(t // TOK,),
            in_specs=[pl.BlockSpec((1, 1, TOK), lambda i, *m: (i, 0, 0), memory_space=pltpu.SMEM),
                      pl.BlockSpec((TOK * SUB, LANES), lambda i, *m: (i, 0))],
            out_specs=pl.BlockSpec(memory_space=pl.ANY),
            scratch_shapes=[pltpu.VMEM((TM_EXP * SUB, LANES), h.dtype),
                            pltpu.SemaphoreType.DMA(()), pltpu.SemaphoreType.DMA(())]),
        out_shape=jax.ShapeDtypeStruct((n_rows * SUB, LANES), h.dtype),
        compiler_params=_cparams(1),
        name="dispatch",
    )(ends, padded, pos3, h)


def _expert_kernel(ea_ref, eb_ref, tv_ref, h_ref, *refs):
    i = pl.program_id(0)
    w_refs, y_ref, wb_refs = refs[:6], refs[6], refs[7:]
    prev = jnp.maximum(i - 1, 0)

    for slot, e_ref in enumerate((ea_ref, eb_ref)):
        @pl.when((i == 0) | (e_ref[i] != e_ref[prev]))
        def _():
            for j in range(3):
                wb_refs[3 * slot + j][...] = w_refs[3 * slot + j][0, 0].astype(BF16)

    @pl.when(tv_ref[i] == 1)
    def _():
        hb = _load_token_major(h_ref, TM_EXP).astype(BF16)
        for slot in range(2):
            wg, wu, wd = (wb_refs[3 * slot + j][...] for j in range(3))
            he = (_silu(_dot(hb, wg)) * _dot(hb, wu)).astype(BF16)
            y = _dot(he, wd)
            for s in range(SUB):
                y_ref[pl.ds(slot * SUB + s, TM_EXP, stride=2 * SUB), :] = y[:, s * LANES:(s + 1) * LANES]

    @pl.when(tv_ref[i] == 0)
    def _():
        y_ref[...] = jnp.zeros_like(y_ref)


def _experts(hs, tile_ea, tile_eb, tile_valid, wg, wu, wd, layer):
    d = wg.shape[2]
    r = hs.shape[0] // SUB
    sel_a = lambda i, ea, eb, tv: (layer, ea[i], 0, 0)
    sel_b = lambda i, ea, eb, tv: (layer, eb[i], 0, 0)
    w_specs = []
    for sel in (sel_a, sel_b):
        w_specs += [pl.BlockSpec((1, 1, d, D_EXPERT), sel), pl.BlockSpec((1, 1, d, D_EXPERT), sel),
                    pl.BlockSpec((1, 1, D_EXPERT, d), sel)]
    wb = [pltpu.VMEM((d, D_EXPERT), BF16), pltpu.VMEM((d, D_EXPERT), BF16), pltpu.VMEM((D_EXPERT, d), BF16)]
    return pl.pallas_call(
        _expert_kernel,
        grid_spec=pltpu.PrefetchScalarGridSpec(
            num_scalar_prefetch=3, grid=(r // TM_EXP,),
            in_specs=[pl.BlockSpec((TM_EXP * SUB, LANES), lambda i, ea, eb, tv: (i * tv[i], 0))] + w_specs,
            out_specs=pl.BlockSpec((TM_EXP * 2 * SUB, LANES), lambda i, ea, eb, tv: (i, 0)),
            scratch_shapes=wb + wb),
        out_shape=jax.ShapeDtypeStruct((r * 2 * SUB, LANES), F32),
        compiler_params=_cparams(1),
        name="experts",
    )(tile_ea, tile_eb, tile_valid, hs, wg, wu, wd, wg, wu, wd)


def _combine_kernel(cr_ref, pos_ref, ys_ref, wts_ref, x_ref, npost_ref, g2_ref, *rest, n_first):
    n = TOK
    buf_ref, sem = rest[-2:]
    outs = rest[:-2]

    wide = 2 * SUB

    def issue(r, carry):
        for k in range(2):
            tok = 2 * r + k
            src = pl.multiple_of(pos_ref[0, 0, tok] * wide, wide)
            dst = pl.multiple_of(tok * wide, wide)
            _row_copy(ys_ref.at[pl.ds(src, wide)], buf_ref.at[pl.ds(dst, wide)], sem).start(priority=k)
        return carry

    lax.fori_loop(0, n // 2, issue, 0, unroll=8)
    _row_copy(ys_ref.at[pl.ds(0, n * wide)], buf_ref, sem).wait()

    w = wts_ref[...]
    y_lo = jnp.concatenate([buf_ref[pl.ds(s, n, stride=wide), :] for s in range(SUB)], axis=1)
    y_hi = jnp.concatenate([buf_ref[pl.ds(SUB + s, n, stride=wide), :] for s in range(SUB)], axis=1)
    y = y_lo * w[:, 0:1] + y_hi * w[:, 1:2]
    res = x_ref[...] + g2_ref[0] * _rms(y, npost_ref[...])
    if n_first is None:
        outs[0][...] = res
    else:
        i = pl.program_id(0)

        @pl.when(i < n_first)
        def _():
            outs[0][...] = res

        @pl.when(i >= n_first)
        def _():
            outs[1][...] = res


def _combine(lay, ys3, pos3, wts, x, n_post, g2, split):
    t, d = x.shape
    cr = lay.cond_rows(TOK)
    row = lambda i, cr: (i, 0)
    const = lambda i, cr: (0, 0)
    in_specs = [pl.BlockSpec((1, 1, TOK), lambda i, cr: (i, 0, 0), memory_space=pltpu.SMEM),
                pl.BlockSpec(memory_space=pl.ANY),
                pl.BlockSpec((TOK, 2), row),
                pl.BlockSpec((TOK, d), row),
                pl.BlockSpec((1, d), const),
                pl.BlockSpec((1, 1, d), lambda i, cr: (cr[i], 0, 0))]
    if split:
        n_first = lay.t_ctx // TOK
        out_specs = [pl.BlockSpec((TOK, d), lambda i, cr: (jnp.minimum(i, n_first - 1), 0)),
                     pl.BlockSpec((TOK, d), lambda i, cr: (jnp.maximum(i - n_first, 0), 0))]
        out_shape = [jax.ShapeDtypeStruct((lay.t_ctx, d), F32), jax.ShapeDtypeStruct((t - lay.t_ctx, d), F32)]
    else:
        n_first = None
        out_specs = [pl.BlockSpec((TOK, d), row)]
        out_shape = [jax.ShapeDtypeStruct((t, d), F32)]
    return pl.pallas_call(
        functools.partial(_combine_kernel, n_first=n_first),
        grid_spec=pltpu.PrefetchScalarGridSpec(
            num_scalar_prefetch=1, grid=(t // TOK,), in_specs=in_specs, out_specs=out_specs,
            scratch_shapes=[pltpu.VMEM((TOK * 2 * SUB, LANES), F32), pltpu.SemaphoreType.DMA(())]),
        out_shape=out_shape,
        compiler_params=_cparams(1),
        name="combine",
    )(cr, pos3, ys3, wts, x, n_post.reshape(1, d), g2)


def _moe(lay, x, g_pre, sc, sh, g2, n_post, rwt, rbias, wg, wu, wd, layer, split):
    t, d = x.shape
    h, route, wts, cnt = _router(lay, x, g_pre, sc, sh, rwt, rbias)
    cls, rank = route[0], route[1]
    counts = cnt[:N_CLASSES, 0]
    padded = ((counts + TM_EXP - 1) // TM_EXP) * TM_EXP
    ends = jnp.cumsum(padded).astype(I32)
    offs = ends - padded
    cids = jnp.arange(N_CLASSES, dtype=I32)[:, None]
    pos = jnp.sum(jnp.where(cls[None] == cids, offs[:, None], 0), axis=0) + rank
    pos3 = pos.reshape(t // TOK, 1, TOK).astype(I32)
    n_rows = t + N_CLASSES * TM_EXP
    n_tiles = n_rows // TM_EXP
    starts = jnp.arange(n_tiles, dtype=I32) * TM_EXP
    tile_cls = jnp.minimum(jnp.sum((starts[:, None] >= ends[None, :]).astype(I32), axis=1), N_CLASSES - 1)
    tile_valid = (starts < ends[-1]).astype(I32)
    grp, pair = tile_cls // N_PAIRS, tile_cls % N_PAIRS
    pids = jnp.arange(N_PAIRS, dtype=I32)[None, :]
    lo = jnp.sum(jnp.where(pair[:, None] == pids, jnp.asarray(PAIR_LO, I32)[None, :], 0), axis=1)
    hi = jnp.sum(jnp.where(pair[:, None] == pids, jnp.asarray(PAIR_HI, I32)[None, :], 0), axis=1)
    tile_ea = (grp * EXP_PER_GROUP + lo).astype(I32)
    tile_eb = (grp * EXP_PER_GROUP + hi).astype(I32)
    hs = _dispatch(h, pos3, ends, padded.astype(I32), n_rows)
    ys = _experts(hs, tile_ea, tile_eb, tile_valid, wg, wu, wd, layer)
    return _combine(lay, ys, pos3, wts.T, x, n_post, g2, split)


def _rope_tables(ld):
    rows = ld // GRID_W
    r = jnp.repeat(jnp.arange(rows, dtype=F32), GRID_W)
    col = jnp.tile(jnp.arange(GRID_W, dtype=F32), rows)
    quarter = RET_DK // 4
    inv = ROPE_BASE ** (-jnp.arange(quarter, dtype=F32) / quarter)
    ang = jnp.concatenate([r[:, None] * inv, col[:, None] * inv], axis=-1)
    cos, sin = jnp.cos(ang), jnp.sin(ang)
    c = jnp.concatenate([cos, cos], axis=-1)
    s = jnp.concatenate([-sin, sin], axis=-1)
    ident_c = jnp.ones((TOK, RET_DK), F32)
    ident_s = jnp.zeros((TOK, RET_DK), F32)
    return jnp.concatenate([c, ident_c], axis=0), jnp.concatenate([s, ident_s], axis=0)


def kernel(x_prompt, x_sample, state_ret, state_gdn, c, c_ctx, w_mod, b_mod, norm_mix_pre, norm_mix_post,
           norm_ffn_pre, norm_ffn_post, ev_w_in, ev_conv_w, ev_conv_ln_g, ev_conv_ln_b, ev_ret_decay, ev_w_out,
           od_w_in, od_conv_w, od_a_log, od_dt_bias, od_norm_w, od_w_out, router_w, router_bias,
           moe_w_gate, moe_w_up, moe_w_down):
    bc, lc, d = x_prompt.shape
    bd, ld, _ = x_sample.shape
    depth = w_mod.shape[0]
    lay = _Layout(bc, lc, bd, ld)
    t = lay.t

    x = (x_prompt.reshape(bc * lc, d), x_sample.reshape(bd * ld, d))
    cond = jnp.zeros((COND_PAD, d), F32).at[0].set(c_ctx).at[1:1 + bd].set(c)
    mod = _modulation(cond, w_mod, b_mod)
    mod = mod.reshape(depth, COND_PAD, N_MOD, 1, d).transpose(0, 2, 1, 3, 4)

    rope_c, rope_s = _rope_tables(ld)
    rwt = router_w.T
    ret_states, gdn_states = [], []
    for layer in range(depth):
        sh1, sc1, g1, sh2, sc2, g2 = (mod[layer, j] for j in range(N_MOD))
        i = layer // 2
        if layer % 2 == 0:
            (proj,) = _inproj(lay, x, norm_mix_pre[layer], sc1, sh1, [ev_w_in[i].astype(BF16)], [BF16])
            s0 = jnp.concatenate([jnp.zeros((bc,) + state_ret.shape[2:], F32), state_ret[:, i]], axis=0)
            log_gamma = -jnp.exp(ev_ret_decay[i].astype(F32))
            o_f, o_b, s_out = _retention(lay, proj, log_gamma, rope_c, rope_s, s0)
            ret_states.append(s_out[:bc])
            cw = jnp.zeros((32, CONV_CH), F32).at[:CONV_W].set(ev_conv_w[i])
            x = _even_post(lay, proj, o_f, o_b, x, cw, ev_conv_ln_g[i], ev_conv_ln_b[i],
                           ev_w_out[i].astype(BF16), norm_mix_post[layer], g1)
        else:
            w_in = od_w_in[i]
            n_main = 2 * GDN_KW + 2 * GDN_VW
            w_small = jnp.zeros((d, LANES), F32).at[:, :4 * GDN_HEADS].set(w_in[:, n_main:])
            proj, ab = _inproj(lay, x, norm_mix_pre[layer], sc1, sh1,
                               [w_in[:, :n_main].astype(BF16), w_small.astype(BF16)], [BF16, F32])
            cw = jnp.zeros((8, 2 * GDN_KW + GDN_VW), F32).at[:SHORT_W].set(od_conv_w[i])
            alog_row = jnp.zeros((1, LANES), F32).at[0, :2 * GDN_HEADS].set(od_a_log[i].reshape(-1))
            dtb_row = jnp.zeros((1, LANES), F32).at[0, :2 * GDN_HEADS].set(od_dt_bias[i].reshape(-1))
            qkv, gb = _odd_pre(lay, proj, ab, cw, alog_row, dtb_row)
            s0 = jnp.concatenate([jnp.zeros((bc,) + state_gdn.shape[2:], F32), state_gdn[:, i]], axis=0)
            o_f, o_b, s_out = _gdn(lay, qkv, gb, s0)
            gdn_states.append(s_out[:bc])
            x = _odd_post(lay, o_f, o_b, proj, x, od_norm_w[i], od_w_out[i].astype(BF16),
                          norm_mix_post[layer], g1)
        outs = _moe(lay, x, norm_ffn_pre[layer], sc2, sh2, g2, norm_ffn_post[layer], rwt, router_bias,
                    moe_w_gate, moe_w_up, moe_w_down, layer, split=(layer == depth - 1))
        x = outs[0]

    y_prompt = outs[0].reshape(bc, lc, d)
    y_sample = outs[1].reshape(bd, ld, d)
    new_ret = jnp.stack(ret_states, axis=1)
    new_gdn = jnp.stack(gdn_states, axis=1)
    return y_prompt, y_sample, new_ret.astype(x_prompt.dtype), new_gdn.astype(x_prompt.dtype)
```

```python
import functools
import math

import jax
import jax.numpy as jnp
import numpy as np
from jax import lax
from jax.experimental import pallas as pl
from jax.experimental.pallas import tpu as pltpu

F32 = jnp.float32
BF16 = jnp.bfloat16
I32 = jnp.int32

D_MODEL = 1024
N_MOD = 6
EPS = 1e-6
GRID_W = 64
CONV_CH = 512
CONV_W = 31
RET_HEADS = 4
RET_DK = 128
RET_W = 512
ROPE_BASE = 10000.0
EVEN_IN = 2 * CONV_CH + 4 * RET_W
GDN_HEADS = 8
GDN_DK = 128
GDN_KW = 1024
GDN_VW = 1024
SHORT_W = 5
N_EXPERTS = 16
N_GROUPS = 4
EXP_PER_GROUP = 4
D_EXPERT = 512

LANES = 128
TOK = 256
TM_PROJ = 512
TM_ROUTE = 512
TM_EXP = 512
GDN_CHUNK = 128
HALO = 16
COND_PAD = 16
VMEM_LIMIT = 56 * 1024 * 1024
NEG_BIG = -1e30


def _cparams(n_axes=1, vmem=VMEM_LIMIT):
    return pltpu.CompilerParams(dimension_semantics=("arbitrary",) * n_axes, vmem_limit_bytes=vmem)


def _silu(x):
    return x * jax.nn.sigmoid(x)


def _rms(x, g):
    return x * lax.rsqrt(jnp.mean(x * x, axis=-1, keepdims=True) + EPS) * g


def _dot(a, b):
    return jnp.dot(a, b, preferred_element_type=F32)


def _dot_nt(a, b):
    return lax.dot_general(a, b, (((1,), (1,)), ((), ())), preferred_element_type=F32)


def _dot_tn(a, b):
    return lax.dot_general(a, b, (((0,), (0,)), ((), ())), preferred_element_type=F32)


def _split3(x):
    x1 = x.astype(BF16)
    r = x - x1.astype(F32)
    x2 = r.astype(BF16)
    x3 = (r - x2.astype(F32)).astype(BF16)
    return x1, x2, x3


class _Layout:
    def __init__(self, bc, lc, bd, ld):
        self.bc, self.lc, self.bd, self.ld = bc, lc, bd, ld
        self.t_ctx = bc * lc
        self.t = bc * lc + bd * ld
        self.n_seq = bc + bd
        assert lc % TOK == 0 and ld % TOK == 0 and self.t_ctx % TM_PROJ == 0 and ld % TM_PROJ == 0
        assert self.t % TM_ROUTE == 0 and lc % GDN_CHUNK == 0 and ld % GDN_CHUNK == 0

    def seq_of_row(self, r):
        if r < self.t_ctx:
            return r // self.lc, r % self.lc, self.lc
        r2 = r - self.t_ctx
        return self.bc + r2 // self.ld, r2 % self.ld, self.ld

    def cond_rows(self, tile):
        out = []
        for i in range(self.t // tile):
            s, _, _ = self.seq_of_row(i * tile)
            out.append(0 if s < self.bc else 1 + s - self.bc)
        return np.asarray(out, np.int32)

    def edges(self, tile):
        left, right = [], []
        for i in range(self.t // tile):
            _, p, l = self.seq_of_row(i * tile)
            left.append(int(p == 0))
            right.append(int(p + tile == l))
        return np.asarray(left, np.int32), np.asarray(right, np.int32)

    def scan_schedule(self, chunk):
        fb, bb, sq, fi, la, rf, rb = [], [], [], [], [], [], []
        ident = self.ld // chunk
        for s in range(self.n_seq):
            if s < self.bc:
                base, n = s * self.lc // chunk, self.lc // chunk
            else:
                base, n = (self.t_ctx + (s - self.bc) * self.ld) // chunk, self.ld // chunk
            for c in range(n):
                fb.append(base + c)
                bb.append(base + n - 1 - c)
                sq.append(s)
                fi.append(int(c == 0))
                la.append(int(c == n - 1))
                rf.append(ident if s < self.bc else c)
                rb.append(ident if s < self.bc else n - 1 - c)
        return [np.asarray(a, np.int32) for a in (fb, bb, sq, fi, la, rf, rb)]


def _mod_kernel(c_ref, w_ref, b_ref, o_ref):
    s = _silu(c_ref[...])
    o_ref[0] = _dot(s.astype(BF16), w_ref[0].astype(BF16)) + b_ref[0]


def _modulation(cond, w_mod, b_mod):
    depth, d, n = w_mod.shape
    nt = n // d
    return pl.pallas_call(
        _mod_kernel,
        grid=(depth, nt),
        in_specs=[pl.BlockSpec((COND_PAD, d), lambda l, j: (0, 0)),
                  pl.BlockSpec((1, d, d), lambda l, j: (l, 0, j)),
                  pl.BlockSpec((1, 1, d), lambda l, j: (l, 0, j))],
        out_specs=pl.BlockSpec((1, COND_PAD, d), lambda l, j: (l, 0, j)),
        out_shape=jax.ShapeDtypeStruct((depth, COND_PAD, n), F32),
        compiler_params=_cparams(2),
        name="modulation",
    )(cond, w_mod, b_mod.reshape(depth, 1, n))


def _two_source(x, tile):
    if isinstance(x, (tuple, list)):
        xa, xb = x
        n_first = xa.shape[0] // tile
    else:
        xa = xb = x
        n_first = x.shape[0] // tile
    d = xa.shape[1]
    specs = [pl.BlockSpec((tile, d), lambda i, *m: (jnp.minimum(i, n_first - 1), 0)),
             pl.BlockSpec((tile, d), lambda i, *m: (jnp.maximum(i - n_first, 0), 0))]
    return xa, xb, n_first, specs


def _read_two_source(xa_ref, xb_ref, n_first):
    return jnp.where(pl.program_id(0) < n_first, xa_ref[...], xb_ref[...])


def _inproj_kernel(cr_ref, xa_ref, xb_ref, g_ref, sc_ref, sh_ref, *refs, n_first):
    n = len(refs) // 2
    x = _read_two_source(xa_ref, xb_ref, n_first)
    h = _rms(x, g_ref[...]) * (1.0 + sc_ref[0]) + sh_ref[0]
    hb = h.astype(BF16)
    for w_ref, o_ref in zip(refs[:n], refs[n:]):
        o_ref[...] = _dot(hb, w_ref[...]).astype(o_ref.dtype)


def _inproj(lay, x, g, sc, sh, weights, out_dtypes):
    t, d = lay.t, g.shape[0]
    cr = lay.cond_rows(TM_PROJ)
    xa, xb, n_first, x_specs = _two_source(x, TM_PROJ)
    row = lambda i, cr: (i, 0)
    const = lambda i, cr: (0, 0)
    cond = lambda i, cr: (cr[i], 0, 0)
    in_specs = x_specs + [pl.BlockSpec((1, d), const), pl.BlockSpec((1, 1, d), cond), pl.BlockSpec((1, 1, d), cond)]
    in_specs += [pl.BlockSpec(w.shape, const) for w in weights]
    out_specs = [pl.BlockSpec((TM_PROJ, w.shape[1]), row) for w in weights]
    out_shape = [jax.ShapeDtypeStruct((t, w.shape[1]), dt) for w, dt in zip(weights, out_dtypes)]
    return pl.pallas_call(
        functools.partial(_inproj_kernel, n_first=n_first),
        grid_spec=pltpu.PrefetchScalarGridSpec(num_scalar_prefetch=1, grid=(t // TM_PROJ,),
                                               in_specs=in_specs, out_specs=out_specs),
        out_shape=out_shape,
        compiler_params=_cparams(1),
        name="inproj",
    )(cr, xa, xb, g.reshape(1, d), sc, sh, *weights)


def _rope(x, c, s):
    return x * c + pltpu.roll(x, RET_DK // 2, 1) * s


def _ret_kernel(fb_ref, bb_ref, sq_ref, fi_ref, la_ref, rf_ref, rb_ref,
                lg_ref, qf_ref, kf_ref, vf_ref, qb_ref, kb_ref, vb_ref,
                cf_ref, sf_ref, cb_ref, sb_ref, s0_ref,
                of_ref, ob_ref, sout_ref,
                st_ref, dm_ref, dec_ref, *, n_ctx):
    g = pl.program_id(0)
    c = TOK

    @pl.when(g == 0)
    def _():
        ii = lax.broadcasted_iota(I32, (c, c), 0)
        jj = lax.broadcasted_iota(I32, (c, c), 1)
        diff = (ii - jj).astype(F32)
        ri = lax.broadcasted_iota(I32, (c, RET_DK), 0).astype(F32)
        for h in range(RET_HEADS):
            lf = lg_ref[0, h]
            lb = lg_ref[1, h]
            low = jnp.exp(lf * jnp.maximum(diff, 0.0))
            up = jnp.exp(lb * jnp.maximum(-diff, 0.0))
            dm_ref[h] = jnp.where(diff > 0, low, jnp.where(diff < 0, up, 2.0))
            dec_ref[0, h] = jnp.exp(lf * (ri + 1.0))
            dec_ref[1, h] = jnp.exp(lf * (c - 1.0 - ri))
            dec_ref[2, h] = jnp.exp(lb * (c - ri))
            dec_ref[3, h] = jnp.exp(lb * ri)

    @pl.when((fi_ref[g] == 1) & (sq_ref[g] < n_ctx))
    def _():
        st_ref[...] = jnp.zeros_like(st_ref)

    @pl.when((fi_ref[g] == 1) & (sq_ref[g] >= n_ctx))
    def _():
        st_ref[...] = s0_ref[0]

    scale = RET_DK ** -0.5
    cf, sf, cb, sb = cf_ref[...], sf_ref[...], cb_ref[...], sb_ref[...]
    zero_row = jnp.zeros((1, RET_DK), F32)
    for h in range(RET_HEADS):
        sl = slice(h * RET_DK, (h + 1) * RET_DK)
        lf = lg_ref[0, h]
        lb = lg_ref[1, h]
        q = _rope(qf_ref[:, sl].astype(F32), cf, sf)
        k = _rope(kf_ref[:, sl].astype(F32), cf, sf) * scale
        v = vf_ref[:, sl]
        s = _dot_nt(q.astype(BF16), k.astype(BF16)) * dm_ref[h]
        st = st_ref[0, h]
        o = _dot(s.astype(BF16), v) + _dot((q * dec_ref[0, h]).astype(BF16), st.astype(BF16))
        of_ref[:, sl] = o
        st_ref[0, h] = jnp.exp(zero_row + lf * c) * st + _dot_tn((k * dec_ref[1, h]).astype(BF16), v)
        q = _rope(qb_ref[:, sl].astype(F32), cb, sb)
        k = _rope(kb_ref[:, sl].astype(F32), cb, sb) * scale
        v = vb_ref[:, sl]
        st = st_ref[1, h]
        ob_ref[:, sl] = _dot((q * dec_ref[2, h]).astype(BF16), st.astype(BF16))
        st_ref[1, h] = jnp.exp(zero_row + lb * c) * st + _dot_tn((k * dec_ref[3, h]).astype(BF16), v)

    @pl.when((la_ref[g] == 1) & (sq_ref[g] < n_ctx))
    def _():
        sout_ref[0] = st_ref[...]


def _retention(lay, proj, log_gamma, rope_c, rope_s, s0):
    t = proj.shape[0]
    sched = lay.scan_schedule(TOK)
    qcol, kcol, vcol = (2 * CONV_CH) // RET_W, (2 * CONV_CH) // RET_W + 1, (2 * CONV_CH) // RET_W + 2

    def tok(which, col):
        return pl.BlockSpec((TOK, RET_W), lambda g, *m: (m[which][g], col))

    def rope(which):
        return pl.BlockSpec((TOK, RET_DK), lambda g, *m: (m[which][g], 0))

    n_ctx = lay.bc
    sblk = (1, 2, RET_HEADS, RET_DK, RET_DK)
    state_in = pl.BlockSpec(sblk, lambda g, *m: (jnp.maximum(m[2][g] - n_ctx, 0), 0, 0, 0, 0))
    state_out = pl.BlockSpec(sblk, lambda g, *m: (jnp.minimum(m[2][g], n_ctx - 1), 0, 0, 0, 0))
    in_specs = [pl.BlockSpec(memory_space=pltpu.SMEM),
                tok(0, qcol), tok(0, kcol), tok(0, vcol), tok(1, qcol), tok(1, kcol), tok(1, vcol),
                rope(5), rope(5), rope(6), rope(6), state_in]
    out_specs = [pl.BlockSpec((TOK, RET_W), lambda g, *m: (m[0][g], 0)),
                 pl.BlockSpec((TOK, RET_W), lambda g, *m: (m[1][g], 0)), state_out]
    out_shape = [jax.ShapeDtypeStruct((t, RET_W), F32), jax.ShapeDtypeStruct((t, RET_W), F32),
                 jax.ShapeDtypeStruct((n_ctx,) + s0.shape[1:], F32)]
    scratch = [pltpu.VMEM((2, RET_HEADS, RET_DK, RET_DK), F32),
               pltpu.VMEM((RET_HEADS, TOK, TOK), F32),
               pltpu.VMEM((4, RET_HEADS, TOK, RET_DK), F32)]
    return pl.pallas_call(
        functools.partial(_ret_kernel, n_ctx=n_ctx),
        grid_spec=pltpu.PrefetchScalarGridSpec(num_scalar_prefetch=7, grid=(len(sched[0]),),
                                               in_specs=in_specs, out_specs=out_specs,
                                               scratch_shapes=scratch),
        out_shape=out_shape,
        compiler_params=_cparams(1),
        name="retention",
    )(*sched, log_gamma, proj, proj, proj, proj, proj, proj, rope_c, rope_s, rope_c, rope_s, s0)


def _even_post_kernel(cr_ref, le_ref, re_ref,
                      glu_ref, prev_ref, next_ref, gt_ref, of_ref, ob_ref, xa_ref, xb_ref,
                      cw_ref, lng_ref, lnb_ref, wout_ref, npost_ref, g1_ref,
                      out_ref, buf_ref, *, n_first):
    i = pl.program_id(0)

    def glu(r):
        r = r.astype(F32)
        return r[:, :CONV_CH] * jax.nn.sigmoid(r[:, CONV_CH:])

    keep_l = jnp.where(le_ref[i] == 1, 0.0, 1.0)
    keep_r = jnp.where(re_ref[i] == 1, 0.0, 1.0)
    buf_ref[0:HALO, :] = glu(prev_ref[...]) * keep_l
    buf_ref[HALO:HALO + TOK, :] = glu(glu_ref[...])
    buf_ref[HALO + TOK:, :] = glu(next_ref[...]) * keep_r
    off = HALO - CONV_W // 2
    sub = 8
    cols = []
    for cb in range(CONV_CH // LANES):
        ls = slice(cb * LANES, (cb + 1) * LANES)
        acc = jnp.zeros((TOK, LANES), F32)
        for r in range(sub):
            part = None
            for m in range((off + CONV_W - 1) // sub + 1):
                j = sub * m + r - off
                if 0 <= j < CONV_W:
                    term = buf_ref[sub * m:sub * m + TOK + sub, ls] * cw_ref[j:j + 1, ls]
                    part = term if part is None else part + term
            acc = acc + part[r:r + TOK, :]
        cols.append(acc)
    acc = jnp.concatenate(cols, axis=1)
    mu = jnp.mean(acc, axis=-1, keepdims=True)
    ac = acc - mu
    y = ac * lax.rsqrt(jnp.mean(ac * ac, axis=-1, keepdims=True) + EPS) * lng_ref[...] + lnb_ref[...]
    conv_out = _silu(y)

    o = of_ref[...] + ob_ref[...]
    gt = gt_ref[...].astype(F32)
    parts = []
    for h in range(RET_HEADS):
        sl = slice(h * RET_DK, (h + 1) * RET_DK)
        oh = o[:, sl]
        oc = oh - jnp.mean(oh, axis=-1, keepdims=True)
        on = oc * lax.rsqrt(jnp.mean(oc * oc, axis=-1, keepdims=True) + EPS)
        parts.append(on * _silu(gt[:, sl]))
    ret_out = jnp.concatenate(parts, axis=1)

    out = _dot(conv_out.astype(BF16), wout_ref[0:CONV_CH, :]) + _dot(ret_out.astype(BF16), wout_ref[CONV_CH:, :])
    out_ref[...] = _read_two_source(xa_ref, xb_ref, n_first) + g1_ref[0] * _rms(out, npost_ref[...])


def _even_post(lay, proj, o_f, o_b, x, conv_w, ln_g, ln_b, w_out, n_post, g1):
    t, d = lay.t, n_post.shape[0]
    cr = lay.cond_rows(TOK)
    le, re = lay.edges(TOK)
    xa, xb, n_first, x_specs = _two_source(x, TOK)
    hb = TOK // HALO
    n_halo = t // HALO
    row = lambda i, *m: (i, 0)
    const = lambda i, *m: (0, 0)
    in_specs = [pl.BlockSpec((TOK, 2 * CONV_CH), row),
                pl.BlockSpec((HALO, 2 * CONV_CH), lambda i, *m: (jnp.maximum(i * hb - 1, 0), 0)),
                pl.BlockSpec((HALO, 2 * CONV_CH), lambda i, *m: (jnp.minimum((i + 1) * hb, n_halo - 1), 0)),
                pl.BlockSpec((TOK, RET_W), lambda i, *m: (i, EVEN_IN // RET_W - 1)),
                pl.BlockSpec((TOK, RET_W), row), pl.BlockSpec((TOK, RET_W), row)] + x_specs + [
                pl.BlockSpec(conv_w.shape, const), pl.BlockSpec((1, CONV_CH), const), pl.BlockSpec((1, CONV_CH), const),
                pl.BlockSpec(w_out.shape, const), pl.BlockSpec((1, d), const),
                pl.BlockSpec((1, 1, d), lambda i, *m: (m[0][i], 0, 0))]
    return pl.pallas_call(
        functools.partial(_even_post_kernel, n_first=n_first),
        grid_spec=pltpu.PrefetchScalarGridSpec(num_scalar_prefetch=3, grid=(t // TOK,),
                                               in_specs=in_specs, out_specs=pl.BlockSpec((TOK, d), row),
                                               scratch_shapes=[pltpu.VMEM((TOK + 2 * HALO, CONV_CH), F32)]),
        out_shape=jax.ShapeDtypeStruct((t, d), F32),
        compiler_params=_cparams(1),
        name="even_post",
    )(cr, le, re, proj, proj, proj, proj, o_f, o_b, xa, xb, conv_w, ln_g.reshape(1, -1), ln_b.reshape(1, -1),
      w_out, n_post.reshape(1, d), g1)


def _odd_pre_kernel(le_ref, re_ref, qkv_ref, prev_ref, next_ref, ab_ref, cw_ref, alog_ref, dtb_ref,
                    qkv_out_ref, gb_out_ref, buf_ref):
    i = pl.program_id(0)
    keep_l = jnp.where(le_ref[i] == 1, 0.0, 1.0)
    keep_r = jnp.where(re_ref[i] == 1, 0.0, 1.0)
    buf_ref[0:HALO, :] = prev_ref[...].astype(F32) * keep_l
    buf_ref[HALO:HALO + TOK, :] = qkv_ref[...].astype(F32)
    buf_ref[HALO + TOK:, :] = next_ref[...].astype(F32) * keep_r
    off = HALO - SHORT_W // 2
    for blk in range(3 * GDN_HEADS):
        sl = slice(blk * GDN_DK, (blk + 1) * GDN_DK)
        acc = jnp.zeros((TOK, GDN_DK), F32)
        for j in range(SHORT_W):
            acc = acc + buf_ref[off + j:off + j + TOK, sl] * cw_ref[j:j + 1, sl]
        y = _silu(acc)
        if blk < 2 * GDN_HEADS:
            y = y * lax.rsqrt(jnp.sum(y * y, axis=-1, keepdims=True) + EPS)
            if blk < GDN_HEADS:
                y = y * (GDN_DK ** -0.5)
        qkv_out_ref[:, sl] = y.astype(qkv_out_ref.dtype)
    ab = ab_ref[...]
    z = ab + dtb_ref[...]
    softplus = jnp.maximum(z, 0.0) + jnp.log(1.0 + jnp.exp(-jnp.abs(z)))
    gate = -jnp.exp(alog_ref[...]) * softplus
    beta = jax.nn.sigmoid(ab)
    lane = lax.broadcasted_iota(I32, ab.shape, 1)
    gb_out_ref[...] = jnp.where(lane < 2 * GDN_HEADS, gate, beta)


def _odd_pre(lay, proj, ab, conv_w, alog_row, dtb_row):
    t = proj.shape[0]
    w = 2 * GDN_KW + GDN_VW
    le, re = lay.edges(TOK)
    hb = TOK // HALO
    n_halo = t // HALO
    row = lambda i, *m: (i, 0)
    const = lambda i, *m: (0, 0)
    in_specs = [pl.BlockSpec((TOK, w), row),
                pl.BlockSpec((HALO, w), lambda i, *m: (jnp.maximum(i * hb - 1, 0), 0)),
                pl.BlockSpec((HALO, w), lambda i, *m: (jnp.minimum((i + 1) * hb, n_halo - 1), 0)),
                pl.BlockSpec((TOK, LANES), row),
                pl.BlockSpec(conv_w.shape, const), pl.BlockSpec((1, LANES), const), pl.BlockSpec((1, LANES), const)]
    out_specs = [pl.BlockSpec((TOK, w), row), pl.BlockSpec((TOK, LANES), row)]
    return pl.pallas_call(
        _odd_pre_kernel,
        grid_spec=pltpu.PrefetchScalarGridSpec(num_scalar_prefetch=2, grid=(t // TOK,),
                                               in_specs=in_specs, out_specs=out_specs,
                                               scratch_shapes=[pltpu.VMEM((TOK + 2 * HALO, w), F32)]),
        out_shape=[jax.ShapeDtypeStruct((t, w), BF16), jax.ShapeDtypeStruct((t, LANES), F32)],
        compiler_params=_cparams(1),
        name="odd_pre",
    )(le, re, proj, proj, proj, ab, conv_w, alog_row, dtb_row)


_GDN_LEVELS = tuple(2 ** p for p in range(int(math.log2(GDN_CHUNK))))


def _gdn_masks(msk_ref, tri_ref):
    c = GDN_CHUNK
    ii = lax.broadcasted_iota(I32, (c, c), 0)
    jj = lax.broadcasted_iota(I32, (c, c), 1)
    one = jnp.ones((c, c), F32)
    zero = jnp.zeros((c, c), F32)
    for rev in (0, 1):
        a, b = (ii, jj) if rev == 0 else (jj, ii)
        base = rev * 9
        msk_ref[base + 0] = jnp.where(a >= b, one, zero)
        msk_ref[base + 1] = jnp.where(a > b, one, zero)
        for l, m in enumerate(_GDN_LEVELS):
            sh = int(math.log2(m))
            ab_, bb_ = a >> sh, b >> sh
            hit = ((ab_ & 1) == 1) & (bb_ == ab_ - 1)
            msk_ref[base + 2 + l] = jnp.where(hit, one, zero)
        tri_ref[rev] = jnp.where(a >= b, one, zero).astype(BF16)


def _gdn_chunk_step(dirs, st_ref, msk_ref, tri_ref):
    c = GDN_CHUNK
    nh = GDN_HEADS
    probs = [(rev, h) for rev in range(2) for h in range(nh)]
    ones = jnp.ones((c, c), BF16)
    gcum, gcum_t, e_all, kdec_all, elast_all, gbs = [], [], [], [], [], []
    for rev in range(2):
        gb = dirs[rev][3][...]
        g1, g2, g3 = _split3(gb)
        tri = tri_ref[rev]
        gc = _dot(tri, g1) + _dot(tri, g2) + _dot(tri, g3)
        gl = _dot(ones, g1) + _dot(ones, g2) + _dot(ones, g3)
        gbs.append(gb)
        gcum.append(gc)
        gcum_t.append(gc.T)
        e_all.append(jnp.exp(gc))
        kdec_all.append(jnp.exp(gl - gc))
        elast_all.append(jnp.exp(gl))

    def col(rev, h):
        return rev * nh + h

    def hsl(h):
        return slice(h * GDN_DK, (h + 1) * GDN_DK)

    st = [st_ref[rev, h] for rev, h in probs]

    a, attn, kbs = [], [], []
    for rev, h in probs:
        base = rev * 9
        p = col(rev, h)
        q_ref, k_ref = dirs[rev][0], dirs[rev][1]
        k = k_ref[:, hsl(h)]
        gcb = jnp.broadcast_to(gcum[rev][:, p:p + 1], (c, c))
        grb = jnp.broadcast_to(gcum_t[rev][p:p + 1, :], (c, c))
        decay = jnp.exp(jnp.where(msk_ref[base] > 0, gcb - grb, NEG_BIG))
        kb = k.astype(F32) * gbs[rev][:, 2 * nh + p:2 * nh + p + 1]
        kbs.append(kb)
        a.append(_dot_nt(kb.astype(BF16), k) * decay * msk_ref[base + 1])
        attn.append((_dot_nt(q_ref[:, hsl(h)], k) * decay).astype(BF16))

    tinv = []
    for i, (rev, h) in enumerate(probs):
        base = rev * 9
        tinv.append(msk_ref[base] - msk_ref[base + 1] - a[i] * msk_ref[base + 2])
    for l in range(1, len(_GDN_LEVELS)):
        bt = []
        for i, (rev, h) in enumerate(probs):
            bt.append(_dot((a[i] * msk_ref[rev * 9 + 2 + l]).astype(BF16), tinv[i].astype(BF16)))
        for i in range(len(probs)):
            tinv[i] = tinv[i] - _dot(tinv[i].astype(BF16), bt[i].astype(BF16))

    sol = []
    for i, (rev, h) in enumerate(probs):
        p = col(rev, h)
        v = dirs[rev][2][:, hsl(h)].astype(F32)
        beta = gbs[rev][:, 2 * nh + p:2 * nh + p + 1]
        rhs = jnp.concatenate([v * beta, kbs[i] * e_all[rev][:, p:p + 1]], axis=1)
        sol.append(_dot(tinv[i].astype(BF16), rhs.astype(BF16)))
    v_new = []
    for i in range(len(probs)):
        u, w = sol[i][:, :GDN_DK], sol[i][:, GDN_DK:]
        v_new.append((u - _dot(w.astype(BF16), st[i].astype(BF16))).astype(BF16))
    for i, (rev, h) in enumerate(probs):
        p = col(rev, h)
        q = dirs[rev][0][:, hsl(h)].astype(F32)
        qs = (q * e_all[rev][:, p:p + 1]).astype(BF16)
        dirs[rev][4][:, hsl(h)] = (_dot(qs, st[i].astype(BF16)) + _dot(attn[i], v_new[i])).astype(BF16)
    new_st = []
    for i, (rev, h) in enumerate(probs):
        p = col(rev, h)
        kd = (dirs[rev][1][:, hsl(h)].astype(F32) * kdec_all[rev][:, p:p + 1]).astype(BF16)
        new_st.append(st[i] * elast_all[rev][:, p:p + 1] + _dot_tn(kd, v_new[i]))
    for i, (rev, h) in enumerate(probs):
        st_ref[rev, h] = new_st[i]


def _gdn_kernel(fb_ref, bb_ref, sq_ref, fi_ref, la_ref,
                qf_ref, kf_ref, vf_ref, gf_ref, qb_ref, kb_ref, vb_ref, gbk_ref, s0_ref,
                of_ref, ob_ref, sout_ref,
                st_ref, msk_ref, tri_ref, *, n_ctx):
    g = pl.program_id(0)

    @pl.when(g == 0)
    def _():
        _gdn_masks(msk_ref, tri_ref)

    @pl.when((fi_ref[g] == 1) & (sq_ref[g] < n_ctx))
    def _():
        st_ref[...] = jnp.zeros_like(st_ref)

    @pl.when((fi_ref[g] == 1) & (sq_ref[g] >= n_ctx))
    def _():
        st_ref[...] = s0_ref[0]

    _gdn_chunk_step(((qf_ref, kf_ref, vf_ref, gf_ref, of_ref), (qb_ref, kb_ref, vb_ref, gbk_ref, ob_ref)),
                    st_ref, msk_ref, tri_ref)

    @pl.when((la_ref[g] == 1) & (sq_ref[g] < n_ctx))
    def _():
        sout_ref[0] = st_ref[...]


def _gdn(lay, qkv, gb, s0):
    t = qkv.shape[0]
    c = GDN_CHUNK
    sched = lay.scan_schedule(c)[:5]

    def tok(which, col):
        return pl.BlockSpec((c, GDN_KW), lambda g, *m: (m[which][g], col))

    def gate(which):
        return pl.BlockSpec((c, LANES), lambda g, *m: (m[which][g], 0))

    n_ctx = lay.bc
    sblk = (1, 2, GDN_HEADS, GDN_DK, GDN_DK)
    state_in = pl.BlockSpec(sblk, lambda g, *m: (jnp.maximum(m[2][g] - n_ctx, 0), 0, 0, 0, 0))
    state_out = pl.BlockSpec(sblk, lambda g, *m: (jnp.minimum(m[2][g], n_ctx - 1), 0, 0, 0, 0))
    in_specs = [tok(0, 0), tok(0, 1), tok(0, 2), gate(0), tok(1, 0), tok(1, 1), tok(1, 2), gate(1), state_in]
    out_specs = [pl.BlockSpec((c, GDN_VW), lambda g, *m: (m[0][g], 0)),
                 pl.BlockSpec((c, GDN_VW), lambda g, *m: (m[1][g], 0)), state_out]
    out_shape = [jax.ShapeDtypeStruct((t, GDN_VW), BF16), jax.ShapeDtypeStruct((t, GDN_VW), BF16),
                 jax.ShapeDtypeStruct((n_ctx,) + s0.shape[1:], F32)]
    scratch = [pltpu.VMEM((2, GDN_HEADS, GDN_DK, GDN_DK), F32),
               pltpu.VMEM((18, c, c), F32), pltpu.VMEM((2, c, c), BF16)]
    return pl.pallas_call(
        functools.partial(_gdn_kernel, n_ctx=n_ctx),
        grid_spec=pltpu.PrefetchScalarGridSpec(num_scalar_prefetch=5, grid=(len(sched[0]),),
                                               in_specs=in_specs, out_specs=out_specs,
                                               scratch_shapes=scratch),
        out_shape=out_shape,
        compiler_params=_cparams(1),
        name="gdn_scan",
    )(*sched, qkv, qkv, qkv, gb, qkv, qkv, qkv, gb, s0)


def _odd_post_kernel(cr_ref, of_ref, ob_ref, z_ref, x_ref, nw_ref, wout_ref, npost_ref, g1_ref, out_ref):
    o = of_ref[...].astype(F32) + ob_ref[...].astype(F32)
    z = z_ref[...].astype(F32)
    nw = nw_ref[...]
    parts = []
    for h in range(GDN_HEADS):
        sl = slice(h * GDN_DK, (h + 1) * GDN_DK)
        parts.append(_rms(o[:, sl], nw) * _silu(z[:, sl]))
    y = jnp.concatenate(parts, axis=1)
    out = _dot(y.astype(BF16), wout_ref[...])
    out_ref[...] = x_ref[...] + g1_ref[0] * _rms(out, npost_ref[...])


def _odd_post(lay, o_f, o_b, proj, x, norm_w, w_out, n_post, g1):
    t, d = x.shape
    cr = lay.cond_rows(TOK)
    row = lambda i, *m: (i, 0)
    const = lambda i, *m: (0, 0)
    in_specs = [pl.BlockSpec((TOK, GDN_VW), row), pl.BlockSpec((TOK, GDN_VW), row),
                pl.BlockSpec((TOK, GDN_VW), lambda i, *m: (i, 3)),
                pl.BlockSpec((TOK, d), row),
                pl.BlockSpec((1, GDN_DK), const), pl.BlockSpec(w_out.shape, const), pl.BlockSpec((1, d), const),
                pl.BlockSpec((1, 1, d), lambda i, *m: (m[0][i], 0, 0))]
    return pl.pallas_call(
        _odd_post_kernel,
        grid_spec=pltpu.PrefetchScalarGridSpec(num_scalar_prefetch=1, grid=(t // TOK,),
                                               in_specs=in_specs, out_specs=pl.BlockSpec((TOK, d), row)),
        out_shape=jax.ShapeDtypeStruct((t, d), F32),
        compiler_params=_cparams(1),
        name="odd_post",
    )(cr, o_f, o_b, proj, x, norm_w.reshape(1, -1), w_out, n_post.reshape(1, d), g1)


U32 = jnp.uint32
PSUB = D_MODEL // (2 * LANES)
HI_MASK = 0xFFFF0000


def _pack_pairs(x):
    half = x.shape[1] // 2
    lo = lax.bitcast_convert_type(x[:, :half].astype(BF16).astype(F32), U32) >> 16
    hi = lax.bitcast_convert_type(x[:, half:].astype(BF16).astype(F32), U32) & jnp.uint32(HI_MASK)
    return lo | hi


def _unpack_pairs(w):
    lo = lax.bitcast_convert_type(w << 16, F32)
    hi = lax.bitcast_convert_type(w & jnp.uint32(HI_MASK), F32)
    return jnp.concatenate([lo, hi], axis=1)


def _store_token_major(ref, w):
    n = w.shape[0]
    for s in range(PSUB):
        ref[pl.ds(s, n, stride=PSUB), :] = w[:, s * LANES:(s + 1) * LANES]


def _load_token_major(ref, n):
    return jnp.concatenate([ref[pl.ds(s, n, stride=PSUB), :] for s in range(PSUB)], axis=1)


def _router_kernel(cr_ref, x_ref, g_ref, sc_ref, sh_ref, rwt_ref, rb_ref,
                   h_ref, ids_ref, wts_ref, rank_ref, cnt_ref,
                   base_ref, su_ref):
    i = pl.program_id(0)
    tm = TM_ROUTE

    @pl.when(i == 0)
    def _():
        base_ref[...] = jnp.zeros_like(base_ref)
        ii = lax.broadcasted_iota(I32, (tm, tm), 0)
        jj = lax.broadcasted_iota(I32, (tm, tm), 1)
        su_ref[...] = jnp.where(ii < jj, 1.0, 0.0).astype(BF16)

    h = _rms(x_ref[...], g_ref[...]) * (1.0 + sc_ref[0]) + sh_ref[0]
    hb = h.astype(BF16)
    _store_token_major(h_ref, _pack_pairs(h))
    h1 = hb
    w1, w2, w3 = _split3(rwt_ref[...])
    logits = _dot_nt(w1, h1) + _dot_nt(w2, h1) + _dot_nt(w3, h1)
    score = jax.nn.sigmoid(logits)
    sel = score + rb_ref[...]

    def row(a, e):
        return a[e:e + 1, :]

    gsum = []
    for gi in range(N_GROUPS):
        a, b, c, d = (row(sel, gi * EXP_PER_GROUP + j) for j in range(EXP_PER_GROUP))
        hi1, lo1 = jnp.maximum(a, b), jnp.minimum(a, b)
        hi2, lo2 = jnp.maximum(c, d), jnp.minimum(c, d)
        gsum.append(jnp.maximum(hi1, hi2) + jnp.maximum(jnp.minimum(hi1, hi2), jnp.maximum(lo1, lo2)))
    best = jnp.zeros_like(gsum[0]).astype(I32)
    cur = gsum[0]
    for gi in range(1, N_GROUPS):
        upd = gsum[gi] > cur
        best = jnp.where(upd, gi, best)
        cur = jnp.where(upd, gsum[gi], cur)

    def pick(arr, j):
        out = row(arr, j)
        for gi in range(1, N_GROUPS):
            out = jnp.where(best == gi, row(arr, gi * EXP_PER_GROUP + j), out)
        return out

    vals = [pick(sel, j) for j in range(EXP_PER_GROUP)]
    scs = [pick(score, j) for j in range(EXP_PER_GROUP)]

    def argmax_first(vs):
        idx = jnp.zeros_like(best)
        m = vs[0]
        for j in range(1, EXP_PER_GROUP):
            upd = vs[j] > m
            idx = jnp.where(upd, j, idx)
            m = jnp.where(upd, vs[j], m)
        return idx

    i1 = argmax_first(vals)
    vals2 = [jnp.where(i1 == j, -jnp.inf, vals[j]) for j in range(EXP_PER_GROUP)]
    i2 = argmax_first(vals2)

    def take(vs, idx):
        out = vs[0]
        for j in range(1, EXP_PER_GROUP):
            out = jnp.where(idx == j, vs[j], out)
        return out

    s1, s2 = take(scs, i1), take(scs, i2)
    tot = s1 + s2
    e1 = best * EXP_PER_GROUP + i1
    e2 = best * EXP_PER_GROUP + i2
    ids_ref[0:1, :] = e1
    ids_ref[1:2, :] = e2
    wts_ref[0:1, :] = s1 / tot
    wts_ref[1:2, :] = s2 / tot

    erow = lax.broadcasted_iota(I32, (N_EXPERTS, tm), 0)
    m1 = jnp.where(erow == e1, 1.0, 0.0)
    m2 = jnp.where(erow == e2, 1.0, 0.0)
    both = m1 + m2
    before = _dot(both.astype(BF16), su_ref[...]) + base_ref[...]
    rank_ref[0:1, :] = jnp.sum(m1 * before, axis=0, keepdims=True).astype(I32)
    rank_ref[1:2, :] = jnp.sum(m2 * before, axis=0, keepdims=True).astype(I32)
    base_ref[...] = base_ref[...] + jnp.sum(both, axis=1, keepdims=True)
    cnt_ref[...] = jnp.broadcast_to(base_ref[...], cnt_ref.shape).astype(I32)


def _router(lay, x, g, sc, sh, rwt, rbias):
    t, d = x.shape
    cr = lay.cond_rows(TM_ROUTE)
    tm = TM_ROUTE
    row = lambda i, cr: (i, 0)
    col = lambda i, cr: (0, i)
    const = lambda i, cr: (0, 0)
    cond = lambda i, cr: (cr[i], 0, 0)
    in_specs = [pl.BlockSpec((tm, d), row), pl.BlockSpec((1, d), const),
                pl.BlockSpec((1, 1, d), cond), pl.BlockSpec((1, 1, d), cond),
                pl.BlockSpec((N_EXPERTS, d), const), pl.BlockSpec((N_EXPERTS, 1), const)]
    out_specs = [pl.BlockSpec((tm * PSUB, LANES), row), pl.BlockSpec((2, tm), col), pl.BlockSpec((2, tm), col),
                 pl.BlockSpec((2, tm), col), pl.BlockSpec((N_EXPERTS, LANES), const)]
    out_shape = [jax.ShapeDtypeStruct((t * PSUB, LANES), U32), jax.ShapeDtypeStruct((2, t), I32),
                 jax.ShapeDtypeStruct((2, t), F32), jax.ShapeDtypeStruct((2, t), I32),
                 jax.ShapeDtypeStruct((N_EXPERTS, LANES), I32)]
    return pl.pallas_call(
        _router_kernel,
        grid_spec=pltpu.PrefetchScalarGridSpec(num_scalar_prefetch=1, grid=(t // tm,),
                                               in_specs=in_specs, out_specs=out_specs,
                                               scratch_shapes=[pltpu.VMEM((N_EXPERTS, 1), F32),
                                                               pltpu.VMEM((tm, tm), BF16)]),
        out_shape=out_shape,
        compiler_params=_cparams(1),
        name="router",
    )(cr, x, g.reshape(1, d), sc, sh, rwt, rbias.reshape(N_EXPERTS, 1))


def _row_copy(src, dst, sem):
    return pltpu.make_async_copy(src, dst, sem)


def _dispatch_kernel(ends_ref, padded_ref, pos_ref, h_ref, sorted_ref, zero_ref, sem, zsem, *, n_tiles):
    n = TOK

    @pl.when(pl.program_id(0) == 0)
    def _():
        zero_ref[...] = jnp.zeros_like(zero_ref)

        def fill_tile(start_row):
            start = pl.multiple_of(start_row * PSUB, PSUB)
            fill = _row_copy(zero_ref, sorted_ref.at[pl.ds(start, TM_EXP * PSUB)], zsem)
            fill.start()
            fill.wait()

        for e in range(N_EXPERTS):
            @pl.when(padded_ref[e] > 0)
            def _():
                fill_tile(ends_ref[e] - TM_EXP)
        for j in range(n_tiles - N_EXPERTS, n_tiles):
            @pl.when(j * TM_EXP >= ends_ref[N_EXPERTS - 1])
            def _():
                fill_tile(j * TM_EXP)

    def issue(r, carry):
        src = h_ref.at[pl.ds(pl.multiple_of(r * PSUB, PSUB), PSUB)]
        for k in range(2):
            dst = pl.multiple_of(pos_ref[0, 0, 2 * r + k] * PSUB, PSUB)
            _row_copy(src, sorted_ref.at[pl.ds(dst, PSUB)], sem).start(priority=k)
        return carry

    lax.fori_loop(0, n, issue, 0, unroll=8)
    for _ in range(2):
        _row_copy(h_ref, sorted_ref.at[pl.ds(0, n * PSUB)], sem).wait()


def _dispatch(h, pos3, ends, padded, n_rows):
    t = h.shape[0] // PSUB
    return pl.pallas_call(
        functools.partial(_dispatch_kernel, n_tiles=n_rows // TM_EXP),
        grid_spec=pltpu.PrefetchScalarGridSpec(
            num_scalar_prefetch=2, grid=(t // TOK,),
            in_specs=[pl.BlockSpec((1, 1, 2 * TOK), lambda i, *m: (i, 0, 0), memory_space=pltpu.SMEM),
                      pl.BlockSpec((TOK * PSUB, LANES), lambda i, *m: (i, 0))],
            out_specs=pl.BlockSpec(memory_space=pl.ANY),
            scratch_shapes=[pltpu.VMEM((TM_EXP * PSUB, LANES), h.dtype),
                            pltpu.SemaphoreType.DMA(()), pltpu.SemaphoreType.DMA(())]),
        out_shape=jax.ShapeDtypeStruct((n_rows * PSUB, LANES), h.dtype),
        compiler_params=_cparams(1),
        name="dispatch",
    )(ends, padded, pos3, h)


def _expert_kernel(te_ref, tv_ref, h_ref, wg_ref, wu_ref, wd_ref, y_ref, wgb_ref, wub_ref, wdb_ref):
    i = pl.program_id(0)

    @pl.when((i == 0) | (te_ref[i] != te_ref[jnp.maximum(i - 1, 0)]))
    def _():
        wgb_ref[...] = wg_ref[0, 0].astype(BF16)
        wub_ref[...] = wu_ref[0, 0].astype(BF16)
        wdb_ref[...] = wd_ref[0, 0].astype(BF16)

    @pl.when(tv_ref[i] == 1)
    def _():
        hb = _unpack_pairs(_load_token_major(h_ref, TM_EXP)).astype(BF16)
        a = _dot(hb, wgb_ref[...])
        b = _dot(hb, wub_ref[...])
        he = (_silu(a) * b).astype(BF16)
        _store_token_major(y_ref, _pack_pairs(_dot(he, wdb_ref[...])))

    @pl.when(tv_ref[i] == 0)
    def _():
        y_ref[...] = jnp.zeros_like(y_ref)


def _experts(hs, tile_expert, tile_valid, wg, wu, wd, layer):
    d = wg.shape[2]
    r = hs.shape[0] // PSUB
    row = lambda i, te, tv: (i, 0)
    wsel = lambda i, te, tv: (layer, te[i], 0, 0)
    return pl.pallas_call(
        _expert_kernel,
        grid_spec=pltpu.PrefetchScalarGridSpec(
            num_scalar_prefetch=2, grid=(r // TM_EXP,),
            in_specs=[pl.BlockSpec((TM_EXP * PSUB, LANES), lambda i, te, tv: (i * tv[i], 0)),
                      pl.BlockSpec((1, 1, d, D_EXPERT), wsel), pl.BlockSpec((1, 1, d, D_EXPERT), wsel),
                      pl.BlockSpec((1, 1, D_EXPERT, d), wsel)],
            out_specs=pl.BlockSpec((TM_EXP * PSUB, LANES), row),
            scratch_shapes=[pltpu.VMEM((d, D_EXPERT), BF16), pltpu.VMEM((d, D_EXPERT), BF16),
                            pltpu.VMEM((D_EXPERT, d), BF16)]),
        out_shape=jax.ShapeDtypeStruct(hs.shape, U32),
        compiler_params=_cparams(1),
        name="experts",
    )(tile_expert, tile_valid, hs, wg, wu, wd)


def _combine_kernel(cr_ref, pos_ref, ys_ref, wts_ref, x_ref, npost_ref, g2_ref, *rest, n_first):
    n = TOK
    buf_ref, sem = rest[-2:]
    outs = rest[:-2]

    def issue(r, carry):
        row = pl.multiple_of(r * PSUB, PSUB)
        for k in range(2):
            src = pl.multiple_of(pos_ref[0, 0, 2 * r + k] * PSUB, PSUB)
            _row_copy(ys_ref.at[pl.ds(src, PSUB)], buf_ref.at[k, pl.ds(row, PSUB)], sem).start(priority=k)
        return carry

    lax.fori_loop(0, n, issue, 0, unroll=8)
    for slot in range(2):
        _row_copy(ys_ref.at[pl.ds(0, n * PSUB)], buf_ref.at[slot], sem).wait()

    w = wts_ref[...]
    y = (_unpack_pairs(_load_token_major(buf_ref.at[0], n)) * w[:, 0:1]
         + _unpack_pairs(_load_token_major(buf_ref.at[1], n)) * w[:, 1:2])
    res = x_ref[...] + g2_ref[0] * _rms(y, npost_ref[...])
    if n_first is None:
        outs[0][...] = res
    else:
        i = pl.program_id(0)

        @pl.when(i < n_first)
        def _():
            outs[0][...] = res

        @pl.when(i >= n_first)
        def _():
            outs[1][...] = res


def _combine(lay, ys3, pos3, wts, x, n_post, g2, split):
    t, d = x.shape
    cr = lay.cond_rows(TOK)
    row = lambda i, cr: (i, 0)
    const = lambda i, cr: (0, 0)
    in_specs = [pl.BlockSpec((1, 1, 2 * TOK), lambda i, cr: (i, 0, 0), memory_space=pltpu.SMEM),
                pl.BlockSpec(memory_space=pl.ANY),
                pl.BlockSpec((TOK, 2), row),
                pl.BlockSpec((TOK, d), row),
                pl.BlockSpec((1, d), const),
                pl.BlockSpec((1, 1, d), lambda i, cr: (cr[i], 0, 0))]
    if split:
        n_first = lay.t_ctx // TOK
        out_specs = [pl.BlockSpec((TOK, d), lambda i, cr: (jnp.minimum(i, n_first - 1), 0)),
                     pl.BlockSpec((TOK, d), lambda i, cr: (jnp.maximum(i - n_first, 0), 0))]
        out_shape = [jax.ShapeDtypeStruct((lay.t_ctx, d), F32), jax.ShapeDtypeStruct((t - lay.t_ctx, d), F32)]
    else:
        n_first = None
        out_specs = [pl.BlockSpec((TOK, d), row)]
        out_shape = [jax.ShapeDtypeStruct((t, d), F32)]
    return pl.pallas_call(
        functools.partial(_combine_kernel, n_first=n_first),
        grid_spec=pltpu.PrefetchScalarGridSpec(
            num_scalar_prefetch=1, grid=(t // TOK,), in_specs=in_specs, out_specs=out_specs,
            scratch_shapes=[pltpu.VMEM((2, TOK * PSUB, LANES), U32), pltpu.SemaphoreType.DMA(())]),
        out_shape=out_shape,
        compiler_params=_cparams(1),
        name="combine",
    )(cr, pos3, ys3, wts, x, n_post.reshape(1, d), g2)


def _moe(lay, x, g_pre, sc, sh, g2, n_post, rwt, rbias, wg, wu, wd, layer, split):
    t, d = x.shape
    h, ids, wts, rank, cnt = _router(lay, x, g_pre, sc, sh, rwt, rbias)
    counts = cnt[:, 0]
    padded = ((counts + TM_EXP - 1) // TM_EXP) * TM_EXP
    ends = jnp.cumsum(padded).astype(I32)
    offs = ends - padded
    eids = jnp.arange(N_EXPERTS, dtype=I32)[:, None, None]
    pos = jnp.sum(jnp.where(ids[None] == eids, offs[:, None, None], 0), axis=0) + rank
    pos3 = pos.T.reshape(t // TOK, 1, 2 * TOK).astype(I32)
    n_rows = 2 * t + N_EXPERTS * TM_EXP
    n_tiles = n_rows // TM_EXP
    starts = jnp.arange(n_tiles, dtype=I32) * TM_EXP
    tile_expert = jnp.minimum(jnp.sum((starts[:, None] >= ends[None, :]).astype(I32), axis=1), N_EXPERTS - 1)
    tile_valid = (starts < ends[-1]).astype(I32)
    hs = _dispatch(h, pos3, ends, padded.astype(I32), n_rows)
    ys = _experts(hs, tile_expert.astype(I32), tile_valid, wg, wu, wd, layer)
    return _combine(lay, ys, pos3, wts.T, x, n_post, g2, split)


def _rope_tables(ld):
    rows = ld // GRID_W
    r = jnp.repeat(jnp.arange(rows, dtype=F32), GRID_W)
    col = jnp.tile(jnp.arange(GRID_W, dtype=F32), rows)
    quarter = RET_DK // 4
    inv = ROPE_BASE ** (-jnp.arange(quarter, dtype=F32) / quarter)
    ang = jnp.concatenate([r[:, None] * inv, col[:, None] * inv], axis=-1)
    cos, sin = jnp.cos(ang), jnp.sin(ang)
    c = jnp.concatenate([cos, cos], axis=-1)
    s = jnp.concatenate([-sin, sin], axis=-1)
    ident_c = jnp.ones((TOK, RET_DK), F32)
    ident_s = jnp.zeros((TOK, RET_DK), F32)
    return jnp.concatenate([c, ident_c], axis=0), jnp.concatenate([s, ident_s], axis=0)


def kernel(x_prompt, x_sample, state_ret, state_gdn, c, c_ctx, w_mod, b_mod, norm_mix_pre, norm_mix_post,
           norm_ffn_pre, norm_ffn_post, ev_w_in, ev_conv_w, ev_conv_ln_g, ev_conv_ln_b, ev_ret_decay, ev_w_out,
           od_w_in, od_conv_w, od_a_log, od_dt_bias, od_norm_w, od_w_out, router_w, router_bias,
           moe_w_gate, moe_w_up, moe_w_down):
    bc, lc, d = x_prompt.shape
    bd, ld, _ = x_sample.shape
    depth = w_mod.shape[0]
    lay = _Layout(bc, lc, bd, ld)
    t = lay.t

    x = (x_prompt.reshape(bc * lc, d), x_sample.reshape(bd * ld, d))
    cond = jnp.zeros((COND_PAD, d), F32).at[0].set(c_ctx).at[1:1 + bd].set(c)
    mod = _modulation(cond, w_mod, b_mod)
    mod = mod.reshape(depth, COND_PAD, N_MOD, 1, d).transpose(0, 2, 1, 3, 4)

    rope_c, rope_s = _rope_tables(ld)
    rwt = router_w.T
    ret_states, gdn_states = [], []
    for layer in range(depth):
        sh1, sc1, g1, sh2, sc2, g2 = (mod[layer, j] for j in range(N_MOD))
        i = layer // 2
        if layer % 2 == 0:
            (proj,) = _inproj(lay, x, norm_mix_pre[layer], sc1, sh1, [ev_w_in[i].astype(BF16)], [BF16])
            log_gamma = -jnp.exp(ev_ret_decay[i].astype(F32))
            o_f, o_b, s_out = _retention(lay, proj, log_gamma, rope_c, rope_s, state_ret[:, i])
            ret_states.append(s_out)
            cw = jnp.zeros((32, CONV_CH), F32).at[:CONV_W].set(ev_conv_w[i])
            x = _even_post(lay, proj, o_f, o_b, x, cw, ev_conv_ln_g[i], ev_conv_ln_b[i],
                           ev_w_out[i].astype(BF16), norm_mix_post[layer], g1)
        else:
            w_in = od_w_in[i]
            n_main = 2 * GDN_KW + 2 * GDN_VW
            w_small = jnp.zeros((d, LANES), F32).at[:, :4 * GDN_HEADS].set(w_in[:, n_main:])
            proj, ab = _inproj(lay, x, norm_mix_pre[layer], sc1, sh1,
                               [w_in[:, :n_main].astype(BF16), w_small.astype(BF16)], [BF16, F32])
            cw = jnp.zeros((8, 2 * GDN_KW + GDN_VW), F32).at[:SHORT_W].set(od_conv_w[i])
            alog_row = jnp.zeros((1, LANES), F32).at[0, :2 * GDN_HEADS].set(od_a_log[i].reshape(-1))
            dtb_row = jnp.zeros((1, LANES), F32).at[0, :2 * GDN_HEADS].set(od_dt_bias[i].reshape(-1))
            qkv, gb = _odd_pre(lay, proj, ab, cw, alog_row, dtb_row)
            o_f, o_b, s_out = _gdn(lay, qkv, gb, state_gdn[:, i])
            gdn_states.append(s_out)
            x = _odd_post(lay, o_f, o_b, proj, x, od_norm_w[i], od_w_out[i].astype(BF16),
                          norm_mix_post[layer], g1)
        outs = _moe(lay, x, norm_ffn_pre[layer], sc2, sh2, g2, norm_ffn_post[layer], rwt, router_bias,
                    moe_w_gate, moe_w_up, moe_w_down, layer, split=(layer == depth - 1))
        x = outs[0]

    y_prompt = outs[0].reshape(bc, lc, d)
    y_sample = outs[1].reshape(bd, ld, d)
    new_ret = jnp.stack(ret_states, axis=1)
    new_gdn = jnp.stack(gdn_states, axis=1)
    return y_prompt, y_sample, new_ret.astype(x_prompt.dtype), new_gdn.astype(x_prompt.dtype)
```

```python
import functools
import math

import jax
import jax.numpy as jnp
import numpy as np
from jax import lax
from jax.experimental import pallas as pl
from jax.experimental.pallas import tpu as pltpu

F32 = jnp.float32
BF16 = jnp.bfloat16
I32 = jnp.int32

D_MODEL = 1024
N_MOD = 6
EPS = 1e-6
GRID_W = 64
CONV_CH = 512
CONV_W = 31
RET_HEADS = 4
RET_DK = 128
RET_W = 512
ROPE_BASE = 10000.0
EVEN_IN = 2 * CONV_CH + 4 * RET_W
GDN_HEADS = 8
GDN_DK = 128
GDN_KW = 1024
GDN_VW = 1024
SHORT_W = 5
N_EXPERTS = 16
N_GROUPS = 4
EXP_PER_GROUP = 4
D_EXPERT = 512

LANES = 128
TOK = 256
TM_PROJ = 512
TM_ROUTE = 512
TM_EXP = 256
GDN_CHUNK = 128
HALO = 16
COND_PAD = 16
VMEM_LIMIT = 56 * 1024 * 1024
NEG_BIG = -1e30


def _cparams(n_axes=1, vmem=VMEM_LIMIT):
    return pltpu.CompilerParams(dimension_semantics=("arbitrary",) * n_axes, vmem_limit_bytes=vmem)


def _silu(x):
    return x * jax.nn.sigmoid(x)


def _rms(x, g):
    return x * lax.rsqrt(jnp.mean(x * x, axis=-1, keepdims=True) + EPS) * g


def _dot(a, b):
    return jnp.dot(a, b, preferred_element_type=F32)


def _dot_nt(a, b):
    return lax.dot_general(a, b, (((1,), (1,)), ((), ())), preferred_element_type=F32)


def _dot_tn(a, b):
    return lax.dot_general(a, b, (((0,), (0,)), ((), ())), preferred_element_type=F32)


def _split3(x):
    x1 = x.astype(BF16)
    r = x - x1.astype(F32)
    x2 = r.astype(BF16)
    x3 = (r - x2.astype(F32)).astype(BF16)
    return x1, x2, x3


class _Layout:
    def __init__(self, bc, lc, bd, ld):
        self.bc, self.lc, self.bd, self.ld = bc, lc, bd, ld
        self.t_ctx = bc * lc
        self.t = bc * lc + bd * ld
        self.n_seq = bc + bd
        assert lc % TOK == 0 and ld % TOK == 0 and self.t_ctx % TM_PROJ == 0 and ld % TM_PROJ == 0
        assert self.t % TM_ROUTE == 0 and lc % GDN_CHUNK == 0 and ld % GDN_CHUNK == 0

    def seq_of_row(self, r):
        if r < self.t_ctx:
            return r // self.lc, r % self.lc, self.lc
        r2 = r - self.t_ctx
        return self.bc + r2 // self.ld, r2 % self.ld, self.ld

    def cond_rows(self, tile):
        out = []
        for i in range(self.t // tile):
            s, _, _ = self.seq_of_row(i * tile)
            out.append(0 if s < self.bc else 1 + s - self.bc)
        return np.asarray(out, np.int32)

    def edges(self, tile):
        left, right = [], []
        for i in range(self.t // tile):
            _, p, l = self.seq_of_row(i * tile)
            left.append(int(p == 0))
            right.append(int(p + tile == l))
        return np.asarray(left, np.int32), np.asarray(right, np.int32)

    def scan_schedule(self, chunk):
        fb, bb, sq, fi, la, rf, rb = [], [], [], [], [], [], []
        ident = self.ld // chunk
        for s in range(self.n_seq):
            if s < self.bc:
                base, n = s * self.lc // chunk, self.lc // chunk
            else:
                base, n = (self.t_ctx + (s - self.bc) * self.ld) // chunk, self.ld // chunk
            for c in range(n):
                fb.append(base + c)
                bb.append(base + n - 1 - c)
                sq.append(s)
                fi.append(int(c == 0))
                la.append(int(c == n - 1))
                rf.append(ident if s < self.bc else c)
                rb.append(ident if s < self.bc else n - 1 - c)
        return [np.asarray(a, np.int32) for a in (fb, bb, sq, fi, la, rf, rb)]


def _mod_kernel(c_ref, w_ref, b_ref, o_ref):
    s = _silu(c_ref[...])
    o_ref[0] = _dot(s.astype(BF16), w_ref[0].astype(BF16)) + b_ref[0]


def _modulation(cond, w_mod, b_mod):
    depth, d, n = w_mod.shape
    nt = n // d
    return pl.pallas_call(
        _mod_kernel,
        grid=(depth, nt),
        in_specs=[pl.BlockSpec((COND_PAD, d), lambda l, j: (0, 0)),
                  pl.BlockSpec((1, d, d), lambda l, j: (l, 0, j)),
                  pl.BlockSpec((1, 1, d), lambda l, j: (l, 0, j))],
        out_specs=pl.BlockSpec((1, COND_PAD, d), lambda l, j: (l, 0, j)),
        out_shape=jax.ShapeDtypeStruct((depth, COND_PAD, n), F32),
        compiler_params=_cparams(2),
        name="modulation",
    )(cond, w_mod, b_mod.reshape(depth, 1, n))


def _two_source(x, tile):
    if isinstance(x, (tuple, list)):
        xa, xb = x
        n_first = xa.shape[0] // tile
    else:
        xa = xb = x
        n_first = x.shape[0] // tile
    d = xa.shape[1]
    specs = [pl.BlockSpec((tile, d), lambda i, *m: (jnp.minimum(i, n_first - 1), 0)),
             pl.BlockSpec((tile, d), lambda i, *m: (jnp.maximum(i - n_first, 0), 0))]
    return xa, xb, n_first, specs


def _read_two_source(xa_ref, xb_ref, n_first):
    return jnp.where(pl.program_id(0) < n_first, xa_ref[...], xb_ref[...])


def _inproj_kernel(cr_ref, xa_ref, xb_ref, g_ref, sc_ref, sh_ref, *refs, n_first):
    n = len(refs) // 2
    x = _read_two_source(xa_ref, xb_ref, n_first)
    h = _rms(x, g_ref[...]) * (1.0 + sc_ref[0]) + sh_ref[0]
    hb = h.astype(BF16)
    for w_ref, o_ref in zip(refs[:n], refs[n:]):
        o_ref[...] = _dot(hb, w_ref[...]).astype(o_ref.dtype)


def _inproj(lay, x, g, sc, sh, weights, out_dtypes):
    t, d = lay.t, g.shape[0]
    cr = lay.cond_rows(TM_PROJ)
    xa, xb, n_first, x_specs = _two_source(x, TM_PROJ)
    row = lambda i, cr: (i, 0)
    const = lambda i, cr: (0, 0)
    cond = lambda i, cr: (cr[i], 0, 0)
    in_specs = x_specs + [pl.BlockSpec((1, d), const), pl.BlockSpec((1, 1, d), cond), pl.BlockSpec((1, 1, d), cond)]
    in_specs += [pl.BlockSpec(w.shape, const) for w in weights]
    out_specs = [pl.BlockSpec((TM_PROJ, w.shape[1]), row) for w in weights]
    out_shape = [jax.ShapeDtypeStruct((t, w.shape[1]), dt) for w, dt in zip(weights, out_dtypes)]
    return pl.pallas_call(
        functools.partial(_inproj_kernel, n_first=n_first),
        grid_spec=pltpu.PrefetchScalarGridSpec(num_scalar_prefetch=1, grid=(t // TM_PROJ,),
                                               in_specs=in_specs, out_specs=out_specs),
        out_shape=out_shape,
        compiler_params=_cparams(1),
        name="inproj",
    )(cr, xa, xb, g.reshape(1, d), sc, sh, *weights)


def _rope(x, c, s):
    return x * c + pltpu.roll(x, RET_DK // 2, 1) * s


def _ret_kernel(fb_ref, bb_ref, sq_ref, fi_ref, la_ref, rf_ref, rb_ref,
                lg_ref, qf_ref, kf_ref, vf_ref, qb_ref, kb_ref, vb_ref,
                cf_ref, sf_ref, cb_ref, sb_ref, s0_ref,
                of_ref, ob_ref, sout_ref,
                st_ref, dm_ref, dec_ref, *, n_ctx):
    g = pl.program_id(0)
    c = TOK

    @pl.when(g == 0)
    def _():
        ii = lax.broadcasted_iota(I32, (c, c), 0)
        jj = lax.broadcasted_iota(I32, (c, c), 1)
        diff = (ii - jj).astype(F32)
        ri = lax.broadcasted_iota(I32, (c, RET_DK), 0).astype(F32)
        for h in range(RET_HEADS):
            lf = lg_ref[0, h]
            lb = lg_ref[1, h]
            low = jnp.exp(lf * jnp.maximum(diff, 0.0))
            up = jnp.exp(lb * jnp.maximum(-diff, 0.0))
            dm_ref[h] = jnp.where(diff > 0, low, jnp.where(diff < 0, up, 2.0))
            dec_ref[0, h] = jnp.exp(lf * (ri + 1.0))
            dec_ref[1, h] = jnp.exp(lf * (c - 1.0 - ri))
            dec_ref[2, h] = jnp.exp(lb * (c - ri))
            dec_ref[3, h] = jnp.exp(lb * ri)

    @pl.when((fi_ref[g] == 1) & (sq_ref[g] < n_ctx))
    def _():
        st_ref[...] = jnp.zeros_like(st_ref)

    @pl.when((fi_ref[g] == 1) & (sq_ref[g] >= n_ctx))
    def _():
        st_ref[...] = s0_ref[0]

    scale = RET_DK ** -0.5
    cf, sf, cb, sb = cf_ref[...], sf_ref[...], cb_ref[...], sb_ref[...]
    zero_row = jnp.zeros((1, RET_DK), F32)
    for h in range(RET_HEADS):
        sl = slice(h * RET_DK, (h + 1) * RET_DK)
        lf = lg_ref[0, h]
        lb = lg_ref[1, h]
        q = _rope(qf_ref[:, sl].astype(F32), cf, sf)
        k = _rope(kf_ref[:, sl].astype(F32), cf, sf) * scale
        v = vf_ref[:, sl]
        s = _dot_nt(q.astype(BF16), k.astype(BF16)) * dm_ref[h]
        st = st_ref[0, h]
        o = _dot(s.astype(BF16), v) + _dot((q * dec_ref[0, h]).astype(BF16), st.astype(BF16))
        of_ref[:, sl] = o
        st_ref[0, h] = jnp.exp(zero_row + lf * c) * st + _dot_tn((k * dec_ref[1, h]).astype(BF16), v)
        q = _rope(qb_ref[:, sl].astype(F32), cb, sb)
        k = _rope(kb_ref[:, sl].astype(F32), cb, sb) * scale
        v = vb_ref[:, sl]
        st = st_ref[1, h]
        ob_ref[:, sl] = _dot((q * dec_ref[2, h]).astype(BF16), st.astype(BF16))
        st_ref[1, h] = jnp.exp(zero_row + lb * c) * st + _dot_tn((k * dec_ref[3, h]).astype(BF16), v)

    @pl.when((la_ref[g] == 1) & (sq_ref[g] < n_ctx))
    def _():
        sout_ref[0] = st_ref[...]


def _retention(lay, proj, log_gamma, rope_c, rope_s, s0):
    t = proj.shape[0]
    sched = lay.scan_schedule(TOK)
    qcol, kcol, vcol = (2 * CONV_CH) // RET_W, (2 * CONV_CH) // RET_W + 1, (2 * CONV_CH) // RET_W + 2

    def tok(which, col):
        return pl.BlockSpec((TOK, RET_W), lambda g, *m: (m[which][g], col))

    def rope(which):
        return pl.BlockSpec((TOK, RET_DK), lambda g, *m: (m[which][g], 0))

    n_ctx = lay.bc
    sblk = (1, 2, RET_HEADS, RET_DK, RET_DK)
    state_in = pl.BlockSpec(sblk, lambda g, *m: (jnp.maximum(m[2][g] - n_ctx, 0), 0, 0, 0, 0))
    state_out = pl.BlockSpec(sblk, lambda g, *m: (jnp.minimum(m[2][g], n_ctx - 1), 0, 0, 0, 0))
    in_specs = [pl.BlockSpec(memory_space=pltpu.SMEM),
                tok(0, qcol), tok(0, kcol), tok(0, vcol), tok(1, qcol), tok(1, kcol), tok(1, vcol),
                rope(5), rope(5), rope(6), rope(6), state_in]
    out_specs = [pl.BlockSpec((TOK, RET_W), lambda g, *m: (m[0][g], 0)),
                 pl.BlockSpec((TOK, RET_W), lambda g, *m: (m[1][g], 0)), state_out]
    out_shape = [jax.ShapeDtypeStruct((t, RET_W), F32), jax.ShapeDtypeStruct((t, RET_W), F32),
                 jax.ShapeDtypeStruct((n_ctx,) + s0.shape[1:], F32)]
    scratch = [pltpu.VMEM((2, RET_HEADS, RET_DK, RET_DK), F32),
               pltpu.VMEM((RET_HEADS, TOK, TOK), F32),
               pltpu.VMEM((4, RET_HEADS, TOK, RET_DK), F32)]
    return pl.pallas_call(
        functools.partial(_ret_kernel, n_ctx=n_ctx),
        grid_spec=pltpu.PrefetchScalarGridSpec(num_scalar_prefetch=7, grid=(len(sched[0]),),
                                               in_specs=in_specs, out_specs=out_specs,
                                               scratch_shapes=scratch),
        out_shape=out_shape,
        compiler_params=_cparams(1),
        name="retention",
    )(*sched, log_gamma, proj, proj, proj, proj, proj, proj, rope_c, rope_s, rope_c, rope_s, s0)


def _even_post_kernel(cr_ref, le_ref, re_ref,
                      glu_ref, prev_ref, next_ref, gt_ref, of_ref, ob_ref, xa_ref, xb_ref,
                      cw_ref, lng_ref, lnb_ref, wout_ref, npost_ref, g1_ref,
                      out_ref, buf_ref, *, n_first):
    i = pl.program_id(0)

    def glu(r):
        r = r.astype(F32)
        return r[:, :CONV_CH] * jax.nn.sigmoid(r[:, CONV_CH:])

    keep_l = jnp.where(le_ref[i] == 1, 0.0, 1.0)
    keep_r = jnp.where(re_ref[i] == 1, 0.0, 1.0)
    buf_ref[0:HALO, :] = glu(prev_ref[...]) * keep_l
    buf_ref[HALO:HALO + TOK, :] = glu(glu_ref[...])
    buf_ref[HALO + TOK:, :] = glu(next_ref[...]) * keep_r
    off = HALO - CONV_W // 2
    sub = 8
    cols = []
    for cb in range(CONV_CH // LANES):
        ls = slice(cb * LANES, (cb + 1) * LANES)
        acc = jnp.zeros((TOK, LANES), F32)
        for r in range(sub):
            part = None
            for m in range((off + CONV_W - 1) // sub + 1):
                j = sub * m + r - off
                if 0 <= j < CONV_W:
                    term = buf_ref[sub * m:sub * m + TOK + sub, ls] * cw_ref[j:j + 1, ls]
                    part = term if part is None else part + term
            acc = acc + part[r:r + TOK, :]
        cols.append(acc)
    acc = jnp.concatenate(cols, axis=1)
    mu = jnp.mean(acc, axis=-1, keepdims=True)
    ac = acc - mu
    y = ac * lax.rsqrt(jnp.mean(ac * ac, axis=-1, keepdims=True) + EPS) * lng_ref[...] + lnb_ref[...]
    conv_out = _silu(y)

    o = of_ref[...] + ob_ref[...]
    gt = gt_ref[...].astype(F32)
    parts = []
    for h in range(RET_HEADS):
        sl = slice(h * RET_DK, (h + 1) * RET_DK)
        oh = o[:, sl]
        oc = oh - jnp.mean(oh, axis=-1, keepdims=True)
        on = oc * lax.rsqrt(jnp.mean(oc * oc, axis=-1, keepdims=True) + EPS)
        parts.append(on * _silu(gt[:, sl]))
    ret_out = jnp.concatenate(parts, axis=1)

    out = _dot(conv_out.astype(BF16), wout_ref[0:CONV_CH, :]) + _dot(ret_out.astype(BF16), wout_ref[CONV_CH:, :])
    out_ref[...] = _read_two_source(xa_ref, xb_ref, n_first) + g1_ref[0] * _rms(out, npost_ref[...])


def _even_post(lay, proj, o_f, o_b, x, conv_w, ln_g, ln_b, w_out, n_post, g1):
    t, d = lay.t, n_post.shape[0]
    cr = lay.cond_rows(TOK)
    le, re = lay.edges(TOK)
    xa, xb, n_first, x_specs = _two_source(x, TOK)
    hb = TOK // HALO
    n_halo = t // HALO
    row = lambda i, *m: (i, 0)
    const = lambda i, *m: (0, 0)
    in_specs = [pl.BlockSpec((TOK, 2 * CONV_CH), row),
                pl.BlockSpec((HALO, 2 * CONV_CH), lambda i, *m: (jnp.maximum(i * hb - 1, 0), 0)),
                pl.BlockSpec((HALO, 2 * CONV_CH), lambda i, *m: (jnp.minimum((i + 1) * hb, n_halo - 1), 0)),
                pl.BlockSpec((TOK, RET_W), lambda i, *m: (i, EVEN_IN // RET_W - 1)),
                pl.BlockSpec((TOK, RET_W), row), pl.BlockSpec((TOK, RET_W), row)] + x_specs + [
                pl.BlockSpec(conv_w.shape, const), pl.BlockSpec((1, CONV_CH), const), pl.BlockSpec((1, CONV_CH), const),
                pl.BlockSpec(w_out.shape, const), pl.BlockSpec((1, d), const),
                pl.BlockSpec((1, 1, d), lambda i, *m: (m[0][i], 0, 0))]
    return pl.pallas_call(
        functools.partial(_even_post_kernel, n_first=n_first),
        grid_spec=pltpu.PrefetchScalarGridSpec(num_scalar_prefetch=3, grid=(t // TOK,),
                                               in_specs=in_specs, out_specs=pl.BlockSpec((TOK, d), row),
                                               scratch_shapes=[pltpu.VMEM((TOK + 2 * HALO, CONV_CH), F32)]),
        out_shape=jax.ShapeDtypeStruct((t, d), F32),
        compiler_params=_cparams(1),
        name="even_post",
    )(cr, le, re, proj, proj, proj, proj, o_f, o_b, xa, xb, conv_w, ln_g.reshape(1, -1), ln_b.reshape(1, -1),
      w_out, n_post.reshape(1, d), g1)


XHALO = 8
QKV_BLOCK = 256


def _odd_in_kernel(cr_ref, le_ref, re_ref, x_ref, xp_ref, xn_ref, g_ref, sc_ref, sh_ref,
                   wqkv_ref, wz_ref, wab_ref, cw_ref, alog_ref, dtb_ref,
                   qkv_ref, z_ref, gb_ref):
    i = pl.program_id(0)
    n = TOK
    keep_l = jnp.where(le_ref[i] == 1, 0.0, 1.0)
    keep_r = jnp.where(re_ref[i] == 1, 0.0, 1.0)
    xs = jnp.concatenate([xp_ref[...], x_ref[...], xn_ref[...]], axis=0)
    h = _rms(xs, g_ref[...]) * (1.0 + sc_ref[0]) + sh_ref[0]
    rows = lax.broadcasted_iota(I32, (n + 2 * XHALO, 1), 0)
    keep = jnp.where(rows < XHALO, keep_l, jnp.where(rows >= n + XHALO, keep_r, 1.0))
    hb = (h * keep).astype(BF16)
    off = XHALO - SHORT_W // 2
    heads_per_block = QKV_BLOCK // GDN_DK
    for blk in range((2 * GDN_KW + GDN_VW) // QKV_BLOCK):
        cs = slice(blk * QKV_BLOCK, (blk + 1) * QKV_BLOCK)
        p = _dot(hb, wqkv_ref[:, cs])
        acc = jnp.zeros((n, QKV_BLOCK), F32)
        for j in range(SHORT_W):
            acc = acc + p[off + j:off + j + n, :] * cw_ref[j:j + 1, cs]
        y = _silu(acc)
        for hh in range(heads_per_block):
            head = blk * heads_per_block + hh
            yh = y[:, hh * GDN_DK:(hh + 1) * GDN_DK]
            if head < 2 * GDN_HEADS:
                yh = yh * lax.rsqrt(jnp.sum(yh * yh, axis=-1, keepdims=True) + EPS)
                if head < GDN_HEADS:
                    yh = yh * (GDN_DK ** -0.5)
            qkv_ref[:, head * GDN_DK:(head + 1) * GDN_DK] = yh.astype(qkv_ref.dtype)
    hc = hb[XHALO:XHALO + n, :]
    z_ref[...] = _dot(hc, wz_ref[...]).astype(z_ref.dtype)
    ab = _dot(hc, wab_ref[...])
    zz = ab + dtb_ref[...]
    softplus = jnp.maximum(zz, 0.0) + jnp.log(1.0 + jnp.exp(-jnp.abs(zz)))
    gate = -jnp.exp(alog_ref[...]) * softplus
    beta = jax.nn.sigmoid(ab)
    lane = lax.broadcasted_iota(I32, ab.shape, 1)
    gb_ref[...] = jnp.where(lane < 2 * GDN_HEADS, gate, beta)


def _odd_in(lay, x, g, sc, sh, w_qkv, w_z, w_ab, conv_w, alog_row, dtb_row):
    t, d = x.shape
    w = w_qkv.shape[1]
    cr = lay.cond_rows(TOK)
    le, re = lay.edges(TOK)
    hb = TOK // XHALO
    n_halo = t // XHALO
    row = lambda i, *m: (i, 0)
    const = lambda i, *m: (0, 0)
    cond = lambda i, *m: (m[0][i], 0, 0)
    in_specs = [pl.BlockSpec((TOK, d), row),
                pl.BlockSpec((XHALO, d), lambda i, *m: (jnp.maximum(i * hb - 1, 0), 0)),
                pl.BlockSpec((XHALO, d), lambda i, *m: (jnp.minimum((i + 1) * hb, n_halo - 1), 0)),
                pl.BlockSpec((1, d), const), pl.BlockSpec((1, 1, d), cond), pl.BlockSpec((1, 1, d), cond),
                pl.BlockSpec(w_qkv.shape, const), pl.BlockSpec(w_z.shape, const), pl.BlockSpec(w_ab.shape, const),
                pl.BlockSpec(conv_w.shape, const), pl.BlockSpec((1, LANES), const), pl.BlockSpec((1, LANES), const)]
    out_specs = [pl.BlockSpec((TOK, w), row), pl.BlockSpec((TOK, w_z.shape[1]), row), pl.BlockSpec((TOK, LANES), row)]
    out_shape = [jax.ShapeDtypeStruct((t, w), BF16), jax.ShapeDtypeStruct((t, w_z.shape[1]), BF16),
                 jax.ShapeDtypeStruct((t, LANES), F32)]
    return pl.pallas_call(
        _odd_in_kernel,
        grid_spec=pltpu.PrefetchScalarGridSpec(num_scalar_prefetch=3, grid=(t // TOK,),
                                               in_specs=in_specs, out_specs=out_specs),
        out_shape=out_shape,
        compiler_params=_cparams(1),
        name="odd_in",
    )(cr, le, re, x, x, x, g.reshape(1, d), sc, sh, w_qkv, w_z, w_ab, conv_w, alog_row, dtb_row)


_GDN_LEVELS = tuple(2 ** p for p in range(int(math.log2(GDN_CHUNK))))


def _gdn_masks(msk_ref, tri_ref):
    c = GDN_CHUNK
    ii = lax.broadcasted_iota(I32, (c, c), 0)
    jj = lax.broadcasted_iota(I32, (c, c), 1)
    one = jnp.ones((c, c), F32)
    zero = jnp.zeros((c, c), F32)
    for rev in (0, 1):
        a, b = (ii, jj) if rev == 0 else (jj, ii)
        base = rev * 9
        msk_ref[base + 0] = jnp.where(a >= b, one, zero)
        msk_ref[base + 1] = jnp.where(a > b, one, zero)
        for l, m in enumerate(_GDN_LEVELS):
            sh = int(math.log2(m))
            ab_, bb_ = a >> sh, b >> sh
            hit = ((ab_ & 1) == 1) & (bb_ == ab_ - 1)
            msk_ref[base + 2 + l] = jnp.where(hit, one, zero)
        tri_ref[rev] = jnp.where(a >= b, one, zero).astype(BF16)


def _gdn_chunk_step(dirs, st_ref, msk_ref, tri_ref):
    c = GDN_CHUNK
    nh = GDN_HEADS
    probs = [(rev, h) for rev in range(2) for h in range(nh)]
    ones = jnp.ones((c, c), BF16)
    gcum, gcum_t, e_all, kdec_all, elast_all, gbs = [], [], [], [], [], []
    for rev in range(2):
        gb = dirs[rev][3][...]
        g1, g2, g3 = _split3(gb)
        tri = tri_ref[rev]
        gc = _dot(tri, g1) + _dot(tri, g2) + _dot(tri, g3)
        gl = _dot(ones, g1) + _dot(ones, g2) + _dot(ones, g3)
        gbs.append(gb)
        gcum.append(gc)
        gcum_t.append(gc.T)
        e_all.append(jnp.exp(gc))
        kdec_all.append(jnp.exp(gl - gc))
        elast_all.append(jnp.exp(gl))

    def col(rev, h):
        return rev * nh + h

    def hsl(h):
        return slice(h * GDN_DK, (h + 1) * GDN_DK)

    st = [st_ref[rev, h] for rev, h in probs]

    a, attn, kbs = [], [], []
    for rev, h in probs:
        base = rev * 9
        p = col(rev, h)
        q_ref, k_ref = dirs[rev][0], dirs[rev][1]
        k = k_ref[:, hsl(h)]
        gcb = jnp.broadcast_to(gcum[rev][:, p:p + 1], (c, c))
        grb = jnp.broadcast_to(gcum_t[rev][p:p + 1, :], (c, c))
        decay = jnp.exp(jnp.where(msk_ref[base] > 0, gcb - grb, NEG_BIG))
        kb = k.astype(F32) * gbs[rev][:, 2 * nh + p:2 * nh + p + 1]
        kbs.append(kb)
        a.append(_dot_nt(kb.astype(BF16), k) * decay * msk_ref[base + 1])
        attn.append((_dot_nt(q_ref[:, hsl(h)], k) * decay).astype(BF16))

    tinv = []
    for i, (rev, h) in enumerate(probs):
        base = rev * 9
        tinv.append(msk_ref[base] - msk_ref[base + 1] - a[i] * msk_ref[base + 2])
    for l in range(1, len(_GDN_LEVELS)):
        bt = []
        for i, (rev, h) in enumerate(probs):
            bt.append(_dot((a[i] * msk_ref[rev * 9 + 2 + l]).astype(BF16), tinv[i].astype(BF16)))
        for i in range(len(probs)):
            tinv[i] = tinv[i] - _dot(tinv[i].astype(BF16), bt[i].astype(BF16))

    sol = []
    for i, (rev, h) in enumerate(probs):
        p = col(rev, h)
        v = dirs[rev][2][:, hsl(h)].astype(F32)
        beta = gbs[rev][:, 2 * nh + p:2 * nh + p + 1]
        rhs = jnp.concatenate([v * beta, kbs[i] * e_all[rev][:, p:p + 1]], axis=1)
        sol.append(_dot(tinv[i].astype(BF16), rhs.astype(BF16)))
    v_new = []
    for i in range(len(probs)):
        u, w = sol[i][:, :GDN_DK], sol[i][:, GDN_DK:]
        v_new.append((u - _dot(w.astype(BF16), st[i].astype(BF16))).astype(BF16))
    for i, (rev, h) in enumerate(probs):
        p = col(rev, h)
        q = dirs[rev][0][:, hsl(h)].astype(F32)
        qs = (q * e_all[rev][:, p:p + 1]).astype(BF16)
        dirs[rev][4][:, hsl(h)] = (_dot(qs, st[i].astype(BF16)) + _dot(attn[i], v_new[i])).astype(BF16)
    new_st = []
    for i, (rev, h) in enumerate(probs):
        p = col(rev, h)
        kd = (dirs[rev][1][:, hsl(h)].astype(F32) * kdec_all[rev][:, p:p + 1]).astype(BF16)
        new_st.append(st[i] * elast_all[rev][:, p:p + 1] + _dot_tn(kd, v_new[i]))
    for i, (rev, h) in enumerate(probs):
        st_ref[rev, h] = new_st[i]


def _gdn_kernel(fb_ref, bb_ref, sq_ref, fi_ref, la_ref,
                qf_ref, kf_ref, vf_ref, gf_ref, qb_ref, kb_ref, vb_ref, gbk_ref, s0_ref,
                of_ref, ob_ref, sout_ref,
                st_ref, msk_ref, tri_ref, *, n_ctx):
    g = pl.program_id(0)

    @pl.when(g == 0)
    def _():
        _gdn_masks(msk_ref, tri_ref)

    @pl.when((fi_ref[g] == 1) & (sq_ref[g] < n_ctx))
    def _():
        st_ref[...] = jnp.zeros_like(st_ref)

    @pl.when((fi_ref[g] == 1) & (sq_ref[g] >= n_ctx))
    def _():
        st_ref[...] = s0_ref[0]

    _gdn_chunk_step(((qf_ref, kf_ref, vf_ref, gf_ref, of_ref), (qb_ref, kb_ref, vb_ref, gbk_ref, ob_ref)),
                    st_ref, msk_ref, tri_ref)

    @pl.when((la_ref[g] == 1) & (sq_ref[g] < n_ctx))
    def _():
        sout_ref[0] = st_ref[...]


def _gdn(lay, qkv, gb, s0):
    t = qkv.shape[0]
    c = GDN_CHUNK
    sched = lay.scan_schedule(c)[:5]

    def tok(which, col):
        return pl.BlockSpec((c, GDN_KW), lambda g, *m: (m[which][g], col))

    def gate(which):
        return pl.BlockSpec((c, LANES), lambda g, *m: (m[which][g], 0))

    n_ctx = lay.bc
    sblk = (1, 2, GDN_HEADS, GDN_DK, GDN_DK)
    state_in = pl.BlockSpec(sblk, lambda g, *m: (jnp.maximum(m[2][g] - n_ctx, 0), 0, 0, 0, 0))
    state_out = pl.BlockSpec(sblk, lambda g, *m: (jnp.minimum(m[2][g], n_ctx - 1), 0, 0, 0, 0))
    in_specs = [tok(0, 0), tok(0, 1), tok(0, 2), gate(0), tok(1, 0), tok(1, 1), tok(1, 2), gate(1), state_in]
    out_specs = [pl.BlockSpec((c, GDN_VW), lambda g, *m: (m[0][g], 0)),
                 pl.BlockSpec((c, GDN_VW), lambda g, *m: (m[1][g], 0)), state_out]
    out_shape = [jax.ShapeDtypeStruct((t, GDN_VW), BF16), jax.ShapeDtypeStruct((t, GDN_VW), BF16),
                 jax.ShapeDtypeStruct((n_ctx,) + s0.shape[1:], F32)]
    scratch = [pltpu.VMEM((2, GDN_HEADS, GDN_DK, GDN_DK), F32),
               pltpu.VMEM((18, c, c), F32), pltpu.VMEM((2, c, c), BF16)]
    return pl.pallas_call(
        functools.partial(_gdn_kernel, n_ctx=n_ctx),
        grid_spec=pltpu.PrefetchScalarGridSpec(num_scalar_prefetch=5, grid=(len(sched[0]),),
                                               in_specs=in_specs, out_specs=out_specs,
                                               scratch_shapes=scratch),
        out_shape=out_shape,
        compiler_params=_cparams(1),
        name="gdn_scan",
    )(*sched, qkv, qkv, qkv, gb, qkv, qkv, qkv, gb, s0)


def _odd_post_kernel(cr_ref, of_ref, ob_ref, z_ref, x_ref, nw_ref, wout_ref, npost_ref, g1_ref, out_ref):
    o = of_ref[...].astype(F32) + ob_ref[...].astype(F32)
    z = z_ref[...].astype(F32)
    nw = nw_ref[...]
    parts = []
    for h in range(GDN_HEADS):
        sl = slice(h * GDN_DK, (h + 1) * GDN_DK)
        parts.append(_rms(o[:, sl], nw) * _silu(z[:, sl]))
    y = jnp.concatenate(parts, axis=1)
    out = _dot(y.astype(BF16), wout_ref[...])
    out_ref[...] = x_ref[...] + g1_ref[0] * _rms(out, npost_ref[...])


def _odd_post(lay, o_f, o_b, proj, x, norm_w, w_out, n_post, g1):
    t, d = x.shape
    cr = lay.cond_rows(TOK)
    row = lambda i, *m: (i, 0)
    const = lambda i, *m: (0, 0)
    in_specs = [pl.BlockSpec((TOK, GDN_VW), row), pl.BlockSpec((TOK, GDN_VW), row),
                pl.BlockSpec((TOK, GDN_VW), row),
                pl.BlockSpec((TOK, d), row),
                pl.BlockSpec((1, GDN_DK), const), pl.BlockSpec(w_out.shape, const), pl.BlockSpec((1, d), const),
                pl.BlockSpec((1, 1, d), lambda i, *m: (m[0][i], 0, 0))]
    return pl.pallas_call(
        _odd_post_kernel,
        grid_spec=pltpu.PrefetchScalarGridSpec(num_scalar_prefetch=1, grid=(t // TOK,),
                                               in_specs=in_specs, out_specs=pl.BlockSpec((TOK, d), row)),
        out_shape=jax.ShapeDtypeStruct((t, d), F32),
        compiler_params=_cparams(1),
        name="odd_post",
    )(cr, o_f, o_b, proj, x, norm_w.reshape(1, -1), w_out, n_post.reshape(1, d), g1)


U32 = jnp.uint32
PSUB = D_MODEL // (2 * LANES)
HI_MASK = 0xFFFF0000
N_PAIRS = EXP_PER_GROUP * (EXP_PER_GROUP - 1) // 2
N_CLASSES = N_GROUPS * N_PAIRS
CLS_PAD = 32
PAIR_LO = (0, 0, 0, 1, 1, 2)
PAIR_HI = (1, 2, 3, 2, 3, 3)


def _pack_pairs(x):
    half = x.shape[1] // 2
    lo = lax.bitcast_convert_type(x[:, :half].astype(BF16).astype(F32), U32) >> 16
    hi = lax.bitcast_convert_type(x[:, half:].astype(BF16).astype(F32), U32) & jnp.uint32(HI_MASK)
    return lo | hi


def _unpack_pairs(w):
    lo = lax.bitcast_convert_type(w << 16, F32)
    hi = lax.bitcast_convert_type(w & jnp.uint32(HI_MASK), F32)
    return jnp.concatenate([lo, hi], axis=1)


def _store_token_major(ref, w):
    n = w.shape[0]
    for s in range(PSUB):
        ref[pl.ds(s, n, stride=PSUB), :] = w[:, s * LANES:(s + 1) * LANES]


def _load_token_major(ref, n):
    return jnp.concatenate([ref[pl.ds(s, n, stride=PSUB), :] for s in range(PSUB)], axis=1)


def _router_kernel(cr_ref, x_ref, g_ref, sc_ref, sh_ref, rwt_ref, rb_ref,
                   h_ref, route_ref, wts_ref, cnt_ref,
                   base_ref, su_ref):
    i = pl.program_id(0)
    tm = TM_ROUTE

    @pl.when(i == 0)
    def _():
        base_ref[...] = jnp.zeros_like(base_ref)
        ii = lax.broadcasted_iota(I32, (tm, tm), 0)
        jj = lax.broadcasted_iota(I32, (tm, tm), 1)
        su_ref[...] = jnp.where(ii < jj, 1.0, 0.0).astype(BF16)

    h = _rms(x_ref[...], g_ref[...]) * (1.0 + sc_ref[0]) + sh_ref[0]
    hb = h.astype(BF16)
    _store_token_major(h_ref, _pack_pairs(h))
    h1 = hb
    w1, w2, w3 = _split3(rwt_ref[...])
    logits = _dot_nt(w1, h1) + _dot_nt(w2, h1) + _dot_nt(w3, h1)
    score = jax.nn.sigmoid(logits)
    sel = score + rb_ref[...]

    def row(a, e):
        return a[e:e + 1, :]

    gsum = []
    for gi in range(N_GROUPS):
        a, b, c, d = (row(sel, gi * EXP_PER_GROUP + j) for j in range(EXP_PER_GROUP))
        hi1, lo1 = jnp.maximum(a, b), jnp.minimum(a, b)
        hi2, lo2 = jnp.maximum(c, d), jnp.minimum(c, d)
        gsum.append(jnp.maximum(hi1, hi2) + jnp.maximum(jnp.minimum(hi1, hi2), jnp.maximum(lo1, lo2)))
    best = jnp.zeros_like(gsum[0]).astype(I32)
    cur = gsum[0]
    for gi in range(1, N_GROUPS):
        upd = gsum[gi] > cur
        best = jnp.where(upd, gi, best)
        cur = jnp.where(upd, gsum[gi], cur)

    def pick(arr, j):
        out = row(arr, j)
        for gi in range(1, N_GROUPS):
            out = jnp.where(best == gi, row(arr, gi * EXP_PER_GROUP + j), out)
        return out

    vals = [pick(sel, j) for j in range(EXP_PER_GROUP)]
    scs = [pick(score, j) for j in range(EXP_PER_GROUP)]

    def argmax_first(vs):
        idx = jnp.zeros_like(best)
        m = vs[0]
        for j in range(1, EXP_PER_GROUP):
            upd = vs[j] > m
            idx = jnp.where(upd, j, idx)
            m = jnp.where(upd, vs[j], m)
        return idx

    i1 = argmax_first(vals)
    vals2 = [jnp.where(i1 == j, -jnp.inf, vals[j]) for j in range(EXP_PER_GROUP)]
    i2 = argmax_first(vals2)

    def take(vs, idx):
        out = vs[0]
        for j in range(1, EXP_PER_GROUP):
            out = jnp.where(idx == j, vs[j], out)
        return out

    s1, s2 = take(scs, i1), take(scs, i2)
    tot = s1 + s2
    first_low = i1 < i2
    lo = jnp.minimum(i1, i2)
    hi = jnp.maximum(i1, i2)
    pair = jnp.where(lo == 0, hi - 1, jnp.where(lo == 1, hi + 1, N_PAIRS - 1))
    cls = best * N_PAIRS + pair
    wts_ref[0:1, :] = jnp.where(first_low, s1, s2) / tot
    wts_ref[1:2, :] = jnp.where(first_low, s2, s1) / tot

    crow = lax.broadcasted_iota(I32, (CLS_PAD, tm), 0)
    onehot = jnp.where(crow == cls, 1.0, 0.0)
    before = _dot(onehot.astype(BF16), su_ref[...]) + base_ref[...]
    route_ref[0:1, :] = cls
    route_ref[1:2, :] = jnp.sum(onehot * before, axis=0, keepdims=True).astype(I32)
    base_ref[...] = base_ref[...] + jnp.sum(onehot, axis=1, keepdims=True)
    cnt_ref[...] = jnp.broadcast_to(base_ref[...], cnt_ref.shape).astype(I32)


def _router(lay, x, g, sc, sh, rwt, rbias):
    t, d = x.shape
    cr = lay.cond_rows(TM_ROUTE)
    tm = TM_ROUTE
    row = lambda i, cr: (i, 0)
    col = lambda i, cr: (0, i)
    const = lambda i, cr: (0, 0)
    cond = lambda i, cr: (cr[i], 0, 0)
    in_specs = [pl.BlockSpec((tm, d), row), pl.BlockSpec((1, d), const),
                pl.BlockSpec((1, 1, d), cond), pl.BlockSpec((1, 1, d), cond),
                pl.BlockSpec((N_EXPERTS, d), const), pl.BlockSpec((N_EXPERTS, 1), const)]
    out_specs = [pl.BlockSpec((tm * PSUB, LANES), row), pl.BlockSpec((2, tm), col), pl.BlockSpec((2, tm), col),
                 pl.BlockSpec((CLS_PAD, LANES), const)]
    out_shape = [jax.ShapeDtypeStruct((t * PSUB, LANES), U32), jax.ShapeDtypeStruct((2, t), I32),
                 jax.ShapeDtypeStruct((2, t), F32), jax.ShapeDtypeStruct((CLS_PAD, LANES), I32)]
    return pl.pallas_call(
        _router_kernel,
        grid_spec=pltpu.PrefetchScalarGridSpec(num_scalar_prefetch=1, grid=(t // tm,),
                                               in_specs=in_specs, out_specs=out_specs,
                                               scratch_shapes=[pltpu.VMEM((CLS_PAD, 1), F32),
                                                               pltpu.VMEM((tm, tm), BF16)]),
        out_shape=out_shape,
        compiler_params=_cparams(1),
        name="router",
    )(cr, x, g.reshape(1, d), sc, sh, rwt, rbias.reshape(N_EXPERTS, 1))


def _row_copy(src, dst, sem):
    return pltpu.make_async_copy(src, dst, sem)


def _dispatch_kernel(ends_ref, padded_ref, pos_ref, h_ref, sorted_ref, zero_ref, sem, zsem, *, n_tiles):
    n = TOK

    @pl.when(pl.program_id(0) == 0)
    def _():
        zero_ref[...] = jnp.zeros_like(zero_ref)

        def fill_tile(start_row):
            start = pl.multiple_of(start_row * PSUB, PSUB)
            fill = _row_copy(zero_ref, sorted_ref.at[pl.ds(start, TM_EXP * PSUB)], zsem)
            fill.start()
            fill.wait()

        for c in range(N_CLASSES):
            @pl.when(padded_ref[c] > 0)
            def _():
                fill_tile(ends_ref[c] - TM_EXP)
        for j in range(n_tiles - N_CLASSES, n_tiles):
            @pl.when(j * TM_EXP >= ends_ref[N_CLASSES - 1])
            def _():
                fill_tile(j * TM_EXP)

    def issue(r, carry):
        for k in range(2):
            tok = 2 * r + k
            src = h_ref.at[pl.ds(pl.multiple_of(tok * PSUB, PSUB), PSUB)]
            dst = pl.multiple_of(pos_ref[0, 0, tok] * PSUB, PSUB)
            _row_copy(src, sorted_ref.at[pl.ds(dst, PSUB)], sem).start(priority=k)
        return carry

    lax.fori_loop(0, n // 2, issue, 0, unroll=8)
    _row_copy(h_ref, sorted_ref.at[pl.ds(0, n * PSUB)], sem).wait()


def _dispatch(h, pos3, ends, padded, n_rows):
    t = h.shape[0] // PSUB
    return pl.pallas_call(
        functools.partial(_dispatch_kernel, n_tiles=n_rows // TM_EXP),
        grid_spec=pltpu.PrefetchScalarGridSpec(
            num_scalar_prefetch=2, grid=(t // TOK,),
            in_specs=[pl.BlockSpec((1, 1, TOK), lambda i, *m: (i, 0, 0), memory_space=pltpu.SMEM),
                      pl.BlockSpec((TOK * PSUB, LANES), lambda i, *m: (i, 0))],
            out_specs=pl.BlockSpec(memory_space=pl.ANY),
            scratch_shapes=[pltpu.VMEM((TM_EXP * PSUB, LANES), h.dtype),
                            pltpu.SemaphoreType.DMA(()), pltpu.SemaphoreType.DMA(())]),
        out_shape=jax.ShapeDtypeStruct((n_rows * PSUB, LANES), h.dtype),
        compiler_params=_cparams(1),
        name="dispatch",
    )(ends, padded, pos3, h)


def _expert_kernel(ea_ref, eb_ref, tv_ref, h_ref, *refs):
    i = pl.program_id(0)
    w_refs, y_ref, wb_refs = refs[:6], refs[6], refs[7:]
    prev = jnp.maximum(i - 1, 0)

    for slot, e_ref in enumerate((ea_ref, eb_ref)):
        @pl.when((i == 0) | (e_ref[i] != e_ref[prev]))
        def _():
            for j in range(3):
                wb_refs[3 * slot + j][...] = w_refs[3 * slot + j][0, 0].astype(BF16)

    @pl.when(tv_ref[i] == 1)
    def _():
        hb = _unpack_pairs(_load_token_major(h_ref, TM_EXP)).astype(BF16)
        for slot in range(2):
            wg, wu, wd = (wb_refs[3 * slot + j][...] for j in range(3))
            he = (_silu(_dot(hb, wg)) * _dot(hb, wu)).astype(BF16)
            y = _pack_pairs(_dot(he, wd))
            for s in range(PSUB):
                y_ref[pl.ds(slot * PSUB + s, TM_EXP, stride=2 * PSUB), :] = y[:, s * LANES:(s + 1) * LANES]

    @pl.when(tv_ref[i] == 0)
    def _():
        y_ref[...] = jnp.zeros_like(y_ref)


def _experts(hs, tile_ea, tile_eb, tile_valid, wg, wu, wd, layer):
    d = wg.shape[2]
    r = hs.shape[0] // PSUB
    sel_a = lambda i, ea, eb, tv: (layer, ea[i], 0, 0)
    sel_b = lambda i, ea, eb, tv: (layer, eb[i], 0, 0)
    w_specs = []
    for sel in (sel_a, sel_b):
        w_specs += [pl.BlockSpec((1, 1, d, D_EXPERT), sel), pl.BlockSpec((1, 1, d, D_EXPERT), sel),
                    pl.BlockSpec((1, 1, D_EXPERT, d), sel)]
    wb = [pltpu.VMEM((d, D_EXPERT), BF16), pltpu.VMEM((d, D_EXPERT), BF16), pltpu.VMEM((D_EXPERT, d), BF16)]
    return pl.pallas_call(
        _expert_kernel,
        grid_spec=pltpu.PrefetchScalarGridSpec(
            num_scalar_prefetch=3, grid=(r // TM_EXP,),
            in_specs=[pl.BlockSpec((TM_EXP * PSUB, LANES), lambda i, ea, eb, tv: (i * tv[i], 0))] + w_specs,
            out_specs=pl.BlockSpec((TM_EXP * 2 * PSUB, LANES), lambda i, ea, eb, tv: (i, 0)),
            scratch_shapes=wb + wb),
        out_shape=jax.ShapeDtypeStruct((r * 2 * PSUB, LANES), U32),
        compiler_params=_cparams(1),
        name="experts",
    )(tile_ea, tile_eb, tile_valid, hs, wg, wu, wd, wg, wu, wd)


def _combine_kernel(cr_ref, pos_ref, ys_ref, wts_ref, x_ref, npost_ref, g2_ref, *rest, n_first):
    n = TOK
    buf_ref, sem = rest[-2:]
    outs = rest[:-2]

    wide = 2 * PSUB

    def issue(r, carry):
        for k in range(2):
            tok = 2 * r + k
            src = pl.multiple_of(pos_ref[0, 0, tok] * wide, wide)
            dst = pl.multiple_of(tok * wide, wide)
            _row_copy(ys_ref.at[pl.ds(src, wide)], buf_ref.at[pl.ds(dst, wide)], sem).start(priority=k)
        return carry

    lax.fori_loop(0, n // 2, issue, 0, unroll=8)
    _row_copy(ys_ref.at[pl.ds(0, n * wide)], buf_ref, sem).wait()

    w = wts_ref[...]
    y_lo = _unpack_pairs(jnp.concatenate([buf_ref[pl.ds(s, n, stride=wide), :] for s in range(PSUB)], axis=1))
    y_hi = _unpack_pairs(jnp.concatenate([buf_ref[pl.ds(PSUB + s, n, stride=wide), :] for s in range(PSUB)],
                                         axis=1))
    y = y_lo * w[:, 0:1] + y_hi * w[:, 1:2]
    res = x_ref[...] + g2_ref[0] * _rms(y, npost_ref[...])
    if n_first is None:
        outs[0][...] = res
    else:
        i = pl.program_id(0)

        @pl.when(i < n_first)
        def _():
            outs[0][...] = res

        @pl.when(i >= n_first)
        def _():
            outs[1][...] = res


def _combine(lay, ys3, pos3, wts, x, n_post, g2, split):
    t, d = x.shape
    cr = lay.cond_rows(TOK)
    row = lambda i, cr: (i, 0)
    const = lambda i, cr: (0, 0)
    in_specs = [pl.BlockSpec((1, 1, TOK), lambda i, cr: (i, 0, 0), memory_space=pltpu.SMEM),
                pl.BlockSpec(memory_space=pl.ANY),
                pl.BlockSpec((TOK, 2), row),
                pl.BlockSpec((TOK, d), row),
                pl.BlockSpec((1, d), const),
                pl.BlockSpec((1, 1, d), lambda i, cr: (cr[i], 0, 0))]
    if split:
        n_first = lay.t_ctx // TOK
        out_specs = [pl.BlockSpec((TOK, d), lambda i, cr: (jnp.minimum(i, n_first - 1), 0)),
                     pl.BlockSpec((TOK, d), lambda i, cr: (jnp.maximum(i - n_first, 0), 0))]
        out_shape = [jax.ShapeDtypeStruct((lay.t_ctx, d), F32), jax.ShapeDtypeStruct((t - lay.t_ctx, d), F32)]
    else:
        n_first = None
        out_specs = [pl.BlockSpec((TOK, d), row)]
        out_shape = [jax.ShapeDtypeStruct((t, d), F32)]
    return pl.pallas_call(
        functools.partial(_combine_kernel, n_first=n_first),
        grid_spec=pltpu.PrefetchScalarGridSpec(
            num_scalar_prefetch=1, grid=(t // TOK,), in_specs=in_specs, out_specs=out_specs,
            scratch_shapes=[pltpu.VMEM((TOK * 2 * PSUB, LANES), U32), pltpu.SemaphoreType.DMA(())]),
        out_shape=out_shape,
        compiler_params=_cparams(1),
        name="combine",
    )(cr, pos3, ys3, wts, x, n_post.reshape(1, d), g2)


def _moe(lay, x, g_pre, sc, sh, g2, n_post, rwt, rbias, wg, wu, wd, layer, split):
    t, d = x.shape
    h, route, wts, cnt = _router(lay, x, g_pre, sc, sh, rwt, rbias)
    cls, rank = route[0], route[1]
    counts = cnt[:N_CLASSES, 0]
    padded = ((counts + TM_EXP - 1) // TM_EXP) * TM_EXP
    ends = jnp.cumsum(padded).astype(I32)
    offs = ends - padded
    cids = jnp.arange(N_CLASSES, dtype=I32)[:, None]
    pos = jnp.sum(jnp.where(cls[None] == cids, offs[:, None], 0), axis=0) + rank
    pos3 = pos.reshape(t // TOK, 1, TOK).astype(I32)
    n_rows = t + N_CLASSES * TM_EXP
    n_tiles = n_rows // TM_EXP
    starts = jnp.arange(n_tiles, dtype=I32) * TM_EXP
    tile_cls = jnp.minimum(jnp.sum((starts[:, None] >= ends[None, :]).astype(I32), axis=1), N_CLASSES - 1)
    tile_valid = (starts < ends[-1]).astype(I32)
    grp, pair = tile_cls // N_PAIRS, tile_cls % N_PAIRS
    pids = jnp.arange(N_PAIRS, dtype=I32)[None, :]
    lo = jnp.sum(jnp.where(pair[:, None] == pids, jnp.asarray(PAIR_LO, I32)[None, :], 0), axis=1)
    hi = jnp.sum(jnp.where(pair[:, None] == pids, jnp.asarray(PAIR_HI, I32)[None, :], 0), axis=1)
    tile_ea = (grp * EXP_PER_GROUP + lo).astype(I32)
    tile_eb = (grp * EXP_PER_GROUP + hi).astype(I32)
    hs = _dispatch(h, pos3, ends, padded.astype(I32), n_rows)
    ys = _experts(hs, tile_ea, tile_eb, tile_valid, wg, wu, wd, layer)
    return _combine(lay, ys, pos3, wts.T, x, n_post, g2, split)


def _rope_tables(ld):
    rows = ld // GRID_W
    r = jnp.repeat(jnp.arange(rows, dtype=F32), GRID_W)
    col = jnp.tile(jnp.arange(GRID_W, dtype=F32), rows)
    quarter = RET_DK // 4
    inv = ROPE_BASE ** (-jnp.arange(quarter, dtype=F32) / quarter)
    ang = jnp.concatenate([r[:, None] * inv, col[:, None] * inv], axis=-1)
    cos, sin = jnp.cos(ang), jnp.sin(ang)
    c = jnp.concatenate([cos, cos], axis=-1)
    s = jnp.concatenate([-sin, sin], axis=-1)
    ident_c = jnp.ones((TOK, RET_DK), F32)
    ident_s = jnp.zeros((TOK, RET_DK), F32)
    return jnp.concatenate([c, ident_c], axis=0), jnp.concatenate([s, ident_s], axis=0)


def kernel(x_prompt, x_sample, state_ret, state_gdn, c, c_ctx, w_mod, b_mod, norm_mix_pre, norm_mix_post,
           norm_ffn_pre, norm_ffn_post, ev_w_in, ev_conv_w, ev_conv_ln_g, ev_conv_ln_b, ev_ret_decay, ev_w_out,
           od_w_in, od_conv_w, od_a_log, od_dt_bias, od_norm_w, od_w_out, router_w, router_bias,
           moe_w_gate, moe_w_up, moe_w_down):
    bc, lc, d = x_prompt.shape
    bd, ld, _ = x_sample.shape
    depth = w_mod.shape[0]
    lay = _Layout(bc, lc, bd, ld)
    t = lay.t

    x = (x_prompt.reshape(bc * lc, d), x_sample.reshape(bd * ld, d))
    cond = jnp.zeros((COND_PAD, d), F32).at[0].set(c_ctx).at[1:1 + bd].set(c)
    mod = _modulation(cond, w_mod, b_mod)
    mod = mod.reshape(depth, COND_PAD, N_MOD, 1, d).transpose(0, 2, 1, 3, 4)

    rope_c, rope_s = _rope_tables(ld)
    rwt = router_w.T
    ret_states, gdn_states = [], []
    for layer in range(depth):
        sh1, sc1, g1, sh2, sc2, g2 = (mod[layer, j] for j in range(N_MOD))
        i = layer // 2
        if layer % 2 == 0:
            (proj,) = _inproj(lay, x, norm_mix_pre[layer], sc1, sh1, [ev_w_in[i].astype(BF16)], [BF16])
            log_gamma = -jnp.exp(ev_ret_decay[i].astype(F32))
            o_f, o_b, s_out = _retention(lay, proj, log_gamma, rope_c, rope_s, state_ret[:, i])
            ret_states.append(s_out)
            cw = jnp.zeros((32, CONV_CH), F32).at[:CONV_W].set(ev_conv_w[i])
            x = _even_post(lay, proj, o_f, o_b, x, cw, ev_conv_ln_g[i], ev_conv_ln_b[i],
                           ev_w_out[i].astype(BF16), norm_mix_post[layer], g1)
        else:
            w_in = od_w_in[i]
            n_qkv = 2 * GDN_KW + GDN_VW
            n_main = n_qkv + GDN_VW
            w_ab = jnp.zeros((d, LANES), F32).at[:, :4 * GDN_HEADS].set(w_in[:, n_main:])
            cw = jnp.zeros((8, n_qkv), F32).at[:SHORT_W].set(od_conv_w[i])
            alog_row = jnp.zeros((1, LANES), F32).at[0, :2 * GDN_HEADS].set(od_a_log[i].reshape(-1))
            dtb_row = jnp.zeros((1, LANES), F32).at[0, :2 * GDN_HEADS].set(od_dt_bias[i].reshape(-1))
            qkv, z, gb = _odd_in(lay, x, norm_mix_pre[layer], sc1, sh1, w_in[:, :n_qkv].astype(BF16),
                                 w_in[:, n_qkv:n_main].astype(BF16), w_ab.astype(BF16), cw, alog_row, dtb_row)
            o_f, o_b, s_out = _gdn(lay, qkv, gb, state_gdn[:, i])
            gdn_states.append(s_out)
            x = _odd_post(lay, o_f, o_b, z, x, od_norm_w[i], od_w_out[i].astype(BF16),
                          norm_mix_post[layer], g1)
        outs = _moe(lay, x, norm_ffn_pre[layer], sc2, sh2, g2, norm_ffn_post[layer], rwt, router_bias,
                    moe_w_gate, moe_w_up, moe_w_down, layer, split=(layer == depth - 1))
        x = outs[0]

    y_prompt = outs[0].reshape(bc, lc, d)
    y_sample = outs[1].reshape(bd, ld, d)
    new_ret = jnp.stack(ret_states, axis=1)
    new_gdn = jnp.stack(gdn_states, axis=1)
    return y_prompt, y_sample, new_ret.astype(x_prompt.dtype), new_gdn.astype(x_prompt.dtype)
```

```python
import functools
import math

import jax
import jax.numpy as jnp
import numpy as np
from jax import lax
from jax.experimental import pallas as pl
from jax.experimental.pallas import tpu as pltpu

F32 = jnp.float32
BF16 = jnp.bfloat16
I32 = jnp.int32

D_MODEL = 1024
N_MOD = 6
EPS = 1e-6
GRID_W = 64
CONV_CH = 512
CONV_W = 31
RET_HEADS = 4
RET_DK = 128
RET_W = 512
ROPE_BASE = 10000.0
EVEN_IN = 2 * CONV_CH + 4 * RET_W
GDN_HEADS = 8
GDN_DK = 128
GDN_KW = 1024
GDN_VW = 1024
SHORT_W = 5
N_EXPERTS = 16
N_GROUPS = 4
EXP_PER_GROUP = 4
D_EXPERT = 512

LANES = 128
TOK = 256
TM_PROJ = 512
TM_ROUTE = 512
TM_EXP = 512
TM_DISP = 512
GDN_CHUNK = 128
HALO = 16
COND_PAD = 16
VMEM_LIMIT = 56 * 1024 * 1024
NEG_BIG = -1e30


def _cparams(n_axes=1, vmem=VMEM_LIMIT):
    return pltpu.CompilerParams(dimension_semantics=("arbitrary",) * n_axes, vmem_limit_bytes=vmem)


def _silu(x):
    return x * jax.nn.sigmoid(x)


def _rms(x, g):
    return x * lax.rsqrt(jnp.mean(x * x, axis=-1, keepdims=True) + EPS) * g


def _dot(a, b):
    return jnp.dot(a, b, preferred_element_type=F32)


def _dot_nt(a, b):
    return lax.dot_general(a, b, (((1,), (1,)), ((), ())), preferred_element_type=F32)


def _dot_tn(a, b):
    return lax.dot_general(a, b, (((0,), (0,)), ((), ())), preferred_element_type=F32)


def _split3(x):
    x1 = x.astype(BF16)
    r = x - x1.astype(F32)
    x2 = r.astype(BF16)
    x3 = (r - x2.astype(F32)).astype(BF16)
    return x1, x2, x3


class _Layout:
    def __init__(self, bc, lc, bd, ld):
        self.bc, self.lc, self.bd, self.ld = bc, lc, bd, ld
        self.t_ctx = bc * lc
        self.t = bc * lc + bd * ld
        self.n_seq = bc + bd
        assert lc % TOK == 0 and ld % TOK == 0 and self.t_ctx % TM_PROJ == 0 and ld % TM_PROJ == 0
        assert self.t % TM_ROUTE == 0 and lc % GDN_CHUNK == 0 and ld % GDN_CHUNK == 0

    def seq_of_row(self, r):
        if r < self.t_ctx:
            return r // self.lc, r % self.lc, self.lc
        r2 = r - self.t_ctx
        return self.bc + r2 // self.ld, r2 % self.ld, self.ld

    def cond_rows(self, tile):
        out = []
        for i in range(self.t // tile):
            s, _, _ = self.seq_of_row(i * tile)
            out.append(0 if s < self.bc else 1 + s - self.bc)
        return np.asarray(out, np.int32)

    def edges(self, tile):
        left, right = [], []
        for i in range(self.t // tile):
            _, p, l = self.seq_of_row(i * tile)
            left.append(int(p == 0))
            right.append(int(p + tile == l))
        return np.asarray(left, np.int32), np.asarray(right, np.int32)

    def scan_schedule(self, chunk):
        fb, bb, sq, fi, la, rf, rb = [], [], [], [], [], [], []
        ident = self.ld // chunk
        for s in range(self.n_seq):
            if s < self.bc:
                base, n = s * self.lc // chunk, self.lc // chunk
            else:
                base, n = (self.t_ctx + (s - self.bc) * self.ld) // chunk, self.ld // chunk
            for c in range(n):
                fb.append(base + c)
                bb.append(base + n - 1 - c)
                sq.append(s)
                fi.append(int(c == 0))
                la.append(int(c == n - 1))
                rf.append(ident if s < self.bc else c)
                rb.append(ident if s < self.bc else n - 1 - c)
        return [np.asarray(a, np.int32) for a in (fb, bb, sq, fi, la, rf, rb)]


def _mod_kernel(c_ref, w_ref, b_ref, o_ref):
    s = _silu(c_ref[...])
    o_ref[0] = _dot(s.astype(BF16), w_ref[0].astype(BF16)) + b_ref[0]


def _modulation(cond, w_mod, b_mod):
    depth, d, n = w_mod.shape
    nt = n // d
    return pl.pallas_call(
        _mod_kernel,
        grid=(depth, nt),
        in_specs=[pl.BlockSpec((COND_PAD, d), lambda l, j: (0, 0)),
                  pl.BlockSpec((1, d, d), lambda l, j: (l, 0, j)),
                  pl.BlockSpec((1, 1, d), lambda l, j: (l, 0, j))],
        out_specs=pl.BlockSpec((1, COND_PAD, d), lambda l, j: (l, 0, j)),
        out_shape=jax.ShapeDtypeStruct((depth, COND_PAD, n), F32),
        compiler_params=_cparams(2),
        name="modulation",
    )(cond, w_mod, b_mod.reshape(depth, 1, n))


def _two_source(x, tile):
    if isinstance(x, (tuple, list)):
        xa, xb = x
        n_first = xa.shape[0] // tile
    else:
        xa = xb = x
        n_first = x.shape[0] // tile
    d = xa.shape[1]
    specs = [pl.BlockSpec((tile, d), lambda i, *m: (jnp.minimum(i, n_first - 1), 0)),
             pl.BlockSpec((tile, d), lambda i, *m: (jnp.maximum(i - n_first, 0), 0))]
    return xa, xb, n_first, specs


def _read_two_source(xa_ref, xb_ref, n_first):
    return jnp.where(pl.program_id(0) < n_first, xa_ref[...], xb_ref[...])


def _inproj_kernel(cr_ref, xa_ref, xb_ref, g_ref, sc_ref, sh_ref, *refs, n_first):
    n = len(refs) // 2
    x = _read_two_source(xa_ref, xb_ref, n_first)
    h = _rms(x, g_ref[...]) * (1.0 + sc_ref[0]) + sh_ref[0]
    hb = h.astype(BF16)
    for w_ref, o_ref in zip(refs[:n], refs[n:]):
        o_ref[...] = _dot(hb, w_ref[...]).astype(o_ref.dtype)


def _inproj(lay, x, g, sc, sh, weights, out_dtypes):
    t, d = lay.t, g.shape[0]
    cr = lay.cond_rows(TM_PROJ)
    xa, xb, n_first, x_specs = _two_source(x, TM_PROJ)
    row = lambda i, cr: (i, 0)
    const = lambda i, cr: (0, 0)
    cond = lambda i, cr: (cr[i], 0, 0)
    in_specs = x_specs + [pl.BlockSpec((1, d), const), pl.BlockSpec((1, 1, d), cond), pl.BlockSpec((1, 1, d), cond)]
    in_specs += [pl.BlockSpec(w.shape, const) for w in weights]
    out_specs = [pl.BlockSpec((TM_PROJ, w.shape[1]), row) for w in weights]
    out_shape = [jax.ShapeDtypeStruct((t, w.shape[1]), dt) for w, dt in zip(weights, out_dtypes)]
    return pl.pallas_call(
        functools.partial(_inproj_kernel, n_first=n_first),
        grid_spec=pltpu.PrefetchScalarGridSpec(num_scalar_prefetch=1, grid=(t // TM_PROJ,),
                                               in_specs=in_specs, out_specs=out_specs),
        out_shape=out_shape,
        compiler_params=_cparams(1),
        name="inproj",
    )(cr, xa, xb, g.reshape(1, d), sc, sh, *weights)


def _rope(x, c, s):
    return x * c + pltpu.roll(x, RET_DK // 2, 1) * s


def _ret_kernel(fb_ref, bb_ref, sq_ref, fi_ref, la_ref, rf_ref, rb_ref,
                lg_ref, qf_ref, kf_ref, vf_ref, qb_ref, kb_ref, vb_ref,
                cf_ref, sf_ref, cb_ref, sb_ref, s0_ref,
                of_ref, ob_ref, sout_ref,
                st_ref, dm_ref, dec_ref, *, n_ctx):
    g = pl.program_id(0)
    c = TOK

    @pl.when(g == 0)
    def _():
        ii = lax.broadcasted_iota(I32, (c, c), 0)
        jj = lax.broadcasted_iota(I32, (c, c), 1)
        diff = (ii - jj).astype(F32)
        ri = lax.broadcasted_iota(I32, (c, RET_DK), 0).astype(F32)
        for h in range(RET_HEADS):
            lf = lg_ref[0, h]
            lb = lg_ref[1, h]
            low = jnp.exp(lf * jnp.maximum(diff, 0.0))
            up = jnp.exp(lb * jnp.maximum(-diff, 0.0))
            dm_ref[h] = jnp.where(diff > 0, low, jnp.where(diff < 0, up, 2.0))
            dec_ref[0, h] = jnp.exp(lf * (ri + 1.0))
            dec_ref[1, h] = jnp.exp(lf * (c - 1.0 - ri))
            dec_ref[2, h] = jnp.exp(lb * (c - ri))
            dec_ref[3, h] = jnp.exp(lb * ri)

    @pl.when((fi_ref[g] == 1) & (sq_ref[g] < n_ctx))
    def _():
        st_ref[...] = jnp.zeros_like(st_ref)

    @pl.when((fi_ref[g] == 1) & (sq_ref[g] >= n_ctx))
    def _():
        st_ref[...] = s0_ref[0]

    scale = RET_DK ** -0.5
    cf, sf, cb, sb = cf_ref[...], sf_ref[...], cb_ref[...], sb_ref[...]
    zero_row = jnp.zeros((1, RET_DK), F32)
    for h in range(RET_HEADS):
        sl = slice(h * RET_DK, (h + 1) * RET_DK)
        lf = lg_ref[0, h]
        lb = lg_ref[1, h]
        q = _rope(qf_ref[:, sl].astype(F32), cf, sf)
        k = _rope(kf_ref[:, sl].astype(F32), cf, sf) * scale
        v = vf_ref[:, sl]
        s = _dot_nt(q.astype(BF16), k.astype(BF16)) * dm_ref[h]
        st = st_ref[0, h]
        o = _dot(s.astype(BF16), v) + _dot((q * dec_ref[0, h]).astype(BF16), st.astype(BF16))
        of_ref[:, sl] = o
        st_ref[0, h] = jnp.exp(zero_row + lf * c) * st + _dot_tn((k * dec_ref[1, h]).astype(BF16), v)
        q = _rope(qb_ref[:, sl].astype(F32), cb, sb)
        k = _rope(kb_ref[:, sl].astype(F32), cb, sb) * scale
        v = vb_ref[:, sl]
        st = st_ref[1, h]
        ob_ref[:, sl] = _dot((q * dec_ref[2, h]).astype(BF16), st.astype(BF16))
        st_ref[1, h] = jnp.exp(zero_row + lb * c) * st + _dot_tn((k * dec_ref[3, h]).astype(BF16), v)

    @pl.when((la_ref[g] == 1) & (sq_ref[g] < n_ctx))
    def _():
        sout_ref[0] = st_ref[...]


def _retention(lay, proj, log_gamma, rope_c, rope_s, s0):
    t = proj.shape[0]
    sched = lay.scan_schedule(TOK)
    qcol, kcol, vcol = (2 * CONV_CH) // RET_W, (2 * CONV_CH) // RET_W + 1, (2 * CONV_CH) // RET_W + 2

    def tok(which, col):
        return pl.BlockSpec((TOK, RET_W), lambda g, *m: (m[which][g], col))

    def rope(which):
        return pl.BlockSpec((TOK, RET_DK), lambda g, *m: (m[which][g], 0))

    n_ctx = lay.bc
    sblk = (1, 2, RET_HEADS, RET_DK, RET_DK)
    state_in = pl.BlockSpec(sblk, lambda g, *m: (jnp.maximum(m[2][g] - n_ctx, 0), 0, 0, 0, 0))
    state_out = pl.BlockSpec(sblk, lambda g, *m: (jnp.minimum(m[2][g], n_ctx - 1), 0, 0, 0, 0))
    in_specs = [pl.BlockSpec(memory_space=pltpu.SMEM),
                tok(0, qcol), tok(0, kcol), tok(0, vcol), tok(1, qcol), tok(1, kcol), tok(1, vcol),
                rope(5), rope(5), rope(6), rope(6), state_in]
    out_specs = [pl.BlockSpec((TOK, RET_W), lambda g, *m: (m[0][g], 0)),
                 pl.BlockSpec((TOK, RET_W), lambda g, *m: (m[1][g], 0)), state_out]
    out_shape = [jax.ShapeDtypeStruct((t, RET_W), F32), jax.ShapeDtypeStruct((t, RET_W), F32),
                 jax.ShapeDtypeStruct((n_ctx,) + s0.shape[1:], F32)]
    scratch = [pltpu.VMEM((2, RET_HEADS, RET_DK, RET_DK), F32),
               pltpu.VMEM((RET_HEADS, TOK, TOK), F32),
               pltpu.VMEM((4, RET_HEADS, TOK, RET_DK), F32)]
    return pl.pallas_call(
        functools.partial(_ret_kernel, n_ctx=n_ctx),
        grid_spec=pltpu.PrefetchScalarGridSpec(num_scalar_prefetch=7, grid=(len(sched[0]),),
                                               in_specs=in_specs, out_specs=out_specs,
                                               scratch_shapes=scratch),
        out_shape=out_shape,
        compiler_params=_cparams(1),
        name="retention",
    )(*sched, log_gamma, proj, proj, proj, proj, proj, proj, rope_c, rope_s, rope_c, rope_s, s0)


def _even_post_kernel(cr_ref, le_ref, re_ref,
                      glu_ref, prev_ref, next_ref, gt_ref, of_ref, ob_ref, xa_ref, xb_ref,
                      cw_ref, lng_ref, lnb_ref, wout_ref, npost_ref, g1_ref,
                      out_ref, buf_ref, *, n_first):
    i = pl.program_id(0)

    def glu(r):
        r = r.astype(F32)
        return r[:, :CONV_CH] * jax.nn.sigmoid(r[:, CONV_CH:])

    keep_l = jnp.where(le_ref[i] == 1, 0.0, 1.0)
    keep_r = jnp.where(re_ref[i] == 1, 0.0, 1.0)
    buf_ref[0:HALO, :] = glu(prev_ref[...]) * keep_l
    buf_ref[HALO:HALO + TOK, :] = glu(glu_ref[...])
    buf_ref[HALO + TOK:, :] = glu(next_ref[...]) * keep_r
    off = HALO - CONV_W // 2
    sub = 8
    cols = []
    for cb in range(CONV_CH // LANES):
        ls = slice(cb * LANES, (cb + 1) * LANES)
        acc = jnp.zeros((TOK, LANES), F32)
        for r in range(sub):
            part = None
            for m in range((off + CONV_W - 1) // sub + 1):
                j = sub * m + r - off
                if 0 <= j < CONV_W:
                    term = buf_ref[sub * m:sub * m + TOK + sub, ls] * cw_ref[j:j + 1, ls]
                    part = term if part is None else part + term
            acc = acc + part[r:r + TOK, :]
        cols.append(acc)
    acc = jnp.concatenate(cols, axis=1)
    mu = jnp.mean(acc, axis=-1, keepdims=True)
    ac = acc - mu
    y = ac * lax.rsqrt(jnp.mean(ac * ac, axis=-1, keepdims=True) + EPS) * lng_ref[...] + lnb_ref[...]
    conv_out = _silu(y)

    o = of_ref[...] + ob_ref[...]
    gt = gt_ref[...].astype(F32)
    parts = []
    for h in range(RET_HEADS):
        sl = slice(h * RET_DK, (h + 1) * RET_DK)
        oh = o[:, sl]
        oc = oh - jnp.mean(oh, axis=-1, keepdims=True)
        on = oc * lax.rsqrt(jnp.mean(oc * oc, axis=-1, keepdims=True) + EPS)
        parts.append(on * _silu(gt[:, sl]))
    ret_out = jnp.concatenate(parts, axis=1)

    out = _dot(conv_out.astype(BF16), wout_ref[0:CONV_CH, :]) + _dot(ret_out.astype(BF16), wout_ref[CONV_CH:, :])
    out_ref[...] = _read_two_source(xa_ref, xb_ref, n_first) + g1_ref[0] * _rms(out, npost_ref[...])


def _even_post(lay, proj, o_f, o_b, x, conv_w, ln_g, ln_b, w_out, n_post, g1):
    t, d = lay.t, n_post.shape[0]
    cr = lay.cond_rows(TOK)
    le, re = lay.edges(TOK)
    xa, xb, n_first, x_specs = _two_source(x, TOK)
    hb = TOK // HALO
    n_halo = t // HALO
    row = lambda i, *m: (i, 0)
    const = lambda i, *m: (0, 0)
    in_specs = [pl.BlockSpec((TOK, 2 * CONV_CH), row),
                pl.BlockSpec((HALO, 2 * CONV_CH), lambda i, *m: (jnp.maximum(i * hb - 1, 0), 0)),
                pl.BlockSpec((HALO, 2 * CONV_CH), lambda i, *m: (jnp.minimum((i + 1) * hb, n_halo - 1), 0)),
                pl.BlockSpec((TOK, RET_W), lambda i, *m: (i, EVEN_IN // RET_W - 1)),
                pl.BlockSpec((TOK, RET_W), row), pl.BlockSpec((TOK, RET_W), row)] + x_specs + [
                pl.BlockSpec(conv_w.shape, const), pl.BlockSpec((1, CONV_CH), const), pl.BlockSpec((1, CONV_CH), const),
                pl.BlockSpec(w_out.shape, const), pl.BlockSpec((1, d), const),
                pl.BlockSpec((1, 1, d), lambda i, *m: (m[0][i], 0, 0))]
    return pl.pallas_call(
        functools.partial(_even_post_kernel, n_first=n_first),
        grid_spec=pltpu.PrefetchScalarGridSpec(num_scalar_prefetch=3, grid=(t // TOK,),
                                               in_specs=in_specs, out_specs=pl.BlockSpec((TOK, d), row),
                                               scratch_shapes=[pltpu.VMEM((TOK + 2 * HALO, CONV_CH), F32)]),
        out_shape=jax.ShapeDtypeStruct((t, d), F32),
        compiler_params=_cparams(1),
        name="even_post",
    )(cr, le, re, proj, proj, proj, proj, o_f, o_b, xa, xb, conv_w, ln_g.reshape(1, -1), ln_b.reshape(1, -1),
      w_out, n_post.reshape(1, d), g1)


XHALO = 8
QKV_BLOCK = 256


def _odd_in_kernel(cr_ref, le_ref, re_ref, x_ref, xp_ref, xn_ref, g_ref, sc_ref, sh_ref,
                   wqkv_ref, wz_ref, wab_ref, cw_ref, alog_ref, dtb_ref,
                   qkv_ref, z_ref, gb_ref):
    i = pl.program_id(0)
    n = TOK
    keep_l = jnp.where(le_ref[i] == 1, 0.0, 1.0)
    keep_r = jnp.where(re_ref[i] == 1, 0.0, 1.0)
    xs = jnp.concatenate([xp_ref[...], x_ref[...], xn_ref[...]], axis=0)
    h = _rms(xs, g_ref[...]) * (1.0 + sc_ref[0]) + sh_ref[0]
    rows = lax.broadcasted_iota(I32, (n + 2 * XHALO, 1), 0)
    keep = jnp.where(rows < XHALO, keep_l, jnp.where(rows >= n + XHALO, keep_r, 1.0))
    hb = (h * keep).astype(BF16)
    off = XHALO - SHORT_W // 2
    heads_per_block = QKV_BLOCK // GDN_DK
    for blk in range((2 * GDN_KW + GDN_VW) // QKV_BLOCK):
        cs = slice(blk * QKV_BLOCK, (blk + 1) * QKV_BLOCK)
        p = _dot(hb, wqkv_ref[:, cs])
        acc = jnp.zeros((n, QKV_BLOCK), F32)
        for j in range(SHORT_W):
            acc = acc + p[off + j:off + j + n, :] * cw_ref[j:j + 1, cs]
        y = _silu(acc)
        for hh in range(heads_per_block):
            head = blk * heads_per_block + hh
            yh = y[:, hh * GDN_DK:(hh + 1) * GDN_DK]
            if head < 2 * GDN_HEADS:
                yh = yh * lax.rsqrt(jnp.sum(yh * yh, axis=-1, keepdims=True) + EPS)
                if head < GDN_HEADS:
                    yh = yh * (GDN_DK ** -0.5)
            qkv_ref[:, head * GDN_DK:(head + 1) * GDN_DK] = yh.astype(qkv_ref.dtype)
    hc = hb[XHALO:XHALO + n, :]
    z_ref[...] = _dot(hc, wz_ref[...]).astype(z_ref.dtype)
    ab = _dot(hc, wab_ref[...])
    zz = ab + dtb_ref[...]
    softplus = jnp.maximum(zz, 0.0) + jnp.log(1.0 + jnp.exp(-jnp.abs(zz)))
    gate = -jnp.exp(alog_ref[...]) * softplus
    beta = jax.nn.sigmoid(ab)
    lane = lax.broadcasted_iota(I32, ab.shape, 1)
    gb_ref[...] = jnp.where(lane < 2 * GDN_HEADS, gate, beta)


def _odd_in(lay, x, g, sc, sh, w_qkv, w_z, w_ab, conv_w, alog_row, dtb_row):
    t, d = x.shape
    w = w_qkv.shape[1]
    cr = lay.cond_rows(TOK)
    le, re = lay.edges(TOK)
    hb = TOK // XHALO
    n_halo = t // XHALO
    row = lambda i, *m: (i, 0)
    const = lambda i, *m: (0, 0)
    cond = lambda i, *m: (m[0][i], 0, 0)
    in_specs = [pl.BlockSpec((TOK, d), row),
                pl.BlockSpec((XHALO, d), lambda i, *m: (jnp.maximum(i * hb - 1, 0), 0)),
                pl.BlockSpec((XHALO, d), lambda i, *m: (jnp.minimum((i + 1) * hb, n_halo - 1), 0)),
                pl.BlockSpec((1, d), const), pl.BlockSpec((1, 1, d), cond), pl.BlockSpec((1, 1, d), cond),
                pl.BlockSpec(w_qkv.shape, const), pl.BlockSpec(w_z.shape, const), pl.BlockSpec(w_ab.shape, const),
                pl.BlockSpec(conv_w.shape, const), pl.BlockSpec((1, LANES), const), pl.BlockSpec((1, LANES), const)]
    out_specs = [pl.BlockSpec((TOK, w), row), pl.BlockSpec((TOK, w_z.shape[1]), row), pl.BlockSpec((TOK, LANES), row)]
    out_shape = [jax.ShapeDtypeStruct((t, w), BF16), jax.ShapeDtypeStruct((t, w_z.shape[1]), BF16),
                 jax.ShapeDtypeStruct((t, LANES), F32)]
    return pl.pallas_call(
        _odd_in_kernel,
        grid_spec=pltpu.PrefetchScalarGridSpec(num_scalar_prefetch=3, grid=(t // TOK,),
                                               in_specs=in_specs, out_specs=out_specs),
        out_shape=out_shape,
        compiler_params=_cparams(1),
        name="odd_in",
    )(cr, le, re, x, x, x, g.reshape(1, d), sc, sh, w_qkv, w_z, w_ab, conv_w, alog_row, dtb_row)


_GDN_LEVELS = tuple(2 ** p for p in range(int(math.log2(GDN_CHUNK))))


def _gdn_masks(msk_ref, tri_ref):
    c = GDN_CHUNK
    ii = lax.broadcasted_iota(I32, (c, c), 0)
    jj = lax.broadcasted_iota(I32, (c, c), 1)
    one = jnp.ones((c, c), F32)
    zero = jnp.zeros((c, c), F32)
    for rev in (0, 1):
        a, b = (ii, jj) if rev == 0 else (jj, ii)
        base = rev * 9
        msk_ref[base + 0] = jnp.where(a >= b, one, zero)
        msk_ref[base + 1] = jnp.where(a > b, one, zero)
        for l, m in enumerate(_GDN_LEVELS):
            sh = int(math.log2(m))
            ab_, bb_ = a >> sh, b >> sh
            hit = ((ab_ & 1) == 1) & (bb_ == ab_ - 1)
            msk_ref[base + 2 + l] = jnp.where(hit, one, zero)
        tri_ref[rev] = jnp.where(a >= b, one, zero).astype(BF16)


def _gdn_chunk_step(dirs, st_ref, msk_ref, tri_ref):
    c = GDN_CHUNK
    nh = GDN_HEADS
    probs = [(rev, h) for rev in range(2) for h in range(nh)]
    ones = jnp.ones((c, c), BF16)
    gcum, gcum_t, e_all, kdec_all, elast_all, gbs = [], [], [], [], [], []
    for rev in range(2):
        gb = dirs[rev][3][...]
        g1, g2, g3 = _split3(gb)
        tri = tri_ref[rev]
        gc = _dot(tri, g1) + _dot(tri, g2) + _dot(tri, g3)
        gl = _dot(ones, g1) + _dot(ones, g2) + _dot(ones, g3)
        gbs.append(gb)
        gcum.append(gc)
        gcum_t.append(gc.T)
        e_all.append(jnp.exp(gc))
        kdec_all.append(jnp.exp(gl - gc))
        elast_all.append(jnp.exp(gl))

    def col(rev, h):
        return rev * nh + h

    def hsl(h):
        return slice(h * GDN_DK, (h + 1) * GDN_DK)

    st = [st_ref[rev, h] for rev, h in probs]

    a, attn, kbs = [], [], []
    for rev, h in probs:
        base = rev * 9
        p = col(rev, h)
        q_ref, k_ref = dirs[rev][0], dirs[rev][1]
        k = k_ref[:, hsl(h)]
        gcb = jnp.broadcast_to(gcum[rev][:, p:p + 1], (c, c))
        grb = jnp.broadcast_to(gcum_t[rev][p:p + 1, :], (c, c))
        decay = jnp.exp(jnp.where(msk_ref[base] > 0, gcb - grb, NEG_BIG))
        kb = k.astype(F32) * gbs[rev][:, 2 * nh + p:2 * nh + p + 1]
        kbs.append(kb)
        a.append(_dot_nt(kb.astype(BF16), k) * decay * msk_ref[base + 1])
        attn.append((_dot_nt(q_ref[:, hsl(h)], k) * decay).astype(BF16))

    tinv = []
    for i, (rev, h) in enumerate(probs):
        base = rev * 9
        tinv.append(msk_ref[base] - msk_ref[base + 1] - a[i] * msk_ref[base + 2])
    for l in range(1, len(_GDN_LEVELS)):
        bt = []
        for i, (rev, h) in enumerate(probs):
            bt.append(_dot((a[i] * msk_ref[rev * 9 + 2 + l]).astype(BF16), tinv[i].astype(BF16)))
        for i in range(len(probs)):
            tinv[i] = tinv[i] - _dot(tinv[i].astype(BF16), bt[i].astype(BF16))

    sol = []
    for i, (rev, h) in enumerate(probs):
        p = col(rev, h)
        v = dirs[rev][2][:, hsl(h)].astype(F32)
        beta = gbs[rev][:, 2 * nh + p:2 * nh + p + 1]
        rhs = jnp.concatenate([v * beta, kbs[i] * e_all[rev][:, p:p + 1]], axis=1)
        sol.append(_dot(tinv[i].astype(BF16), rhs.astype(BF16)))
    v_new = []
    for i in range(len(probs)):
        u, w = sol[i][:, :GDN_DK], sol[i][:, GDN_DK:]
        v_new.append((u - _dot(w.astype(BF16), st[i].astype(BF16))).astype(BF16))
    for i, (rev, h) in enumerate(probs):
        p = col(rev, h)
        q = dirs[rev][0][:, hsl(h)].astype(F32)
        qs = (q * e_all[rev][:, p:p + 1]).astype(BF16)
        dirs[rev][4][:, hsl(h)] = (_dot(qs, st[i].astype(BF16)) + _dot(attn[i], v_new[i])).astype(BF16)
    new_st = []
    for i, (rev, h) in enumerate(probs):
        p = col(rev, h)
        kd = (dirs[rev][1][:, hsl(h)].astype(F32) * kdec_all[rev][:, p:p + 1]).astype(BF16)
        new_st.append(st[i] * elast_all[rev][:, p:p + 1] + _dot_tn(kd, v_new[i]))
    for i, (rev, h) in enumerate(probs):
        st_ref[rev, h] = new_st[i]


def _gdn_kernel(fb_ref, bb_ref, sq_ref, fi_ref, la_ref,
                qf_ref, kf_ref, vf_ref, gf_ref, qb_ref, kb_ref, vb_ref, gbk_ref, s0_ref,
                of_ref, ob_ref, sout_ref,
                st_ref, msk_ref, tri_ref, *, n_ctx):
    g = pl.program_id(0)

    @pl.when(g == 0)
    def _():
        _gdn_masks(msk_ref, tri_ref)

    @pl.when((fi_ref[g] == 1) & (sq_ref[g] < n_ctx))
    def _():
        st_ref[...] = jnp.zeros_like(st_ref)

    @pl.when((fi_ref[g] == 1) & (sq_ref[g] >= n_ctx))
    def _():
        st_ref[...] = s0_ref[0]

    _gdn_chunk_step(((qf_ref, kf_ref, vf_ref, gf_ref, of_ref), (qb_ref, kb_ref, vb_ref, gbk_ref, ob_ref)),
                    st_ref, msk_ref, tri_ref)

    @pl.when((la_ref[g] == 1) & (sq_ref[g] < n_ctx))
    def _():
        sout_ref[0] = st_ref[...]


def _gdn(lay, qkv, gb, s0):
    t = qkv.shape[0]
    c = GDN_CHUNK
    sched = lay.scan_schedule(c)[:5]

    def tok(which, col):
        return pl.BlockSpec((c, GDN_KW), lambda g, *m: (m[which][g], col))

    def gate(which):
        return pl.BlockSpec((c, LANES), lambda g, *m: (m[which][g], 0))

    n_ctx = lay.bc
    sblk = (1, 2, GDN_HEADS, GDN_DK, GDN_DK)
    state_in = pl.BlockSpec(sblk, lambda g, *m: (jnp.maximum(m[2][g] - n_ctx, 0), 0, 0, 0, 0))
    state_out = pl.BlockSpec(sblk, lambda g, *m: (jnp.minimum(m[2][g], n_ctx - 1), 0, 0, 0, 0))
    in_specs = [tok(0, 0), tok(0, 1), tok(0, 2), gate(0), tok(1, 0), tok(1, 1), tok(1, 2), gate(1), state_in]
    out_specs = [pl.BlockSpec((c, GDN_VW), lambda g, *m: (m[0][g], 0)),
                 pl.BlockSpec((c, GDN_VW), lambda g, *m: (m[1][g], 0)), state_out]
    out_shape = [jax.ShapeDtypeStruct((t, GDN_VW), BF16), jax.ShapeDtypeStruct((t, GDN_VW), BF16),
                 jax.ShapeDtypeStruct((n_ctx,) + s0.shape[1:], F32)]
    scratch = [pltpu.VMEM((2, GDN_HEADS, GDN_DK, GDN_DK), F32),
               pltpu.VMEM((18, c, c), F32), pltpu.VMEM((2, c, c), BF16)]
    return pl.pallas_call(
        functools.partial(_gdn_kernel, n_ctx=n_ctx),
        grid_spec=pltpu.PrefetchScalarGridSpec(num_scalar_prefetch=5, grid=(len(sched[0]),),
                                               in_specs=in_specs, out_specs=out_specs,
                                               scratch_shapes=scratch),
        out_shape=out_shape,
        compiler_params=_cparams(1),
        name="gdn_scan",
    )(*sched, qkv, qkv, qkv, gb, qkv, qkv, qkv, gb, s0)


def _odd_post_kernel(cr_ref, of_ref, ob_ref, z_ref, x_ref, nw_ref, wout_ref, npost_ref, g1_ref, out_ref):
    o = of_ref[...].astype(F32) + ob_ref[...].astype(F32)
    z = z_ref[...].astype(F32)
    nw = nw_ref[...]
    parts = []
    for h in range(GDN_HEADS):
        sl = slice(h * GDN_DK, (h + 1) * GDN_DK)
        parts.append(_rms(o[:, sl], nw) * _silu(z[:, sl]))
    y = jnp.concatenate(parts, axis=1)
    out = _dot(y.astype(BF16), wout_ref[...])
    out_ref[...] = x_ref[...] + g1_ref[0] * _rms(out, npost_ref[...])


def _odd_post(lay, o_f, o_b, proj, x, norm_w, w_out, n_post, g1):
    t, d = x.shape
    cr = lay.cond_rows(TOK)
    row = lambda i, *m: (i, 0)
    const = lambda i, *m: (0, 0)
    in_specs = [pl.BlockSpec((TOK, GDN_VW), row), pl.BlockSpec((TOK, GDN_VW), row),
                pl.BlockSpec((TOK, GDN_VW), row),
                pl.BlockSpec((TOK, d), row),
                pl.BlockSpec((1, GDN_DK), const), pl.BlockSpec(w_out.shape, const), pl.BlockSpec((1, d), const),
                pl.BlockSpec((1, 1, d), lambda i, *m: (m[0][i], 0, 0))]
    return pl.pallas_call(
        _odd_post_kernel,
        grid_spec=pltpu.PrefetchScalarGridSpec(num_scalar_prefetch=1, grid=(t // TOK,),
                                               in_specs=in_specs, out_specs=pl.BlockSpec((TOK, d), row)),
        out_shape=jax.ShapeDtypeStruct((t, d), F32),
        compiler_params=_cparams(1),
        name="odd_post",
    )(cr, o_f, o_b, proj, x, norm_w.reshape(1, -1), w_out, n_post.reshape(1, d), g1)


U32 = jnp.uint32
PSUB = D_MODEL // (2 * LANES)
HI_MASK = 0xFFFF0000
N_PAIRS = EXP_PER_GROUP * (EXP_PER_GROUP - 1) // 2
N_CLASSES = N_GROUPS * N_PAIRS
CLS_PAD = 32
PAIR_LO = (0, 0, 0, 1, 1, 2)
PAIR_HI = (1, 2, 3, 2, 3, 3)


def _pack_pairs(x):
    half = x.shape[1] // 2
    lo = lax.bitcast_convert_type(x[:, :half].astype(BF16).astype(F32), U32) >> 16
    hi = lax.bitcast_convert_type(x[:, half:].astype(BF16).astype(F32), U32) & jnp.uint32(HI_MASK)
    return lo | hi


def _unpack_pairs(w):
    lo = lax.bitcast_convert_type(w << 16, F32)
    hi = lax.bitcast_convert_type(w & jnp.uint32(HI_MASK), F32)
    return jnp.concatenate([lo, hi], axis=1)


def _store_token_major(ref, w):
    n = w.shape[0]
    for s in range(PSUB):
        ref[pl.ds(s, n, stride=PSUB), :] = w[:, s * LANES:(s + 1) * LANES]


def _load_token_major(ref, n):
    return jnp.concatenate([ref[pl.ds(s, n, stride=PSUB), :] for s in range(PSUB)], axis=1)


def _router_kernel(cr_ref, x_ref, g_ref, sc_ref, sh_ref, rwt_ref, rb_ref,
                   h_ref, route_ref, wts_ref, cnt_ref,
                   base_ref, su_ref):
    i = pl.program_id(0)
    tm = TM_ROUTE

    @pl.when(i == 0)
    def _():
        base_ref[...] = jnp.zeros_like(base_ref)
        ii = lax.broadcasted_iota(I32, (tm, tm), 0)
        jj = lax.broadcasted_iota(I32, (tm, tm), 1)
        su_ref[...] = jnp.where(ii < jj, 1.0, 0.0).astype(BF16)

    h = _rms(x_ref[...], g_ref[...]) * (1.0 + sc_ref[0]) + sh_ref[0]
    hb = h.astype(BF16)
    _store_token_major(h_ref, _pack_pairs(h))
    h1 = hb
    w1, w2, w3 = _split3(rwt_ref[...])
    logits = _dot_nt(w1, h1) + _dot_nt(w2, h1) + _dot_nt(w3, h1)
    score = jax.nn.sigmoid(logits)
    sel = score + rb_ref[...]

    def row(a, e):
        return a[e:e + 1, :]

    gsum = []
    for gi in range(N_GROUPS):
        a, b, c, d = (row(sel, gi * EXP_PER_GROUP + j) for j in range(EXP_PER_GROUP))
        hi1, lo1 = jnp.maximum(a, b), jnp.minimum(a, b)
        hi2, lo2 = jnp.maximum(c, d), jnp.minimum(c, d)
        gsum.append(jnp.maximum(hi1, hi2) + jnp.maximum(jnp.minimum(hi1, hi2), jnp.maximum(lo1, lo2)))
    best = jnp.zeros_like(gsum[0]).astype(I32)
    cur = gsum[0]
    for gi in range(1, N_GROUPS):
        upd = gsum[gi] > cur
        best = jnp.where(upd, gi, best)
        cur = jnp.where(upd, gsum[gi], cur)

    def pick(arr, j):
        out = row(arr, j)
        for gi in range(1, N_GROUPS):
            out = jnp.where(best == gi, row(arr, gi * EXP_PER_GROUP + j), out)
        return out

    vals = [pick(sel, j) for j in range(EXP_PER_GROUP)]
    scs = [pick(score, j) for j in range(EXP_PER_GROUP)]

    def argmax_first(vs):
        idx = jnp.zeros_like(best)
        m = vs[0]
        for j in range(1, EXP_PER_GROUP):
            upd = vs[j] > m
            idx = jnp.where(upd, j, idx)
            m = jnp.where(upd, vs[j], m)
        return idx

    i1 = argmax_first(vals)
    vals2 = [jnp.where(i1 == j, -jnp.inf, vals[j]) for j in range(EXP_PER_GROUP)]
    i2 = argmax_first(vals2)

    def take(vs, idx):
        out = vs[0]
        for j in range(1, EXP_PER_GROUP):
            out = jnp.where(idx == j, vs[j], out)
        return out

    s1, s2 = take(scs, i1), take(scs, i2)
    tot = s1 + s2
    first_low = i1 < i2
    lo = jnp.minimum(i1, i2)
    hi = jnp.maximum(i1, i2)
    pair = jnp.where(lo == 0, hi - 1, jnp.where(lo == 1, hi + 1, N_PAIRS - 1))
    cls = best * N_PAIRS + pair
    wts_ref[0:1, :] = jnp.where(first_low, s1, s2) / tot
    wts_ref[1:2, :] = jnp.where(first_low, s2, s1) / tot

    crow = lax.broadcasted_iota(I32, (CLS_PAD, tm), 0)
    onehot = jnp.where(crow == cls, 1.0, 0.0)
    before = _dot(onehot.astype(BF16), su_ref[...]) + base_ref[...]
    route_ref[0:1, :] = cls
    route_ref[1:2, :] = jnp.sum(onehot * before, axis=0, keepdims=True).astype(I32)
    base_ref[...] = base_ref[...] + jnp.sum(onehot, axis=1, keepdims=True)
    cnt_ref[...] = jnp.broadcast_to(base_ref[...], cnt_ref.shape).astype(I32)


def _router(lay, x, g, sc, sh, rwt, rbias):
    t, d = x.shape
    cr = lay.cond_rows(TM_ROUTE)
    tm = TM_ROUTE
    row = lambda i, cr: (i, 0)
    col = lambda i, cr: (0, i)
    const = lambda i, cr: (0, 0)
    cond = lambda i, cr: (cr[i], 0, 0)
    in_specs = [pl.BlockSpec((tm, d), row), pl.BlockSpec((1, d), const),
                pl.BlockSpec((1, 1, d), cond), pl.BlockSpec((1, 1, d), cond),
                pl.BlockSpec((N_EXPERTS, d), const), pl.BlockSpec((N_EXPERTS, 1), const)]
    out_specs = [pl.BlockSpec((tm * PSUB, LANES), row), pl.BlockSpec((2, tm), col), pl.BlockSpec((2, tm), col),
                 pl.BlockSpec((CLS_PAD, LANES), const)]
    out_shape = [jax.ShapeDtypeStruct((t * PSUB, LANES), U32), jax.ShapeDtypeStruct((2, t), I32),
                 jax.ShapeDtypeStruct((2, t), F32), jax.ShapeDtypeStruct((CLS_PAD, LANES), I32)]
    return pl.pallas_call(
        _router_kernel,
        grid_spec=pltpu.PrefetchScalarGridSpec(num_scalar_prefetch=1, grid=(t // tm,),
                                               in_specs=in_specs, out_specs=out_specs,
                                               scratch_shapes=[pltpu.VMEM((CLS_PAD, 1), F32),
                                                               pltpu.VMEM((tm, tm), BF16)]),
        out_shape=out_shape,
        compiler_params=_cparams(1),
        name="router",
    )(cr, x, g.reshape(1, d), sc, sh, rwt, rbias.reshape(N_EXPERTS, 1))


def _row_copy(src, dst, sem):
    return pltpu.make_async_copy(src, dst, sem)


def _dispatch_kernel(ends_ref, padded_ref, pos_ref, h_ref, sorted_ref, zero_ref, sem, zsem, *, n_tiles):
    n = TM_DISP

    @pl.when(pl.program_id(0) == 0)
    def _():
        zero_ref[...] = jnp.zeros_like(zero_ref)

        def fill_tile(start_row):
            start = pl.multiple_of(start_row * PSUB, PSUB)
            fill = _row_copy(zero_ref, sorted_ref.at[pl.ds(start, TM_EXP * PSUB)], zsem)
            fill.start()
            fill.wait()

        for c in range(N_CLASSES):
            @pl.when(padded_ref[c] > 0)
            def _():
                fill_tile(ends_ref[c] - TM_EXP)
        for j in range(n_tiles - N_CLASSES, n_tiles):
            @pl.when(j * TM_EXP >= ends_ref[N_CLASSES - 1])
            def _():
                fill_tile(j * TM_EXP)

    def issue(r, carry):
        for k in range(2):
            tok = 2 * r + k
            src = h_ref.at[pl.ds(pl.multiple_of(tok * PSUB, PSUB), PSUB)]
            dst = pl.multiple_of(pos_ref[0, 0, tok] * PSUB, PSUB)
            _row_copy(src, sorted_ref.at[pl.ds(dst, PSUB)], sem).start(priority=k)
        return carry

    lax.fori_loop(0, n // 2, issue, 0, unroll=8)
    _row_copy(h_ref, sorted_ref.at[pl.ds(0, n * PSUB)], sem).wait()


def _dispatch(h, pos3, ends, padded, n_rows):
    t = h.shape[0] // PSUB
    return pl.pallas_call(
        functools.partial(_dispatch_kernel, n_tiles=n_rows // TM_EXP),
        grid_spec=pltpu.PrefetchScalarGridSpec(
            num_scalar_prefetch=2, grid=(t // TM_DISP,),
            in_specs=[pl.BlockSpec((1, 1, TM_DISP), lambda i, *m: (i, 0, 0), memory_space=pltpu.SMEM),
                      pl.BlockSpec((TM_DISP * PSUB, LANES), lambda i, *m: (i, 0))],
            out_specs=pl.BlockSpec(memory_space=pl.ANY),
            scratch_shapes=[pltpu.VMEM((TM_EXP * PSUB, LANES), h.dtype),
                            pltpu.SemaphoreType.DMA(()), pltpu.SemaphoreType.DMA(())]),
        out_shape=jax.ShapeDtypeStruct((n_rows * PSUB, LANES), h.dtype),
        compiler_params=_cparams(1),
        name="dispatch",
    )(ends, padded, pos3, h)


def _expert_kernel(ea_ref, eb_ref, tv_ref, h_ref, *refs):
    i = pl.program_id(0)
    w_refs, y_ref, wb_refs = refs[:6], refs[6], refs[7:]
    prev = jnp.maximum(i - 1, 0)

    for slot, e_ref in enumerate((ea_ref, eb_ref)):
        @pl.when((i == 0) | (e_ref[i] != e_ref[prev]))
        def _():
            for j in range(3):
                wb_refs[3 * slot + j][...] = w_refs[3 * slot + j][0, 0].astype(BF16)

    @pl.when(tv_ref[i] == 1)
    def _():
        hb = _unpack_pairs(_load_token_major(h_ref, TM_EXP)).astype(BF16)
        for slot in range(2):
            wg, wu, wd = (wb_refs[3 * slot + j][...] for j in range(3))
            he = (_silu(_dot(hb, wg)) * _dot(hb, wu)).astype(BF16)
            y = _pack_pairs(_dot(he, wd))
            for s in range(PSUB):
                y_ref[pl.ds(slot * PSUB + s, TM_EXP, stride=2 * PSUB), :] = y[:, s * LANES:(s + 1) * LANES]

    @pl.when(tv_ref[i] == 0)
    def _():
        y_ref[...] = jnp.zeros_like(y_ref)


def _experts(hs, tile_ea, tile_eb, tile_valid, wg, wu, wd, layer):
    d = wg.shape[2]
    r = hs.shape[0] // PSUB
    sel_a = lambda i, ea, eb, tv: (layer, ea[i], 0, 0)
    sel_b = lambda i, ea, eb, tv: (layer, eb[i], 0, 0)
    w_specs = []
    for sel in (sel_a, sel_b):
        w_specs += [pl.BlockSpec((1, 1, d, D_EXPERT), sel), pl.BlockSpec((1, 1, d, D_EXPERT), sel),
                    pl.BlockSpec((1, 1, D_EXPERT, d), sel)]
    wb = [pltpu.VMEM((d, D_EXPERT), BF16), pltpu.VMEM((d, D_EXPERT), BF16), pltpu.VMEM((D_EXPERT, d), BF16)]
    return pl.pallas_call(
        _expert_kernel,
        grid_spec=pltpu.PrefetchScalarGridSpec(
            num_scalar_prefetch=3, grid=(r // TM_EXP,),
            in_specs=[pl.BlockSpec((TM_EXP * PSUB, LANES), lambda i, ea, eb, tv: (i * tv[i], 0))] + w_specs,
            out_specs=pl.BlockSpec((TM_EXP * 2 * PSUB, LANES), lambda i, ea, eb, tv: (i, 0)),
            scratch_shapes=wb + wb),
        out_shape=jax.ShapeDtypeStruct((r * 2 * PSUB, LANES), U32),
        compiler_params=_cparams(1),
        name="experts",
    )(tile_ea, tile_eb, tile_valid, hs, wg, wu, wd, wg, wu, wd)


def _combine_kernel(cr_ref, pos_ref, posn_ref, ys_ref, wts_ref, x_ref, npost_ref, g2_ref, *rest, n_first, n_steps):
    n = TOK
    buf_ref, sem = rest[-2:]
    outs = rest[:-2]
    i = pl.program_id(0)
    wide = 2 * PSUB

    def gather(p_ref, slot):
        def issue(r, carry):
            for k in range(2):
                tok = 2 * r + k
                src = pl.multiple_of(p_ref[0, 0, tok] * wide, wide)
                dst = pl.multiple_of(tok * wide, wide)
                _row_copy(ys_ref.at[pl.ds(src, wide)], buf_ref.at[slot, pl.ds(dst, wide)],
                          sem.at[slot]).start(priority=k)
            return carry

        lax.fori_loop(0, n // 2, issue, 0, unroll=8)

    slot = lax.rem(i, 2)

    @pl.when(i == 0)
    def _():
        gather(pos_ref, 0)

    @pl.when(i + 1 < n_steps)
    def _():
        gather(posn_ref, 1 - slot)

    _row_copy(ys_ref.at[pl.ds(0, n * wide)], buf_ref.at[slot], sem.at[slot]).wait()

    w = wts_ref[...]
    cur = buf_ref.at[slot]
    y_lo = _unpack_pairs(jnp.concatenate([cur[pl.ds(s, n, stride=wide), :] for s in range(PSUB)], axis=1))
    y_hi = _unpack_pairs(jnp.concatenate([cur[pl.ds(PSUB + s, n, stride=wide), :] for s in range(PSUB)], axis=1))
    y = y_lo * w[:, 0:1] + y_hi * w[:, 1:2]
    res = x_ref[...] + g2_ref[0] * _rms(y, npost_ref[...])
    if n_first is None:
        outs[0][...] = res
    else:
        @pl.when(i < n_first)
        def _():
            outs[0][...] = res

        @pl.when(i >= n_first)
        def _():
            outs[1][...] = res


def _combine(lay, ys3, pos3, wts, x, n_post, g2, split):
    t, d = x.shape
    cr = lay.cond_rows(TOK)
    row = lambda i, cr: (i, 0)
    const = lambda i, cr: (0, 0)
    n_steps = t // TOK
    in_specs = [pl.BlockSpec((1, 1, TOK), lambda i, cr: (i, 0, 0), memory_space=pltpu.SMEM),
                pl.BlockSpec((1, 1, TOK), lambda i, cr: (jnp.minimum(i + 1, n_steps - 1), 0, 0),
                             memory_space=pltpu.SMEM),
                pl.BlockSpec(memory_space=pl.ANY),
                pl.BlockSpec((TOK, 2), row),
                pl.BlockSpec((TOK, d), row),
                pl.BlockSpec((1, d), const),
                pl.BlockSpec((1, 1, d), lambda i, cr: (cr[i], 0, 0))]
    if split:
        n_first = lay.t_ctx // TOK
        out_specs = [pl.BlockSpec((TOK, d), lambda i, cr: (jnp.minimum(i, n_first - 1), 0)),
                     pl.BlockSpec((TOK, d), lambda i, cr: (jnp.maximum(i - n_first, 0), 0))]
        out_shape = [jax.ShapeDtypeStruct((lay.t_ctx, d), F32), jax.ShapeDtypeStruct((t - lay.t_ctx, d), F32)]
    else:
        n_first = None
        out_specs = [pl.BlockSpec((TOK, d), row)]
        out_shape = [jax.ShapeDtypeStruct((t, d), F32)]
    return pl.pallas_call(
        functools.partial(_combine_kernel, n_first=n_first, n_steps=n_steps),
        grid_spec=pltpu.PrefetchScalarGridSpec(
            num_scalar_prefetch=1, grid=(n_steps,), in_specs=in_specs, out_specs=out_specs,
            scratch_shapes=[pltpu.VMEM((2, TOK * 2 * PSUB, LANES), U32), pltpu.SemaphoreType.DMA((2,))]),
        out_shape=out_shape,
        compiler_params=_cparams(1),
        name="combine",
    )(cr, pos3, pos3, ys3, wts, x, n_post.reshape(1, d), g2)


def _moe(lay, x, g_pre, sc, sh, g2, n_post, rwt, rbias, wg, wu, wd, layer, split):
    t, d = x.shape
    h, route, wts, cnt = _router(lay, x, g_pre, sc, sh, rwt, rbias)
    cls, rank = route[0], route[1]
    counts = cnt[:N_CLASSES, 0]
    padded = ((counts + TM_EXP - 1) // TM_EXP) * TM_EXP
    ends = jnp.cumsum(padded).astype(I32)
    offs = ends - padded
    cids = jnp.arange(N_CLASSES, dtype=I32)[:, None]
    pos = jnp.sum(jnp.where(cls[None] == cids, offs[:, None], 0), axis=0) + rank
    pos3 = pos.reshape(t // TOK, 1, TOK).astype(I32)
    n_rows = t + N_CLASSES * TM_EXP
    n_tiles = n_rows // TM_EXP
    starts = jnp.arange(n_tiles, dtype=I32) * TM_EXP
    tile_cls = jnp.minimum(jnp.sum((starts[:, None] >= ends[None, :]).astype(I32), axis=1), N_CLASSES - 1)
    tile_valid = (starts < ends[-1]).astype(I32)
    grp, pair = tile_cls // N_PAIRS, tile_cls % N_PAIRS
    pids = jnp.arange(N_PAIRS, dtype=I32)[None, :]
    lo = jnp.sum(jnp.where(pair[:, None] == pids, jnp.asarray(PAIR_LO, I32)[None, :], 0), axis=1)
    hi = jnp.sum(jnp.where(pair[:, None] == pids, jnp.asarray(PAIR_HI, I32)[None, :], 0), axis=1)
    tile_ea = (grp * EXP_PER_GROUP + lo).astype(I32)
    tile_eb = (grp * EXP_PER_GROUP + hi).astype(I32)
    hs = _dispatch(h, pos.reshape(t // TM_DISP, 1, TM_DISP).astype(I32), ends, padded.astype(I32), n_rows)
    ys = _experts(hs, tile_ea, tile_eb, tile_valid, wg, wu, wd, layer)
    return _combine(lay, ys, pos3, wts.T, x, n_post, g2, split)


def _rope_tables(ld):
    rows = ld // GRID_W
    r = jnp.repeat(jnp.arange(rows, dtype=F32), GRID_W)
    col = jnp.tile(jnp.arange(GRID_W, dtype=F32), rows)
    quarter = RET_DK // 4
    inv = ROPE_BASE ** (-jnp.arange(quarter, dtype=F32) / quarter)
    ang = jnp.concatenate([r[:, None] * inv, col[:, None] * inv], axis=-1)
    cos, sin = jnp.cos(ang), jnp.sin(ang)
    c = jnp.concatenate([cos, cos], axis=-1)
    s = jnp.concatenate([-sin, sin], axis=-1)
    ident_c = jnp.ones((TOK, RET_DK), F32)
    ident_s = jnp.zeros((TOK, RET_DK), F32)
    return jnp.concatenate([c, ident_c], axis=0), jnp.concatenate([s, ident_s], axis=0)


def kernel(x_prompt, x_sample, state_ret, state_gdn, c, c_ctx, w_mod, b_mod, norm_mix_pre, norm_mix_post,
           norm_ffn_pre, norm_ffn_post, ev_w_in, ev_conv_w, ev_conv_ln_g, ev_conv_ln_b, ev_ret_decay, ev_w_out,
           od_w_in, od_conv_w, od_a_log, od_dt_bias, od_norm_w, od_w_out, router_w, router_bias,
           moe_w_gate, moe_w_up, moe_w_down):
    bc, lc, d = x_prompt.shape
    bd, ld, _ = x_sample.shape
    depth = w_mod.shape[0]
    lay = _Layout(bc, lc, bd, ld)
    t = lay.t

    x = (x_prompt.reshape(bc * lc, d), x_sample.reshape(bd * ld, d))
    cond = jnp.zeros((COND_PAD, d), F32).at[0].set(c_ctx).at[1:1 + bd].set(c)
    mod = _modulation(cond, w_mod, b_mod)
    mod = mod.reshape(depth, COND_PAD, N_MOD, 1, d).transpose(0, 2, 1, 3, 4)

    rope_c, rope_s = _rope_tables(ld)
    rwt = router_w.T
    ret_states, gdn_states = [], []
    for layer in range(depth):
        sh1, sc1, g1, sh2, sc2, g2 = (mod[layer, j] for j in range(N_MOD))
        i = layer // 2
        if layer % 2 == 0:
            (proj,) = _inproj(lay, x, norm_mix_pre[layer], sc1, sh1, [ev_w_in[i].astype(BF16)], [BF16])
            log_gamma = -jnp.exp(ev_ret_decay[i].astype(F32))
            o_f, o_b, s_out = _retention(lay, proj, log_gamma, rope_c, rope_s, state_ret[:, i])
            ret_states.append(s_out)
            cw = jnp.zeros((32, CONV_CH), F32).at[:CONV_W].set(ev_conv_w[i])
            x = _even_post(lay, proj, o_f, o_b, x, cw, ev_conv_ln_g[i], ev_conv_ln_b[i],
                           ev_w_out[i].astype(BF16), norm_mix_post[layer], g1)
        else:
            w_in = od_w_in[i]
            n_qkv = 2 * GDN_KW + GDN_VW
            n_main = n_qkv + GDN_VW
            w_ab = jnp.zeros((d, LANES), F32).at[:, :4 * GDN_HEADS].set(w_in[:, n_main:])
            cw = jnp.zeros((8, n_qkv), F32).at[:SHORT_W].set(od_conv_w[i])
            alog_row = jnp.zeros((1, LANES), F32).at[0, :2 * GDN_HEADS].set(od_a_log[i].reshape(-1))
            dtb_row = jnp.zeros((1, LANES), F32).at[0, :2 * GDN_HEADS].set(od_dt_bias[i].reshape(-1))
            qkv, z, gb = _odd_in(lay, x, norm_mix_pre[layer], sc1, sh1, w_in[:, :n_qkv].astype(BF16),
                                 w_in[:, n_qkv:n_main].astype(BF16), w_ab.astype(BF16), cw, alog_row, dtb_row)
            o_f, o_b, s_out = _gdn(lay, qkv, gb, state_gdn[:, i])
            gdn_states.append(s_out)
            x = _odd_post(lay, o_f, o_b, z, x, od_norm_w[i], od_w_out[i].astype(BF16),
                          norm_mix_post[layer], g1)
        outs = _moe(lay, x, norm_ffn_pre[layer], sc2, sh2, g2, norm_ffn_post[layer], rwt, router_bias,
                    moe_w_gate, moe_w_up, moe_w_down, layer, split=(layer == depth - 1))
        x = outs[0]

    y_prompt = outs[0].reshape(bc, lc, d)
    y_sample = outs[1].reshape(bd, ld, d)
    new_ret = jnp.stack(ret_states, axis=1)
    new_gdn = jnp.stack(gdn_states, axis=1)
    return y_prompt, y_sample, new_ret.astype(x_prompt.dtype), new_gdn.astype(x_prompt.dtype)
```

```python
import functools
import math

import jax
import jax.numpy as jnp
import numpy as np
from jax import lax
from jax.experimental import pallas as pl
from jax.experimental.pallas import tpu as pltpu

F32 = jnp.float32
BF16 = jnp.bfloat16
I32 = jnp.int32

D_MODEL = 1024
N_MOD = 6
EPS = 1e-6
GRID_W = 64
CONV_CH = 512
CONV_W = 31
RET_HEADS = 4
RET_DK = 128
RET_W = 512
ROPE_BASE = 10000.0
EVEN_IN = 2 * CONV_CH + 4 * RET_W
GDN_HEADS = 8
GDN_DK = 128
GDN_KW = 1024
GDN_VW = 1024
SHORT_W = 5
N_EXPERTS = 16
N_GROUPS = 4
EXP_PER_GROUP = 4
D_EXPERT = 512

LANES = 128
TOK = 256
TM_PROJ = 512
TM_ROUTE = 512
TM_EXP = 512
TM_DISP = 512
GDN_CHUNK = 128
HALO = 16
COND_PAD = 16
VMEM_LIMIT = 56 * 1024 * 1024
NEG_BIG = -1e30


def _cparams(n_axes=1, vmem=VMEM_LIMIT):
    return pltpu.CompilerParams(dimension_semantics=("arbitrary",) * n_axes, vmem_limit_bytes=vmem)


def _silu(x):
    return x * jax.nn.sigmoid(x)


def _rms(x, g):
    return x * lax.rsqrt(jnp.mean(x * x, axis=-1, keepdims=True) + EPS) * g


def _dot(a, b):
    return jnp.dot(a, b, preferred_element_type=F32)


def _dot_nt(a, b):
    return lax.dot_general(a, b, (((1,), (1,)), ((), ())), preferred_element_type=F32)


def _dot_tn(a, b):
    return lax.dot_general(a, b, (((0,), (0,)), ((), ())), preferred_element_type=F32)


def _split3(x):
    x1 = x.astype(BF16)
    r = x - x1.astype(F32)
    x2 = r.astype(BF16)
    x3 = (r - x2.astype(F32)).astype(BF16)
    return x1, x2, x3


class _Layout:
    def __init__(self, bc, lc, bd, ld):
        self.bc, self.lc, self.bd, self.ld = bc, lc, bd, ld
        self.t_ctx = bc * lc
        self.t = bc * lc + bd * ld
        self.n_seq = bc + bd
        assert lc % TOK == 0 and ld % TOK == 0 and self.t_ctx % TM_PROJ == 0 and ld % TM_PROJ == 0
        assert self.t % TM_ROUTE == 0 and lc % GDN_CHUNK == 0 and ld % GDN_CHUNK == 0

    def seq_of_row(self, r):
        if r < self.t_ctx:
            return r // self.lc, r % self.lc, self.lc
        r2 = r - self.t_ctx
        return self.bc + r2 // self.ld, r2 % self.ld, self.ld

    def cond_rows(self, tile):
        out = []
        for i in range(self.t // tile):
            s, _, _ = self.seq_of_row(i * tile)
            out.append(0 if s < self.bc else 1 + s - self.bc)
        return np.asarray(out, np.int32)

    def edges(self, tile):
        left, right = [], []
        for i in range(self.t // tile):
            _, p, l = self.seq_of_row(i * tile)
            left.append(int(p == 0))
            right.append(int(p + tile == l))
        return np.asarray(left, np.int32), np.asarray(right, np.int32)

    def scan_schedule(self, chunk):
        fb, bb, sq, fi, la, rf, rb = [], [], [], [], [], [], []
        ident = self.ld // chunk
        for s in range(self.n_seq):
            if s < self.bc:
                base, n = s * self.lc // chunk, self.lc // chunk
            else:
                base, n = (self.t_ctx + (s - self.bc) * self.ld) // chunk, self.ld // chunk
            for c in range(n):
                fb.append(base + c)
                bb.append(base + n - 1 - c)
                sq.append(s)
                fi.append(int(c == 0))
                la.append(int(c == n - 1))
                rf.append(ident if s < self.bc else c)
                rb.append(ident if s < self.bc else n - 1 - c)
        return [np.asarray(a, np.int32) for a in (fb, bb, sq, fi, la, rf, rb)]


def _mod_kernel(c_ref, w_ref, b_ref, o_ref):
    s = _silu(c_ref[...])
    o_ref[0] = _dot(s.astype(BF16), w_ref[0].astype(BF16)) + b_ref[0]


def _modulation(cond, w_mod, b_mod):
    depth, d, n = w_mod.shape
    nt = n // d
    return pl.pallas_call(
        _mod_kernel,
        grid=(depth, nt),
        in_specs=[pl.BlockSpec((COND_PAD, d), lambda l, j: (0, 0)),
                  pl.BlockSpec((1, d, d), lambda l, j: (l, 0, j)),
                  pl.BlockSpec((1, 1, d), lambda l, j: (l, 0, j))],
        out_specs=pl.BlockSpec((1, COND_PAD, d), lambda l, j: (l, 0, j)),
        out_shape=jax.ShapeDtypeStruct((depth, COND_PAD, n), F32),
        compiler_params=_cparams(2),
        name="modulation",
    )(cond, w_mod, b_mod.reshape(depth, 1, n))


def _two_source(x, tile):
    if isinstance(x, (tuple, list)):
        xa, xb = x
        n_first = xa.shape[0] // tile
    else:
        xa = xb = x
        n_first = x.shape[0] // tile
    d = xa.shape[1]
    specs = [pl.BlockSpec((tile, d), lambda i, *m: (jnp.minimum(i, n_first - 1), 0)),
             pl.BlockSpec((tile, d), lambda i, *m: (jnp.maximum(i - n_first, 0), 0))]
    return xa, xb, n_first, specs


def _read_two_source(xa_ref, xb_ref, n_first):
    return jnp.where(pl.program_id(0) < n_first, xa_ref[...], xb_ref[...])


def _inproj_kernel(cr_ref, xa_ref, xb_ref, g_ref, sc_ref, sh_ref, *refs, n_first):
    n = len(refs) // 2
    x = _read_two_source(xa_ref, xb_ref, n_first)
    h = _rms(x, g_ref[...]) * (1.0 + sc_ref[0]) + sh_ref[0]
    hb = h.astype(BF16)
    for w_ref, o_ref in zip(refs[:n], refs[n:]):
        o_ref[...] = _dot(hb, w_ref[...]).astype(o_ref.dtype)


def _inproj(lay, x, g, sc, sh, weights, out_dtypes):
    t, d = lay.t, g.shape[0]
    cr = lay.cond_rows(TM_PROJ)
    xa, xb, n_first, x_specs = _two_source(x, TM_PROJ)
    row = lambda i, cr: (i, 0)
    const = lambda i, cr: (0, 0)
    cond = lambda i, cr: (cr[i], 0, 0)
    in_specs = x_specs + [pl.BlockSpec((1, d), const), pl.BlockSpec((1, 1, d), cond), pl.BlockSpec((1, 1, d), cond)]
    in_specs += [pl.BlockSpec(w.shape, const) for w in weights]
    out_specs = [pl.BlockSpec((TM_PROJ, w.shape[1]), row) for w in weights]
    out_shape = [jax.ShapeDtypeStruct((t, w.shape[1]), dt) for w, dt in zip(weights, out_dtypes)]
    return pl.pallas_call(
        functools.partial(_inproj_kernel, n_first=n_first),
        grid_spec=pltpu.PrefetchScalarGridSpec(num_scalar_prefetch=1, grid=(t // TM_PROJ,),
                                               in_specs=in_specs, out_specs=out_specs),
        out_shape=out_shape,
        compiler_params=_cparams(1),
        name="inproj",
    )(cr, xa, xb, g.reshape(1, d), sc, sh, *weights)


def _rope(x, c, s):
    return x * c + pltpu.roll(x, RET_DK // 2, 1) * s


def _ret_kernel(fb_ref, bb_ref, sq_ref, fi_ref, la_ref, rf_ref, rb_ref,
                lg_ref, qf_ref, kf_ref, vf_ref, qb_ref, kb_ref, vb_ref,
                cf_ref, sf_ref, cb_ref, sb_ref, s0_ref,
                of_ref, ob_ref, sout_ref,
                st_ref, dm_ref, dec_ref, *, n_ctx):
    g = pl.program_id(0)
    c = TOK

    @pl.when(g == 0)
    def _():
        ii = lax.broadcasted_iota(I32, (c, c), 0)
        jj = lax.broadcasted_iota(I32, (c, c), 1)
        diff = (ii - jj).astype(F32)
        ri = lax.broadcasted_iota(I32, (c, RET_DK), 0).astype(F32)
        for h in range(RET_HEADS):
            lf = lg_ref[0, h]
            lb = lg_ref[1, h]
            low = jnp.exp(lf * jnp.maximum(diff, 0.0))
            up = jnp.exp(lb * jnp.maximum(-diff, 0.0))
            dm_ref[h] = jnp.where(diff > 0, low, jnp.where(diff < 0, up, 2.0))
            dec_ref[0, h] = jnp.exp(lf * (ri + 1.0))
            dec_ref[1, h] = jnp.exp(lf * (c - 1.0 - ri))
            dec_ref[2, h] = jnp.exp(lb * (c - ri))
            dec_ref[3, h] = jnp.exp(lb * ri)

    @pl.when((fi_ref[g] == 1) & (sq_ref[g] < n_ctx))
    def _():
        st_ref[...] = jnp.zeros_like(st_ref)

    @pl.when((fi_ref[g] == 1) & (sq_ref[g] >= n_ctx))
    def _():
        st_ref[...] = s0_ref[0]

    scale = RET_DK ** -0.5
    cf, sf, cb, sb = cf_ref[...], sf_ref[...], cb_ref[...], sb_ref[...]
    zero_row = jnp.zeros((1, RET_DK), F32)
    for h in range(RET_HEADS):
        sl = slice(h * RET_DK, (h + 1) * RET_DK)
        lf = lg_ref[0, h]
        lb = lg_ref[1, h]
        q = _rope(qf_ref[:, sl].astype(F32), cf, sf)
        k = _rope(kf_ref[:, sl].astype(F32), cf, sf) * scale
        v = vf_ref[:, sl]
        s = _dot_nt(q.astype(BF16), k.astype(BF16)) * dm_ref[h]
        st = st_ref[0, h]
        o = _dot(s.astype(BF16), v) + _dot((q * dec_ref[0, h]).astype(BF16), st.astype(BF16))
        of_ref[:, sl] = o
        st_ref[0, h] = jnp.exp(zero_row + lf * c) * st + _dot_tn((k * dec_ref[1, h]).astype(BF16), v)
        q = _rope(qb_ref[:, sl].astype(F32), cb, sb)
        k = _rope(kb_ref[:, sl].astype(F32), cb, sb) * scale
        v = vb_ref[:, sl]
        st = st_ref[1, h]
        ob_ref[:, sl] = _dot((q * dec_ref[2, h]).astype(BF16), st.astype(BF16))
        st_ref[1, h] = jnp.exp(zero_row + lb * c) * st + _dot_tn((k * dec_ref[3, h]).astype(BF16), v)

    @pl.when((la_ref[g] == 1) & (sq_ref[g] < n_ctx))
    def _():
        sout_ref[0] = st_ref[...]


def _retention(lay, proj, log_gamma, rope_c, rope_s, s0):
    t = proj.shape[0]
    sched = lay.scan_schedule(TOK)
    qcol, kcol, vcol = (2 * CONV_CH) // RET_W, (2 * CONV_CH) // RET_W + 1, (2 * CONV_CH) // RET_W + 2

    def tok(which, col):
        return pl.BlockSpec((TOK, RET_W), lambda g, *m: (m[which][g], col))

    def rope(which):
        return pl.BlockSpec((TOK, RET_DK), lambda g, *m: (m[which][g], 0))

    n_ctx = lay.bc
    sblk = (1, 2, RET_HEADS, RET_DK, RET_DK)
    state_in = pl.BlockSpec(sblk, lambda g, *m: (jnp.maximum(m[2][g] - n_ctx, 0), 0, 0, 0, 0))
    state_out = pl.BlockSpec(sblk, lambda g, *m: (jnp.minimum(m[2][g], n_ctx - 1), 0, 0, 0, 0))
    in_specs = [pl.BlockSpec(memory_space=pltpu.SMEM),
                tok(0, qcol), tok(0, kcol), tok(0, vcol), tok(1, qcol), tok(1, kcol), tok(1, vcol),
                rope(5), rope(5), rope(6), rope(6), state_in]
    out_specs = [pl.BlockSpec((TOK, RET_W), lambda g, *m: (m[0][g], 0)),
                 pl.BlockSpec((TOK, RET_W), lambda g, *m: (m[1][g], 0)), state_out]
    out_shape = [jax.ShapeDtypeStruct((t, RET_W), F32), jax.ShapeDtypeStruct((t, RET_W), F32),
                 jax.ShapeDtypeStruct((n_ctx,) + s0.shape[1:], F32)]
    scratch = [pltpu.VMEM((2, RET_HEADS, RET_DK, RET_DK), F32),
               pltpu.VMEM((RET_HEADS, TOK, TOK), F32),
               pltpu.VMEM((4, RET_HEADS, TOK, RET_DK), F32)]
    return pl.pallas_call(
        functools.partial(_ret_kernel, n_ctx=n_ctx),
        grid_spec=pltpu.PrefetchScalarGridSpec(num_scalar_prefetch=7, grid=(len(sched[0]),),
                                               in_specs=in_specs, out_specs=out_specs,
                                               scratch_shapes=scratch),
        out_shape=out_shape,
        compiler_params=_cparams(1),
        name="retention",
    )(*sched, log_gamma, proj, proj, proj, proj, proj, proj, rope_c, rope_s, rope_c, rope_s, s0)


def _even_post_kernel(cr_ref, le_ref, re_ref,
                      glu_ref, prev_ref, next_ref, gt_ref, of_ref, ob_ref, xa_ref, xb_ref,
                      cw_ref, lng_ref, lnb_ref, wout_ref, npost_ref, g1_ref,
                      out_ref, buf_ref, *, n_first):
    i = pl.program_id(0)

    def glu(r):
        r = r.astype(F32)
        return r[:, :CONV_CH] * jax.nn.sigmoid(r[:, CONV_CH:])

    keep_l = jnp.where(le_ref[i] == 1, 0.0, 1.0)
    keep_r = jnp.where(re_ref[i] == 1, 0.0, 1.0)
    buf_ref[0:HALO, :] = glu(prev_ref[...]) * keep_l
    buf_ref[HALO:HALO + TOK, :] = glu(glu_ref[...])
    buf_ref[HALO + TOK:, :] = glu(next_ref[...]) * keep_r
    off = HALO - CONV_W // 2
    sub = 8
    cols = []
    for cb in range(CONV_CH // LANES):
        ls = slice(cb * LANES, (cb + 1) * LANES)
        acc = jnp.zeros((TOK, LANES), F32)
        for r in range(sub):
            part = None
            for m in range((off + CONV_W - 1) // sub + 1):
                j = sub * m + r - off
                if 0 <= j < CONV_W:
                    term = buf_ref[sub * m:sub * m + TOK + sub, ls] * cw_ref[j:j + 1, ls]
                    part = term if part is None else part + term
            acc = acc + part[r:r + TOK, :]
        cols.append(acc)
    acc = jnp.concatenate(cols, axis=1)
    mu = jnp.mean(acc, axis=-1, keepdims=True)
    ac = acc - mu
    y = ac * lax.rsqrt(jnp.mean(ac * ac, axis=-1, keepdims=True) + EPS) * lng_ref[...] + lnb_ref[...]
    conv_out = _silu(y)

    o = of_ref[...] + ob_ref[...]
    gt = gt_ref[...].astype(F32)
    parts = []
    for h in range(RET_HEADS):
        sl = slice(h * RET_DK, (h + 1) * RET_DK)
        oh = o[:, sl]
        oc = oh - jnp.mean(oh, axis=-1, keepdims=True)
        on = oc * lax.rsqrt(jnp.mean(oc * oc, axis=-1, keepdims=True) + EPS)
        parts.append(on * _silu(gt[:, sl]))
    ret_out = jnp.concatenate(parts, axis=1)

    out = _dot(conv_out.astype(BF16), wout_ref[0:CONV_CH, :]) + _dot(ret_out.astype(BF16), wout_ref[CONV_CH:, :])
    out_ref[...] = _read_two_source(xa_ref, xb_ref, n_first) + g1_ref[0] * _rms(out, npost_ref[...])


def _even_post(lay, proj, o_f, o_b, x, conv_w, ln_g, ln_b, w_out, n_post, g1):
    t, d = lay.t, n_post.shape[0]
    cr = lay.cond_rows(TOK)
    le, re = lay.edges(TOK)
    xa, xb, n_first, x_specs = _two_source(x, TOK)
    hb = TOK // HALO
    n_halo = t // HALO
    row = lambda i, *m: (i, 0)
    const = lambda i, *m: (0, 0)
    in_specs = [pl.BlockSpec((TOK, 2 * CONV_CH), row),
                pl.BlockSpec((HALO, 2 * CONV_CH), lambda i, *m: (jnp.maximum(i * hb - 1, 0), 0)),
                pl.BlockSpec((HALO, 2 * CONV_CH), lambda i, *m: (jnp.minimum((i + 1) * hb, n_halo - 1), 0)),
                pl.BlockSpec((TOK, RET_W), lambda i, *m: (i, EVEN_IN // RET_W - 1)),
                pl.BlockSpec((TOK, RET_W), row), pl.BlockSpec((TOK, RET_W), row)] + x_specs + [
                pl.BlockSpec(conv_w.shape, const), pl.BlockSpec((1, CONV_CH), const), pl.BlockSpec((1, CONV_CH), const),
                pl.BlockSpec(w_out.shape, const), pl.BlockSpec((1, d), const),
                pl.BlockSpec((1, 1, d), lambda i, *m: (m[0][i], 0, 0))]
    return pl.pallas_call(
        functools.partial(_even_post_kernel, n_first=n_first),
        grid_spec=pltpu.PrefetchScalarGridSpec(num_scalar_prefetch=3, grid=(t // TOK,),
                                               in_specs=in_specs, out_specs=pl.BlockSpec((TOK, d), row),
                                               scratch_shapes=[pltpu.VMEM((TOK + 2 * HALO, CONV_CH), F32)]),
        out_shape=jax.ShapeDtypeStruct((t, d), F32),
        compiler_params=_cparams(1),
        name="even_post",
    )(cr, le, re, proj, proj, proj, proj, o_f, o_b, xa, xb, conv_w, ln_g.reshape(1, -1), ln_b.reshape(1, -1),
      w_out, n_post.reshape(1, d), g1)


XHALO = 8
QKV_BLOCK = 256


def _odd_in_kernel(cr_ref, le_ref, re_ref, x_ref, xp_ref, xn_ref, g_ref, sc_ref, sh_ref,
                   wqkv_ref, wz_ref, wab_ref, cw_ref, alog_ref, dtb_ref,
                   qkv_ref, z_ref, gb_ref):
    i = pl.program_id(0)
    n = TOK
    keep_l = jnp.where(le_ref[i] == 1, 0.0, 1.0)
    keep_r = jnp.where(re_ref[i] == 1, 0.0, 1.0)
    xs = jnp.concatenate([xp_ref[...], x_ref[...], xn_ref[...]], axis=0)
    h = _rms(xs, g_ref[...]) * (1.0 + sc_ref[0]) + sh_ref[0]
    rows = lax.broadcasted_iota(I32, (n + 2 * XHALO, 1), 0)
    keep = jnp.where(rows < XHALO, keep_l, jnp.where(rows >= n + XHALO, keep_r, 1.0))
    hb = (h * keep).astype(BF16)
    off = XHALO - SHORT_W // 2
    heads_per_block = QKV_BLOCK // GDN_DK
    for blk in range((2 * GDN_KW + GDN_VW) // QKV_BLOCK):
        cs = slice(blk * QKV_BLOCK, (blk + 1) * QKV_BLOCK)
        p = _dot(hb, wqkv_ref[:, cs])
        acc = jnp.zeros((n, QKV_BLOCK), F32)
        for j in range(SHORT_W):
            acc = acc + p[off + j:off + j + n, :] * cw_ref[j:j + 1, cs]
        y = _silu(acc)
        for hh in range(heads_per_block):
            head = blk * heads_per_block + hh
            yh = y[:, hh * GDN_DK:(hh + 1) * GDN_DK]
            if head < 2 * GDN_HEADS:
                yh = yh * lax.rsqrt(jnp.sum(yh * yh, axis=-1, keepdims=True) + EPS)
                if head < GDN_HEADS:
                    yh = yh * (GDN_DK ** -0.5)
            qkv_ref[:, head * GDN_DK:(head + 1) * GDN_DK] = yh.astype(qkv_ref.dtype)
    hc = hb[XHALO:XHALO + n, :]
    z_ref[...] = _dot(hc, wz_ref[...]).astype(z_ref.dtype)
    ab = _dot(hc, wab_ref[...])
    zz = ab + dtb_ref[...]
    softplus = jnp.maximum(zz, 0.0) + jnp.log(1.0 + jnp.exp(-jnp.abs(zz)))
    gate = -jnp.exp(alog_ref[...]) * softplus
    beta = jax.nn.sigmoid(ab)
    lane = lax.broadcasted_iota(I32, ab.shape, 1)
    gb_ref[...] = jnp.where(lane < 2 * GDN_HEADS, gate, beta)


def _odd_in(lay, x, g, sc, sh, w_qkv, w_z, w_ab, conv_w, alog_row, dtb_row):
    t, d = x.shape
    w = w_qkv.shape[1]
    cr = lay.cond_rows(TOK)
    le, re = lay.edges(TOK)
    hb = TOK // XHALO
    n_halo = t // XHALO
    row = lambda i, *m: (i, 0)
    const = lambda i, *m: (0, 0)
    cond = lambda i, *m: (m[0][i], 0, 0)
    in_specs = [pl.BlockSpec((TOK, d), row),
                pl.BlockSpec((XHALO, d), lambda i, *m: (jnp.maximum(i * hb - 1, 0), 0)),
                pl.BlockSpec((XHALO, d), lambda i, *m: (jnp.minimum((i + 1) * hb, n_halo - 1), 0)),
                pl.BlockSpec((1, d), const), pl.BlockSpec((1, 1, d), cond), pl.BlockSpec((1, 1, d), cond),
                pl.BlockSpec(w_qkv.shape, const), pl.BlockSpec(w_z.shape, const), pl.BlockSpec(w_ab.shape, const),
                pl.BlockSpec(conv_w.shape, const), pl.BlockSpec((1, LANES), const), pl.BlockSpec((1, LANES), const)]
    out_specs = [pl.BlockSpec((TOK, w), row), pl.BlockSpec((TOK, w_z.shape[1]), row), pl.BlockSpec((TOK, LANES), row)]
    out_shape = [jax.ShapeDtypeStruct((t, w), BF16), jax.ShapeDtypeStruct((t, w_z.shape[1]), BF16),
                 jax.ShapeDtypeStruct((t, LANES), F32)]
    return pl.pallas_call(
        _odd_in_kernel,
        grid_spec=pltpu.PrefetchScalarGridSpec(num_scalar_prefetch=3, grid=(t // TOK,),
                                               in_specs=in_specs, out_specs=out_specs),
        out_shape=out_shape,
        compiler_params=_cparams(1),
        name="odd_in",
    )(cr, le, re, x, x, x, g.reshape(1, d), sc, sh, w_qkv, w_z, w_ab, conv_w, alog_row, dtb_row)


_GDN_LEVELS = tuple(2 ** p for p in range(int(math.log2(GDN_CHUNK))))
GDN_GROUPS = ((0, 1),)


def _gdn_masks(msk_ref, tri_ref):
    c = GDN_CHUNK
    ii = lax.broadcasted_iota(I32, (c, c), 0)
    jj = lax.broadcasted_iota(I32, (c, c), 1)
    one = jnp.ones((c, c), F32)
    zero = jnp.zeros((c, c), F32)
    for rev in (0, 1):
        a, b = (ii, jj) if rev == 0 else (jj, ii)
        base = rev * 9
        msk_ref[base + 0] = jnp.where(a >= b, one, zero)
        msk_ref[base + 1] = jnp.where(a > b, one, zero)
        for l, m in enumerate(_GDN_LEVELS):
            sh = int(math.log2(m))
            ab_, bb_ = a >> sh, b >> sh
            hit = ((ab_ & 1) == 1) & (bb_ == ab_ - 1)
            msk_ref[base + 2 + l] = jnp.where(hit, one, zero)
        tri_ref[rev] = jnp.where(a >= b, one, zero).astype(BF16)


def _gdn_chunk_step(dirs, st_ref, msk_ref, tri_ref, revs):
    c = GDN_CHUNK
    nh = GDN_HEADS
    probs = [(rev, h) for rev in revs for h in range(nh)]
    ones = jnp.ones((c, c), BF16)
    gcum, gcum_t, e_all, kdec_all, elast_all, gbs = {}, {}, {}, {}, {}, {}
    for rev in revs:
        gb = dirs[rev][3][...]
        g1, g2, g3 = _split3(gb)
        tri = tri_ref[rev]
        gc = _dot(tri, g1) + _dot(tri, g2) + _dot(tri, g3)
        gl = _dot(ones, g1) + _dot(ones, g2) + _dot(ones, g3)
        gbs[rev] = gb
        gcum[rev] = gc
        gcum_t[rev] = gc.T
        e_all[rev] = jnp.exp(gc)
        kdec_all[rev] = jnp.exp(gl - gc)
        elast_all[rev] = jnp.exp(gl)

    def col(rev, h):
        return rev * nh + h

    def hsl(h):
        return slice(h * GDN_DK, (h + 1) * GDN_DK)

    st = [st_ref[rev, h] for rev, h in probs]

    a, attn, kbs = [], [], []
    for rev, h in probs:
        base = rev * 9
        p = col(rev, h)
        q_ref, k_ref = dirs[rev][0], dirs[rev][1]
        k = k_ref[:, hsl(h)]
        gcb = jnp.broadcast_to(gcum[rev][:, p:p + 1], (c, c))
        grb = jnp.broadcast_to(gcum_t[rev][p:p + 1, :], (c, c))
        decay = jnp.exp(jnp.where(msk_ref[base] > 0, gcb - grb, NEG_BIG))
        kb = k.astype(F32) * gbs[rev][:, 2 * nh + p:2 * nh + p + 1]
        kbs.append(kb)
        a.append(_dot_nt(kb.astype(BF16), k) * decay * msk_ref[base + 1])
        attn.append((_dot_nt(q_ref[:, hsl(h)], k) * decay).astype(BF16))

    tinv = []
    for i, (rev, h) in enumerate(probs):
        base = rev * 9
        tinv.append(msk_ref[base] - msk_ref[base + 1] - a[i] * msk_ref[base + 2])
    for l in range(1, len(_GDN_LEVELS)):
        bt = []
        for i, (rev, h) in enumerate(probs):
            bt.append(_dot((a[i] * msk_ref[rev * 9 + 2 + l]).astype(BF16), tinv[i].astype(BF16)))
        for i in range(len(probs)):
            tinv[i] = tinv[i] - _dot(tinv[i].astype(BF16), bt[i].astype(BF16))

    sol = []
    for i, (rev, h) in enumerate(probs):
        p = col(rev, h)
        v = dirs[rev][2][:, hsl(h)].astype(F32)
        beta = gbs[rev][:, 2 * nh + p:2 * nh + p + 1]
        rhs = jnp.concatenate([v * beta, kbs[i] * e_all[rev][:, p:p + 1]], axis=1)
        sol.append(_dot(tinv[i].astype(BF16), rhs.astype(BF16)))
    v_new = []
    for i in range(len(probs)):
        u, w = sol[i][:, :GDN_DK], sol[i][:, GDN_DK:]
        v_new.append((u - _dot(w.astype(BF16), st[i].astype(BF16))).astype(BF16))
    for i, (rev, h) in enumerate(probs):
        p = col(rev, h)
        q = dirs[rev][0][:, hsl(h)].astype(F32)
        qs = (q * e_all[rev][:, p:p + 1]).astype(BF16)
        dirs[rev][4][:, hsl(h)] = (_dot(qs, st[i].astype(BF16)) + _dot(attn[i], v_new[i])).astype(BF16)
    new_st = []
    for i, (rev, h) in enumerate(probs):
        p = col(rev, h)
        kd = (dirs[rev][1][:, hsl(h)].astype(F32) * kdec_all[rev][:, p:p + 1]).astype(BF16)
        new_st.append(st[i] * elast_all[rev][:, p:p + 1] + _dot_tn(kd, v_new[i]))
    for i, (rev, h) in enumerate(probs):
        st_ref[rev, h] = new_st[i]


def _gdn_kernel(fb_ref, bb_ref, sq_ref, fi_ref, la_ref,
                qf_ref, kf_ref, vf_ref, gf_ref, qb_ref, kb_ref, vb_ref, gbk_ref, s0_ref,
                of_ref, ob_ref, sout_ref,
                st_ref, msk_ref, tri_ref, *, n_ctx):
    g = pl.program_id(0)

    @pl.when(g == 0)
    def _():
        _gdn_masks(msk_ref, tri_ref)

    @pl.when((fi_ref[g] == 1) & (sq_ref[g] < n_ctx))
    def _():
        st_ref[...] = jnp.zeros_like(st_ref)

    @pl.when((fi_ref[g] == 1) & (sq_ref[g] >= n_ctx))
    def _():
        st_ref[...] = s0_ref[0]

    dirs = ((qf_ref, kf_ref, vf_ref, gf_ref, of_ref), (qb_ref, kb_ref, vb_ref, gbk_ref, ob_ref))
    for revs in GDN_GROUPS:
        _gdn_chunk_step(dirs, st_ref, msk_ref, tri_ref, revs)

    @pl.when((la_ref[g] == 1) & (sq_ref[g] < n_ctx))
    def _():
        sout_ref[0] = st_ref[...]


def _gdn(lay, qkv, gb, s0):
    t = qkv.shape[0]
    c = GDN_CHUNK
    sched = lay.scan_schedule(c)[:5]

    def tok(which, col):
        return pl.BlockSpec((c, GDN_KW), lambda g, *m: (m[which][g], col))

    def gate(which):
        return pl.BlockSpec((c, LANES), lambda g, *m: (m[which][g], 0))

    n_ctx = lay.bc
    sblk = (1, 2, GDN_HEADS, GDN_DK, GDN_DK)
    state_in = pl.BlockSpec(sblk, lambda g, *m: (jnp.maximum(m[2][g] - n_ctx, 0), 0, 0, 0, 0))
    state_out = pl.BlockSpec(sblk, lambda g, *m: (jnp.minimum(m[2][g], n_ctx - 1), 0, 0, 0, 0))
    in_specs = [tok(0, 0), tok(0, 1), tok(0, 2), gate(0), tok(1, 0), tok(1, 1), tok(1, 2), gate(1), state_in]
    out_specs = [pl.BlockSpec((c, GDN_VW), lambda g, *m: (m[0][g], 0)),
                 pl.BlockSpec((c, GDN_VW), lambda g, *m: (m[1][g], 0)), state_out]
    out_shape = [jax.ShapeDtypeStruct((t, GDN_VW), BF16), jax.ShapeDtypeStruct((t, GDN_VW), BF16),
                 jax.ShapeDtypeStruct((n_ctx,) + s0.shape[1:], F32)]
    scratch = [pltpu.VMEM((2, GDN_HEADS, GDN_DK, GDN_DK), F32),
               pltpu.VMEM((18, c, c), F32), pltpu.VMEM((2, c, c), BF16)]
    return pl.pallas_call(
        functools.partial(_gdn_kernel, n_ctx=n_ctx),
        grid_spec=pltpu.PrefetchScalarGridSpec(num_scalar_prefetch=5, grid=(len(sched[0]),),
                                               in_specs=in_specs, out_specs=out_specs,
                                               scratch_shapes=scratch),
        out_shape=out_shape,
        compiler_params=_cparams(1),
        name="gdn_scan",
    )(*sched, qkv, qkv, qkv, gb, qkv, qkv, qkv, gb, s0)


def _odd_post_kernel(cr_ref, of_ref, ob_ref, z_ref, x_ref, nw_ref, wout_ref, npost_ref, g1_ref, out_ref):
    o = of_ref[...].astype(F32) + ob_ref[...].astype(F32)
    z = z_ref[...].astype(F32)
    nw = nw_ref[...]
    parts = []
    for h in range(GDN_HEADS):
        sl = slice(h * GDN_DK, (h + 1) * GDN_DK)
        parts.append(_rms(o[:, sl], nw) * _silu(z[:, sl]))
    y = jnp.concatenate(parts, axis=1)
    out = _dot(y.astype(BF16), wout_ref[...])
    out_ref[...] = x_ref[...] + g1_ref[0] * _rms(out, npost_ref[...])


def _odd_post(lay, o_f, o_b, proj, x, norm_w, w_out, n_post, g1):
    t, d = x.shape
    cr = lay.cond_rows(TOK)
    row = lambda i, *m: (i, 0)
    const = lambda i, *m: (0, 0)
    in_specs = [pl.BlockSpec((TOK, GDN_VW), row), pl.BlockSpec((TOK, GDN_VW), row),
                pl.BlockSpec((TOK, GDN_VW), row),
                pl.BlockSpec((TOK, d), row),
                pl.BlockSpec((1, GDN_DK), const), pl.BlockSpec(w_out.shape, const), pl.BlockSpec((1, d), const),
                pl.BlockSpec((1, 1, d), lambda i, *m: (m[0][i], 0, 0))]
    return pl.pallas_call(
        _odd_post_kernel,
        grid_spec=pltpu.PrefetchScalarGridSpec(num_scalar_prefetch=1, grid=(t // TOK,),
                                               in_specs=in_specs, out_specs=pl.BlockSpec((TOK, d), row)),
        out_shape=jax.ShapeDtypeStruct((t, d), F32),
        compiler_params=_cparams(1),
        name="odd_post",
    )(cr, o_f, o_b, proj, x, norm_w.reshape(1, -1), w_out, n_post.reshape(1, d), g1)


U32 = jnp.uint32
PSUB = D_MODEL // (2 * LANES)
HI_MASK = 0xFFFF0000
N_PAIRS = EXP_PER_GROUP * (EXP_PER_GROUP - 1) // 2
N_CLASSES = N_GROUPS * N_PAIRS
CLS_PAD = 32
PAIR_LO = (0, 0, 0, 1, 1, 2)
PAIR_HI = (1, 2, 3, 2, 3, 3)


def _pack_pairs(x):
    half = x.shape[1] // 2
    lo = lax.bitcast_convert_type(x[:, :half].astype(BF16).astype(F32), U32) >> 16
    hi = lax.bitcast_convert_type(x[:, half:].astype(BF16).astype(F32), U32) & jnp.uint32(HI_MASK)
    return lo | hi


def _unpack_pairs(w):
    lo = lax.bitcast_convert_type(w << 16, F32)
    hi = lax.bitcast_convert_type(w & jnp.uint32(HI_MASK), F32)
    return jnp.concatenate([lo, hi], axis=1)


def _store_token_major(ref, w):
    n = w.shape[0]
    for s in range(PSUB):
        ref[pl.ds(s, n, stride=PSUB), :] = w[:, s * LANES:(s + 1) * LANES]


def _load_token_major(ref, n):
    return jnp.concatenate([ref[pl.ds(s, n, stride=PSUB), :] for s in range(PSUB)], axis=1)


def _router_kernel(cr_ref, x_ref, g_ref, sc_ref, sh_ref, rwt_ref, rb_ref,
                   h_ref, route_ref, wts_ref, cnt_ref,
                   base_ref, su_ref):
    i = pl.program_id(0)
    tm = TM_ROUTE

    @pl.when(i == 0)
    def _():
        base_ref[...] = jnp.zeros_like(base_ref)
        ii = lax.broadcasted_iota(I32, (tm, tm), 0)
        jj = lax.broadcasted_iota(I32, (tm, tm), 1)
        su_ref[...] = jnp.where(ii < jj, 1.0, 0.0).astype(BF16)

    h = _rms(x_ref[...], g_ref[...]) * (1.0 + sc_ref[0]) + sh_ref[0]
    hb = h.astype(BF16)
    _store_token_major(h_ref, _pack_pairs(h))
    h1 = hb
    w1, w2, w3 = _split3(rwt_ref[...])
    logits = _dot_nt(w1, h1) + _dot_nt(w2, h1) + _dot_nt(w3, h1)
    score = jax.nn.sigmoid(logits)
    sel = score + rb_ref[...]

    def row(a, e):
        return a[e:e + 1, :]

    gsum = []
    for gi in range(N_GROUPS):
        a, b, c, d = (row(sel, gi * EXP_PER_GROUP + j) for j in range(EXP_PER_GROUP))
        hi1, lo1 = jnp.maximum(a, b), jnp.minimum(a, b)
        hi2, lo2 = jnp.maximum(c, d), jnp.minimum(c, d)
        gsum.append(jnp.maximum(hi1, hi2) + jnp.maximum(jnp.minimum(hi1, hi2), jnp.maximum(lo1, lo2)))
    best = jnp.zeros_like(gsum[0]).astype(I32)
    cur = gsum[0]
    for gi in range(1, N_GROUPS):
        upd = gsum[gi] > cur
        best = jnp.where(upd, gi, best)
        cur = jnp.where(upd, gsum[gi], cur)

    def pick(arr, j):
        out = row(arr, j)
        for gi in range(1, N_GROUPS):
            out = jnp.where(best == gi, row(arr, gi * EXP_PER_GROUP + j), out)
        return out

    vals = [pick(sel, j) for j in range(EXP_PER_GROUP)]
    scs = [pick(score, j) for j in range(EXP_PER_GROUP)]

    def argmax_first(vs):
        idx = jnp.zeros_like(best)
        m = vs[0]
        for j in range(1, EXP_PER_GROUP):
            upd = vs[j] > m
            idx = jnp.where(upd, j, idx)
            m = jnp.where(upd, vs[j], m)
        return idx

    i1 = argmax_first(vals)
    vals2 = [jnp.where(i1 == j, -jnp.inf, vals[j]) for j in range(EXP_PER_GROUP)]
    i2 = argmax_first(vals2)

    def take(vs, idx):
        out = vs[0]
        for j in range(1, EXP_PER_GROUP):
            out = jnp.where(idx == j, vs[j], out)
        return out

    s1, s2 = take(scs, i1), take(scs, i2)
    tot = s1 + s2
    first_low = i1 < i2
    lo = jnp.minimum(i1, i2)
    hi = jnp.maximum(i1, i2)
    pair = jnp.where(lo == 0, hi - 1, jnp.where(lo == 1, hi + 1, N_PAIRS - 1))
    cls = best * N_PAIRS + pair
    wts_ref[0:1, :] = jnp.where(first_low, s1, s2) / tot
    wts_ref[1:2, :] = jnp.where(first_low, s2, s1) / tot

    crow = lax.broadcasted_iota(I32, (CLS_PAD, tm), 0)
    onehot = jnp.where(crow == cls, 1.0, 0.0)
    before = _dot(onehot.astype(BF16), su_ref[...]) + base_ref[...]
    route_ref[0:1, :] = cls
    route_ref[1:2, :] = jnp.sum(onehot * before, axis=0, keepdims=True).astype(I32)
    base_ref[...] = base_ref[...] + jnp.sum(onehot, axis=1, keepdims=True)
    cnt_ref[...] = jnp.broadcast_to(base_ref[...], cnt_ref.shape).astype(I32)


def _router(lay, x, g, sc, sh, rwt, rbias):
    t, d = x.shape
    cr = lay.cond_rows(TM_ROUTE)
    tm = TM_ROUTE
    row = lambda i, cr: (i, 0)
    col = lambda i, cr: (0, i)
    const = lambda i, cr: (0, 0)
    cond = lambda i, cr: (cr[i], 0, 0)
    in_specs = [pl.BlockSpec((tm, d), row), pl.BlockSpec((1, d), const),
                pl.BlockSpec((1, 1, d), cond), pl.BlockSpec((1, 1, d), cond),
                pl.BlockSpec((N_EXPERTS, d), const), pl.BlockSpec((N_EXPERTS, 1), const)]
    out_specs = [pl.BlockSpec((tm * PSUB, LANES), row), pl.BlockSpec((2, tm), col), pl.BlockSpec((2, tm), col),
                 pl.BlockSpec((CLS_PAD, LANES), const)]
    out_shape = [jax.ShapeDtypeStruct((t * PSUB, LANES), U32), jax.ShapeDtypeStruct((2, t), I32),
                 jax.ShapeDtypeStruct((2, t), F32), jax.ShapeDtypeStruct((CLS_PAD, LANES), I32)]
    return pl.pallas_call(
        _router_kernel,
        grid_spec=pltpu.PrefetchScalarGridSpec(num_scalar_prefetch=1, grid=(t // tm,),
                                               in_specs=in_specs, out_specs=out_specs,
                                               scratch_shapes=[pltpu.VMEM((CLS_PAD, 1), F32),
                                                               pltpu.VMEM((tm, tm), BF16)]),
        out_shape=out_shape,
        compiler_params=_cparams(1),
        name="router",
    )(cr, x, g.reshape(1, d), sc, sh, rwt, rbias.reshape(N_EXPERTS, 1))


def _row_copy(src, dst, sem):
    return pltpu.make_async_copy(src, dst, sem)


def _dispatch_kernel(ends_ref, padded_ref, pos_ref, h_ref, sorted_ref, zero_ref, stage_ref, sem, zsem, *,
                     n_tiles, n_steps):
    n = TM_DISP
    i = pl.program_id(0)
    slot = lax.rem(i, 2)

    @pl.when(i == 0)
    def _():
        zero_ref[...] = jnp.zeros_like(zero_ref)

        def fill_tile(start_row):
            start = pl.multiple_of(start_row * PSUB, PSUB)
            fill = _row_copy(zero_ref, sorted_ref.at[pl.ds(start, TM_EXP * PSUB)], zsem)
            fill.start()
            fill.wait()

        for c in range(N_CLASSES):
            @pl.when(padded_ref[c] > 0)
            def _():
                fill_tile(ends_ref[c] - TM_EXP)
        for j in range(n_tiles - N_CLASSES, n_tiles):
            @pl.when(j * TM_EXP >= ends_ref[N_CLASSES - 1])
            def _():
                fill_tile(j * TM_EXP)

    stage_ref[slot] = h_ref[...]

    def issue(r, carry):
        for k in range(2):
            tok = 2 * r + k
            src = stage_ref.at[slot, pl.ds(pl.multiple_of(tok * PSUB, PSUB), PSUB)]
            dst = pl.multiple_of(pos_ref[0, 0, tok] * PSUB, PSUB)
            _row_copy(src, sorted_ref.at[pl.ds(dst, PSUB)], sem.at[slot]).start(priority=k)
        return carry

    lax.fori_loop(0, n // 2, issue, 0, unroll=8)

    def drain(s):
        _row_copy(stage_ref.at[s], sorted_ref.at[pl.ds(0, n * PSUB)], sem.at[s]).wait()

    @pl.when(i > 0)
    def _():
        drain(1 - slot)

    @pl.when(i == n_steps - 1)
    def _():
        drain(slot)


def _dispatch(h, pos3, ends, padded, n_rows):
    t = h.shape[0] // PSUB
    n_steps = t // TM_DISP
    return pl.pallas_call(
        functools.partial(_dispatch_kernel, n_tiles=n_rows // TM_EXP, n_steps=n_steps),
        grid_spec=pltpu.PrefetchScalarGridSpec(
            num_scalar_prefetch=2, grid=(n_steps,),
            in_specs=[pl.BlockSpec((1, 1, TM_DISP), lambda i, *m: (i, 0, 0), memory_space=pltpu.SMEM),
                      pl.BlockSpec((TM_DISP * PSUB, LANES), lambda i, *m: (i, 0))],
            out_specs=pl.BlockSpec(memory_space=pl.ANY),
            scratch_shapes=[pltpu.VMEM((TM_EXP * PSUB, LANES), h.dtype),
                            pltpu.VMEM((2, TM_DISP * PSUB, LANES), h.dtype),
                            pltpu.SemaphoreType.DMA((2,)), pltpu.SemaphoreType.DMA(())]),
        out_shape=jax.ShapeDtypeStruct((n_rows * PSUB, LANES), h.dtype),
        compiler_params=_cparams(1),
        name="dispatch",
    )(ends, padded, pos3, h)


def _expert_kernel(ea_ref, eb_ref, tv_ref, h_ref, *refs):
    i = pl.program_id(0)
    w_refs, y_ref, wb_refs = refs[:6], refs[6], refs[7:]
    prev = jnp.maximum(i - 1, 0)

    for slot, e_ref in enumerate((ea_ref, eb_ref)):
        @pl.when((i == 0) | (e_ref[i] != e_ref[prev]))
        def _():
            for j in range(3):
                wb_refs[3 * slot + j][...] = w_refs[3 * slot + j][0, 0].astype(BF16)

    @pl.when(tv_ref[i] == 1)
    def _():
        hb = _unpack_pairs(_load_token_major(h_ref, TM_EXP)).astype(BF16)
        for slot in range(2):
            wg, wu, wd = (wb_refs[3 * slot + j][...] for j in range(3))
            he = (_silu(_dot(hb, wg)) * _dot(hb, wu)).astype(BF16)
            y = _pack_pairs(_dot(he, wd))
            for s in range(PSUB):
                y_ref[pl.ds(slot * PSUB + s, TM_EXP, stride=2 * PSUB), :] = y[:, s * LANES:(s + 1) * LANES]

    @pl.when(tv_ref[i] == 0)
    def _():
        y_ref[...] = jnp.zeros_like(y_ref)


def _experts(hs, tile_ea, tile_eb, tile_valid, wg, wu, wd, layer):
    d = wg.shape[2]
    r = hs.shape[0] // PSUB
    sel_a = lambda i, ea, eb, tv: (layer, ea[i], 0, 0)
    sel_b = lambda i, ea, eb, tv: (layer, eb[i], 0, 0)
    w_specs = []
    for sel in (sel_a, sel_b):
        w_specs += [pl.BlockSpec((1, 1, d, D_EXPERT), sel), pl.BlockSpec((1, 1, d, D_EXPERT), sel),
                    pl.BlockSpec((1, 1, D_EXPERT, d), sel)]
    wb = [pltpu.VMEM((d, D_EXPERT), BF16), pltpu.VMEM((d, D_EXPERT), BF16), pltpu.VMEM((D_EXPERT, d), BF16)]
    return pl.pallas_call(
        _expert_kernel,
        grid_spec=pltpu.PrefetchScalarGridSpec(
            num_scalar_prefetch=3, grid=(r // TM_EXP,),
            in_specs=[pl.BlockSpec((TM_EXP * PSUB, LANES), lambda i, ea, eb, tv: (i * tv[i], 0))] + w_specs,
            out_specs=pl.BlockSpec((TM_EXP * 2 * PSUB, LANES), lambda i, ea, eb, tv: (i, 0)),
            scratch_shapes=wb + wb),
        out_shape=jax.ShapeDtypeStruct((r * 2 * PSUB, LANES), U32),
        compiler_params=_cparams(1),
        name="experts",
    )(tile_ea, tile_eb, tile_valid, hs, wg, wu, wd, wg, wu, wd)


def _combine_kernel(cr_ref, pos_ref, posn_ref, ys_ref, wts_ref, x_ref, npost_ref, g2_ref, *rest, n_first, n_steps):
    n = TOK
    buf_ref, sem = rest[-2:]
    outs = rest[:-2]
    i = pl.program_id(0)
    wide = 2 * PSUB

    def gather(p_ref, slot):
        def issue(r, carry):
            for k in range(2):
                tok = 2 * r + k
                src = pl.multiple_of(p_ref[0, 0, tok] * wide, wide)
                dst = pl.multiple_of(tok * wide, wide)
                _row_copy(ys_ref.at[pl.ds(src, wide)], buf_ref.at[slot, pl.ds(dst, wide)],
                          sem.at[slot]).start(priority=k)
            return carry

        lax.fori_loop(0, n // 2, issue, 0, unroll=8)

    slot = lax.rem(i, 2)

    @pl.when(i == 0)
    def _():
        gather(pos_ref, 0)

    @pl.when(i + 1 < n_steps)
    def _():
        gather(posn_ref, 1 - slot)

    _row_copy(ys_ref.at[pl.ds(0, n * wide)], buf_ref.at[slot], sem.at[slot]).wait()

    w = wts_ref[...]
    cur = buf_ref.at[slot]
    y_lo = _unpack_pairs(jnp.concatenate([cur[pl.ds(s, n, stride=wide), :] for s in range(PSUB)], axis=1))
    y_hi = _unpack_pairs(jnp.concatenate([cur[pl.ds(PSUB + s, n, stride=wide), :] for s in range(PSUB)], axis=1))
    y = y_lo * w[:, 0:1] + y_hi * w[:, 1:2]
    res = x_ref[...] + g2_ref[0] * _rms(y, npost_ref[...])
    if n_first is None:
        outs[0][...] = res
    else:
        @pl.when(i < n_first)
        def _():
            outs[0][...] = res

        @pl.when(i >= n_first)
        def _():
            outs[1][...] = res


def _combine(lay, ys3, pos3, wts, x, n_post, g2, split):
    t, d = x.shape
    cr = lay.cond_rows(TOK)
    row = lambda i, cr: (i, 0)
    const = lambda i, cr: (0, 0)
    n_steps = t // TOK
    in_specs = [pl.BlockSpec((1, 1, TOK), lambda i, cr: (i, 0, 0), memory_space=pltpu.SMEM),
                pl.BlockSpec((1, 1, TOK), lambda i, cr: (jnp.minimum(i + 1, n_steps - 1), 0, 0),
                             memory_space=pltpu.SMEM),
                pl.BlockSpec(memory_space=pl.ANY),
                pl.BlockSpec((TOK, 2), row),
                pl.BlockSpec((TOK, d), row),
                pl.BlockSpec((1, d), const),
                pl.BlockSpec((1, 1, d), lambda i, cr: (cr[i], 0, 0))]
    if split:
        n_first = lay.t_ctx // TOK
        out_specs = [pl.BlockSpec((TOK, d), lambda i, cr: (jnp.minimum(i, n_first - 1), 0)),
                     pl.BlockSpec((TOK, d), lambda i, cr: (jnp.maximum(i - n_first, 0), 0))]
        out_shape = [jax.ShapeDtypeStruct((lay.t_ctx, d), F32), jax.ShapeDtypeStruct((t - lay.t_ctx, d), F32)]
    else:
        n_first = None
        out_specs = [pl.BlockSpec((TOK, d), row)]
        out_shape = [jax.ShapeDtypeStruct((t, d), F32)]
    return pl.pallas_call(
        functools.partial(_combine_kernel, n_first=n_first, n_steps=n_steps),
        grid_spec=pltpu.PrefetchScalarGridSpec(
            num_scalar_prefetch=1, grid=(n_steps,), in_specs=in_specs, out_specs=out_specs,
            scratch_shapes=[pltpu.VMEM((2, TOK * 2 * PSUB, LANES), U32), pltpu.SemaphoreType.DMA((2,))]),
        out_shape=out_shape,
        compiler_params=_cparams(1),
        name="combine",
    )(cr, pos3, pos3, ys3, wts, x, n_post.reshape(1, d), g2)


def _moe(lay, x, g_pre, sc, sh, g2, n_post, rwt, rbias, wg, wu, wd, layer, split):
    t, d = x.shape
    h, route, wts, cnt = _router(lay, x, g_pre, sc, sh, rwt, rbias)
    cls, rank = route[0], route[1]
    counts = cnt[:N_CLASSES, 0]
    padded = ((counts + TM_EXP - 1) // TM_EXP) * TM_EXP
    ends = jnp.cumsum(padded).astype(I32)
    offs = ends - padded
    cids = jnp.arange(N_CLASSES, dtype=I32)[:, None]
    pos = jnp.sum(jnp.where(cls[None] == cids, offs[:, None], 0), axis=0) + rank
    pos3 = pos.reshape(t // TOK, 1, TOK).astype(I32)
    n_rows = t + N_CLASSES * TM_EXP
    n_tiles = n_rows // TM_EXP
    starts = jnp.arange(n_tiles, dtype=I32) * TM_EXP
    tile_cls = jnp.minimum(jnp.sum((starts[:, None] >= ends[None, :]).astype(I32), axis=1), N_CLASSES - 1)
    tile_valid = (starts < ends[-1]).astype(I32)
    grp, pair = tile_cls // N_PAIRS, tile_cls % N_PAIRS
    pids = jnp.arange(N_PAIRS, dtype=I32)[None, :]
    lo = jnp.sum(jnp.where(pair[:, None] == pids, jnp.asarray(PAIR_LO, I32)[None, :], 0), axis=1)
    hi = jnp.sum(jnp.where(pair[:, None] == pids, jnp.asarray(PAIR_HI, I32)[None, :], 0), axis=1)
    tile_ea = (grp * EXP_PER_GROUP + lo).astype(I32)
    tile_eb = (grp * EXP_PER_GROUP + hi).astype(I32)
    hs = _dispatch(h, pos.reshape(t // TM_DISP, 1, TM_DISP).astype(I32), ends, padded.astype(I32), n_rows)
    ys = _experts(hs, tile_ea, tile_eb, tile_valid, wg, wu, wd, layer)
    return _combine(lay, ys, pos3, wts.T, x, n_post, g2, split)


def _rope_tables(ld):
    rows = ld // GRID_W
    r = jnp.repeat(jnp.arange(rows, dtype=F32), GRID_W)
    col = jnp.tile(jnp.arange(GRID_W, dtype=F32), rows)
    quarter = RET_DK // 4
    inv = ROPE_BASE ** (-jnp.arange(quarter, dtype=F32) / quarter)
    ang = jnp.concatenate([r[:, None] * inv, col[:, None] * inv], axis=-1)
    cos, sin = jnp.cos(ang), jnp.sin(ang)
    c = jnp.concatenate([cos, cos], axis=-1)
    s = jnp.concatenate([-sin, sin], axis=-1)
    ident_c = jnp.ones((TOK, RET_DK), F32)
    ident_s = jnp.zeros((TOK, RET_DK), F32)
    return jnp.concatenate([c, ident_c], axis=0), jnp.concatenate([s, ident_s], axis=0)


def kernel(x_prompt, x_sample, state_ret, state_gdn, c, c_ctx, w_mod, b_mod, norm_mix_pre, norm_mix_post,
           norm_ffn_pre, norm_ffn_post, ev_w_in, ev_conv_w, ev_conv_ln_g, ev_conv_ln_b, ev_ret_decay, ev_w_out,
           od_w_in, od_conv_w, od_a_log, od_dt_bias, od_norm_w, od_w_out, router_w, router_bias,
           moe_w_gate, moe_w_up, moe_w_down):
    bc, lc, d = x_prompt.shape
    bd, ld, _ = x_sample.shape
    depth = w_mod.shape[0]
    lay = _Layout(bc, lc, bd, ld)
    t = lay.t

    x = (x_prompt.reshape(bc * lc, d), x_sample.reshape(bd * ld, d))
    cond = jnp.zeros((COND_PAD, d), F32).at[0].set(c_ctx).at[1:1 + bd].set(c)
    mod = _modulation(cond, w_mod, b_mod)
    mod = mod.reshape(depth, COND_PAD, N_MOD, 1, d).transpose(0, 2, 1, 3, 4)

    rope_c, rope_s = _rope_tables(ld)
    rwt = router_w.T
    ret_states, gdn_states = [], []
    for layer in range(depth):
        sh1, sc1, g1, sh2, sc2, g2 = (mod[layer, j] for j in range(N_MOD))
        i = layer // 2
        if layer % 2 == 0:
            (proj,) = _inproj(lay, x, norm_mix_pre[layer], sc1, sh1, [ev_w_in[i].astype(BF16)], [BF16])
            log_gamma = -jnp.exp(ev_ret_decay[i].astype(F32))
            o_f, o_b, s_out = _retention(lay, proj, log_gamma, rope_c, rope_s, state_ret[:, i])
            ret_states.append(s_out)
            cw = jnp.zeros((32, CONV_CH), F32).at[:CONV_W].set(ev_conv_w[i])
            x = _even_post(lay, proj, o_f, o_b, x, cw, ev_conv_ln_g[i], ev_conv_ln_b[i],
                           ev_w_out[i].astype(BF16), norm_mix_post[layer], g1)
        else:
            w_in = od_w_in[i]
            n_qkv = 2 * GDN_KW + GDN_VW
            n_main = n_qkv + GDN_VW
            w_ab = jnp.zeros((d, LANES), F32).at[:, :4 * GDN_HEADS].set(w_in[:, n_main:])
            cw = jnp.zeros((8, n_qkv), F32).at[:SHORT_W].set(od_conv_w[i])
            alog_row = jnp.zeros((1, LANES), F32).at[0, :2 * GDN_HEADS].set(od_a_log[i].reshape(-1))
            dtb_row = jnp.zeros((1, LANES), F32).at[0, :2 * GDN_HEADS].set(od_dt_bias[i].reshape(-1))
            qkv, z, gb = _odd_in(lay, x, norm_mix_pre[layer], sc1, sh1, w_in[:, :n_qkv].astype(BF16),
                                 w_in[:, n_qkv:n_main].astype(BF16), w_ab.astype(BF16), cw, alog_row, dtb_row)
            o_f, o_b, s_out = _gdn(lay, qkv, gb, state_gdn[:, i])
            gdn_states.append(s_out)
            x = _odd_post(lay, o_f, o_b, z, x, od_norm_w[i], od_w_out[i].astype(BF16),
                          norm_mix_post[layer], g1)
        outs = _moe(lay, x, norm_ffn_pre[layer], sc2, sh2, g2, norm_ffn_post[layer], rwt, router_bias,
                    moe_w_gate, moe_w_up, moe_w_down, layer, split=(layer == depth - 1))
        x = outs[0]

    y_prompt = outs[0].reshape(bc, lc, d)
    y_sample = outs[1].reshape(bd, ld, d)
    new_ret = jnp.stack(ret_states, axis=1)
    new_gdn = jnp.stack(gdn_states, axis=1)
    return y_prompt, y_sample, new_ret.astype(x_prompt.dtype), new_gdn.astype(x_prompt.dtype)
```

```python
import functools
import math

import jax
import jax.numpy as jnp
import numpy as np
from jax import lax
from jax.experimental import pallas as pl
from jax.experimental.pallas import tpu as pltpu

F32 = jnp.float32
BF16 = jnp.bfloat16
I32 = jnp.int32

D_MODEL = 1024
N_MOD = 6
EPS = 1e-6
GRID_W = 64
CONV_CH = 512
CONV_W = 31
RET_HEADS = 4
RET_DK = 128
RET_W = 512
ROPE_BASE = 10000.0
EVEN_IN = 2 * CONV_CH + 4 * RET_W
GDN_HEADS = 8
GDN_DK = 128
GDN_KW = 1024
GDN_VW = 1024
SHORT_W = 5
N_EXPERTS = 16
N_GROUPS = 4
EXP_PER_GROUP = 4
D_EXPERT = 512

LANES = 128
TOK = 256
TM_PROJ = 512
TM_ROUTE = 512
TM_EXP = 512
TM_DISP = 512
TM_COMB = 512
GDN_CHUNK = 128
HALO = 16
COND_PAD = 16
VMEM_LIMIT = 56 * 1024 * 1024
NEG_BIG = -1e30


def _cparams(n_axes=1, vmem=VMEM_LIMIT):
    return pltpu.CompilerParams(dimension_semantics=("arbitrary",) * n_axes, vmem_limit_bytes=vmem)


def _silu(x):
    return x * jax.nn.sigmoid(x)


def _rms(x, g):
    return x * lax.rsqrt(jnp.mean(x * x, axis=-1, keepdims=True) + EPS) * g


def _dot(a, b):
    return jnp.dot(a, b, preferred_element_type=F32)


def _dot_nt(a, b):
    return lax.dot_general(a, b, (((1,), (1,)), ((), ())), preferred_element_type=F32)


def _dot_tn(a, b):
    return lax.dot_general(a, b, (((0,), (0,)), ((), ())), preferred_element_type=F32)


def _split3(x):
    x1 = x.astype(BF16)
    r = x - x1.astype(F32)
    x2 = r.astype(BF16)
    x3 = (r - x2.astype(F32)).astype(BF16)
    return x1, x2, x3


class _Layout:
    def __init__(self, bc, lc, bd, ld):
        self.bc, self.lc, self.bd, self.ld = bc, lc, bd, ld
        self.t_ctx = bc * lc
        self.t = bc * lc + bd * ld
        self.n_seq = bc + bd
        assert lc % TOK == 0 and ld % TOK == 0 and self.t_ctx % TM_PROJ == 0 and ld % TM_PROJ == 0
        assert self.t % TM_ROUTE == 0 and lc % GDN_CHUNK == 0 and ld % GDN_CHUNK == 0

    def seq_of_row(self, r):
        if r < self.t_ctx:
            return r // self.lc, r % self.lc, self.lc
        r2 = r - self.t_ctx
        return self.bc + r2 // self.ld, r2 % self.ld, self.ld

    def cond_rows(self, tile):
        out = []
        for i in range(self.t // tile):
            s, _, _ = self.seq_of_row(i * tile)
            out.append(0 if s < self.bc else 1 + s - self.bc)
        return np.asarray(out, np.int32)

    def edges(self, tile):
        left, right = [], []
        for i in range(self.t // tile):
            _, p, l = self.seq_of_row(i * tile)
            left.append(int(p == 0))
            right.append(int(p + tile == l))
        return np.asarray(left, np.int32), np.asarray(right, np.int32)

    def scan_schedule(self, chunk):
        fb, bb, sq, fi, la, rf, rb = [], [], [], [], [], [], []
        ident = self.ld // chunk
        for s in range(self.n_seq):
            if s < self.bc:
                base, n = s * self.lc // chunk, self.lc // chunk
            else:
                base, n = (self.t_ctx + (s - self.bc) * self.ld) // chunk, self.ld // chunk
            for c in range(n):
                fb.append(base + c)
                bb.append(base + n - 1 - c)
                sq.append(s)
                fi.append(int(c == 0))
                la.append(int(c == n - 1))
                rf.append(ident if s < self.bc else c)
                rb.append(ident if s < self.bc else n - 1 - c)
        return [np.asarray(a, np.int32) for a in (fb, bb, sq, fi, la, rf, rb)]


def _mod_kernel(c_ref, w_ref, b_ref, o_ref):
    s = _silu(c_ref[...])
    o_ref[0] = _dot(s.astype(BF16), w_ref[0].astype(BF16)) + b_ref[0]


def _modulation(cond, w_mod, b_mod):
    depth, d, n = w_mod.shape
    nt = n // d
    return pl.pallas_call(
        _mod_kernel,
        grid=(depth, nt),
        in_specs=[pl.BlockSpec((COND_PAD, d), lambda l, j: (0, 0)),
                  pl.BlockSpec((1, d, d), lambda l, j: (l, 0, j)),
                  pl.BlockSpec((1, 1, d), lambda l, j: (l, 0, j))],
        out_specs=pl.BlockSpec((1, COND_PAD, d), lambda l, j: (l, 0, j)),
        out_shape=jax.ShapeDtypeStruct((depth, COND_PAD, n), F32),
        compiler_params=_cparams(2),
        name="modulation",
    )(cond, w_mod, b_mod.reshape(depth, 1, n))


def _two_source(x, tile):
    if isinstance(x, (tuple, list)):
        xa, xb = x
        n_first = xa.shape[0] // tile
    else:
        xa = xb = x
        n_first = x.shape[0] // tile
    d = xa.shape[1]
    specs = [pl.BlockSpec((tile, d), lambda i, *m: (jnp.minimum(i, n_first - 1), 0)),
             pl.BlockSpec((tile, d), lambda i, *m: (jnp.maximum(i - n_first, 0), 0))]
    return xa, xb, n_first, specs


def _read_two_source(xa_ref, xb_ref, n_first):
    return jnp.where(pl.program_id(0) < n_first, xa_ref[...], xb_ref[...])


def _inproj_kernel(cr_ref, xa_ref, xb_ref, g_ref, sc_ref, sh_ref, *refs, n_first):
    n = len(refs) // 2
    x = _read_two_source(xa_ref, xb_ref, n_first)
    h = _rms(x, g_ref[...]) * (1.0 + sc_ref[0]) + sh_ref[0]
    hb = h.astype(BF16)
    for w_ref, o_ref in zip(refs[:n], refs[n:]):
        o_ref[...] = _dot(hb, w_ref[...]).astype(o_ref.dtype)


def _inproj(lay, x, g, sc, sh, weights, out_dtypes):
    t, d = lay.t, g.shape[0]
    cr = lay.cond_rows(TM_PROJ)
    xa, xb, n_first, x_specs = _two_source(x, TM_PROJ)
    row = lambda i, cr: (i, 0)
    const = lambda i, cr: (0, 0)
    cond = lambda i, cr: (cr[i], 0, 0)
    in_specs = x_specs + [pl.BlockSpec((1, d), const), pl.BlockSpec((1, 1, d), cond), pl.BlockSpec((1, 1, d), cond)]
    in_specs += [pl.BlockSpec(w.shape, const) for w in weights]
    out_specs = [pl.BlockSpec((TM_PROJ, w.shape[1]), row) for w in weights]
    out_shape = [jax.ShapeDtypeStruct((t, w.shape[1]), dt) for w, dt in zip(weights, out_dtypes)]
    return pl.pallas_call(
        functools.partial(_inproj_kernel, n_first=n_first),
        grid_spec=pltpu.PrefetchScalarGridSpec(num_scalar_prefetch=1, grid=(t // TM_PROJ,),
                                               in_specs=in_specs, out_specs=out_specs),
        out_shape=out_shape,
        compiler_params=_cparams(1),
        name="inproj",
    )(cr, xa, xb, g.reshape(1, d), sc, sh, *weights)


def _rope(x, c, s):
    return x * c + pltpu.roll(x, RET_DK // 2, 1) * s


def _ret_kernel(fb_ref, bb_ref, sq_ref, fi_ref, la_ref, rf_ref, rb_ref,
                lg_ref, qf_ref, kf_ref, vf_ref, qb_ref, kb_ref, vb_ref,
                cf_ref, sf_ref, cb_ref, sb_ref, s0_ref,
                of_ref, ob_ref, sout_ref,
                st_ref, dm_ref, dec_ref, *, n_ctx):
    g = pl.program_id(0)
    c = TOK

    @pl.when(g == 0)
    def _():
        ii = lax.broadcasted_iota(I32, (c, c), 0)
        jj = lax.broadcasted_iota(I32, (c, c), 1)
        diff = (ii - jj).astype(F32)
        ri = lax.broadcasted_iota(I32, (c, RET_DK), 0).astype(F32)
        for h in range(RET_HEADS):
            lf = lg_ref[0, h]
            lb = lg_ref[1, h]
            low = jnp.exp(lf * jnp.maximum(diff, 0.0))
            up = jnp.exp(lb * jnp.maximum(-diff, 0.0))
            dm_ref[h] = jnp.where(diff > 0, low, jnp.where(diff < 0, up, 2.0))
            dec_ref[0, h] = jnp.exp(lf * (ri + 1.0))
            dec_ref[1, h] = jnp.exp(lf * (c - 1.0 - ri))
            dec_ref[2, h] = jnp.exp(lb * (c - ri))
            dec_ref[3, h] = jnp.exp(lb * ri)

    @pl.when((fi_ref[g] == 1) & (sq_ref[g] < n_ctx))
    def _():
        st_ref[...] = jnp.zeros_like(st_ref)

    @pl.when((fi_ref[g] == 1) & (sq_ref[g] >= n_ctx))
    def _():
        st_ref[...] = s0_ref[0]

    scale = RET_DK ** -0.5
    cf, sf, cb, sb = cf_ref[...], sf_ref[...], cb_ref[...], sb_ref[...]
    zero_row = jnp.zeros((1, RET_DK), F32)
    for h in range(RET_HEADS):
        sl = slice(h * RET_DK, (h + 1) * RET_DK)
        lf = lg_ref[0, h]
        lb = lg_ref[1, h]
        q = _rope(qf_ref[:, sl].astype(F32), cf, sf)
        k = _rope(kf_ref[:, sl].astype(F32), cf, sf) * scale
        v = vf_ref[:, sl]
        s = _dot_nt(q.astype(BF16), k.astype(BF16)) * dm_ref[h]
        st = st_ref[0, h]
        o = _dot(s.astype(BF16), v) + _dot((q * dec_ref[0, h]).astype(BF16), st.astype(BF16))
        of_ref[:, sl] = o
        st_ref[0, h] = jnp.exp(zero_row + lf * c) * st + _dot_tn((k * dec_ref[1, h]).astype(BF16), v)
        q = _rope(qb_ref[:, sl].astype(F32), cb, sb)
        k = _rope(kb_ref[:, sl].astype(F32), cb, sb) * scale
        v = vb_ref[:, sl]
        st = st_ref[1, h]
        ob_ref[:, sl] = _dot((q * dec_ref[2, h]).astype(BF16), st.astype(BF16))
        st_ref[1, h] = jnp.exp(zero_row + lb * c) * st + _dot_tn((k * dec_ref[3, h]).astype(BF16), v)

    @pl.when((la_ref[g] == 1) & (sq_ref[g] < n_ctx))
    def _():
        sout_ref[0] = st_ref[...]


def _retention(lay, proj, log_gamma, rope_c, rope_s, s0):
    t = proj.shape[0]
    sched = lay.scan_schedule(TOK)
    qcol, kcol, vcol = (2 * CONV_CH) // RET_W, (2 * CONV_CH) // RET_W + 1, (2 * CONV_CH) // RET_W + 2

    def tok(which, col):
        return pl.BlockSpec((TOK, RET_W), lambda g, *m: (m[which][g], col))

    def rope(which):
        return pl.BlockSpec((TOK, RET_DK), lambda g, *m: (m[which][g], 0))

    n_ctx = lay.bc
    sblk = (1, 2, RET_HEADS, RET_DK, RET_DK)
    state_in = pl.BlockSpec(sblk, lambda g, *m: (jnp.maximum(m[2][g] - n_ctx, 0), 0, 0, 0, 0))
    state_out = pl.BlockSpec(sblk, lambda g, *m: (jnp.minimum(m[2][g], n_ctx - 1), 0, 0, 0, 0))
    in_specs = [pl.BlockSpec(memory_space=pltpu.SMEM),
                tok(0, qcol), tok(0, kcol), tok(0, vcol), tok(1, qcol), tok(1, kcol), tok(1, vcol),
                rope(5), rope(5), rope(6), rope(6), state_in]
    out_specs = [pl.BlockSpec((TOK, RET_W), lambda g, *m: (m[0][g], 0)),
                 pl.BlockSpec((TOK, RET_W), lambda g, *m: (m[1][g], 0)), state_out]
    out_shape = [jax.ShapeDtypeStruct((t, RET_W), F32), jax.ShapeDtypeStruct((t, RET_W), F32),
                 jax.ShapeDtypeStruct((n_ctx,) + s0.shape[1:], F32)]
    scratch = [pltpu.VMEM((2, RET_HEADS, RET_DK, RET_DK), F32),
               pltpu.VMEM((RET_HEADS, TOK, TOK), F32),
               pltpu.VMEM((4, RET_HEADS, TOK, RET_DK), F32)]
    return pl.pallas_call(
        functools.partial(_ret_kernel, n_ctx=n_ctx),
        grid_spec=pltpu.PrefetchScalarGridSpec(num_scalar_prefetch=7, grid=(len(sched[0]),),
                                               in_specs=in_specs, out_specs=out_specs,
                                               scratch_shapes=scratch),
        out_shape=out_shape,
        compiler_params=_cparams(1),
        name="retention",
    )(*sched, log_gamma, proj, proj, proj, proj, proj, proj, rope_c, rope_s, rope_c, rope_s, s0)


def _even_post_kernel(cr_ref, le_ref, re_ref,
                      glu_ref, prev_ref, next_ref, gt_ref, of_ref, ob_ref, xa_ref, xb_ref,
                      cw_ref, lng_ref, lnb_ref, wout_ref, npost_ref, g1_ref,
                      out_ref, buf_ref, *, n_first):
    i = pl.program_id(0)

    def glu(r):
        r = r.astype(F32)
        return r[:, :CONV_CH] * jax.nn.sigmoid(r[:, CONV_CH:])

    keep_l = jnp.where(le_ref[i] == 1, 0.0, 1.0)
    keep_r = jnp.where(re_ref[i] == 1, 0.0, 1.0)
    buf_ref[0:HALO, :] = glu(prev_ref[...]) * keep_l
    buf_ref[HALO:HALO + TOK, :] = glu(glu_ref[...])
    buf_ref[HALO + TOK:, :] = glu(next_ref[...]) * keep_r
    off = HALO - CONV_W // 2
    sub = 8
    cols = []
    for cb in range(CONV_CH // LANES):
        ls = slice(cb * LANES, (cb + 1) * LANES)
        acc = jnp.zeros((TOK, LANES), F32)
        for r in range(sub):
            part = None
            for m in range((off + CONV_W - 1) // sub + 1):
                j = sub * m + r - off
                if 0 <= j < CONV_W:
                    term = buf_ref[sub * m:sub * m + TOK + sub, ls] * cw_ref[j:j + 1, ls]
                    part = term if part is None else part + term
            acc = acc + part[r:r + TOK, :]
        cols.append(acc)
    acc = jnp.concatenate(cols, axis=1)
    mu = jnp.mean(acc, axis=-1, keepdims=True)
    ac = acc - mu
    y = ac * lax.rsqrt(jnp.mean(ac * ac, axis=-1, keepdims=True) + EPS) * lng_ref[...] + lnb_ref[...]
    conv_out = _silu(y)

    o = of_ref[...] + ob_ref[...]
    gt = gt_ref[...].astype(F32)
    parts = []
    for h in range(RET_HEADS):
        sl = slice(h * RET_DK, (h + 1) * RET_DK)
        oh = o[:, sl]
        oc = oh - jnp.mean(oh, axis=-1, keepdims=True)
        on = oc * lax.rsqrt(jnp.mean(oc * oc, axis=-1, keepdims=True) + EPS)
        parts.append(on * _silu(gt[:, sl]))
    ret_out = jnp.concatenate(parts, axis=1)

    out = _dot(conv_out.astype(BF16), wout_ref[0:CONV_CH, :]) + _dot(ret_out.astype(BF16), wout_ref[CONV_CH:, :])
    out_ref[...] = _read_two_source(xa_ref, xb_ref, n_first) + g1_ref[0] * _rms(out, npost_ref[...])


def _even_post(lay, proj, o_f, o_b, x, conv_w, ln_g, ln_b, w_out, n_post, g1):
    t, d = lay.t, n_post.shape[0]
    cr = lay.cond_rows(TOK)
    le, re = lay.edges(TOK)
    xa, xb, n_first, x_specs = _two_source(x, TOK)
    hb = TOK // HALO
    n_halo = t // HALO
    row = lambda i, *m: (i, 0)
    const = lambda i, *m: (0, 0)
    in_specs = [pl.BlockSpec((TOK, 2 * CONV_CH), row),
                pl.BlockSpec((HALO, 2 * CONV_CH), lambda i, *m: (jnp.maximum(i * hb - 1, 0), 0)),
                pl.BlockSpec((HALO, 2 * CONV_CH), lambda i, *m: (jnp.minimum((i + 1) * hb, n_halo - 1), 0)),
                pl.BlockSpec((TOK, RET_W), lambda i, *m: (i, EVEN_IN // RET_W - 1)),
                pl.BlockSpec((TOK, RET_W), row), pl.BlockSpec((TOK, RET_W), row)] + x_specs + [
                pl.BlockSpec(conv_w.shape, const), pl.BlockSpec((1, CONV_CH), const), pl.BlockSpec((1, CONV_CH), const),
                pl.BlockSpec(w_out.shape, const), pl.BlockSpec((1, d), const),
                pl.BlockSpec((1, 1, d), lambda i, *m: (m[0][i], 0, 0))]
    return pl.pallas_call(
        functools.partial(_even_post_kernel, n_first=n_first),
        grid_spec=pltpu.PrefetchScalarGridSpec(num_scalar_prefetch=3, grid=(t // TOK,),
                                               in_specs=in_specs, out_specs=pl.BlockSpec((TOK, d), row),
                                               scratch_shapes=[pltpu.VMEM((TOK + 2 * HALO, CONV_CH), F32)]),
        out_shape=jax.ShapeDtypeStruct((t, d), F32),
        compiler_params=_cparams(1),
        name="even_post",
    )(cr, le, re, proj, proj, proj, proj, o_f, o_b, xa, xb, conv_w, ln_g.reshape(1, -1), ln_b.reshape(1, -1),
      w_out, n_post.reshape(1, d), g1)


XHALO = 8
QKV_BLOCK = 256


def _odd_in_kernel(cr_ref, le_ref, re_ref, x_ref, xp_ref, xn_ref, g_ref, sc_ref, sh_ref,
                   wqkv_ref, wz_ref, wab_ref, cw_ref, alog_ref, dtb_ref,
                   qkv_ref, z_ref, gb_ref):
    i = pl.program_id(0)
    n = TOK
    keep_l = jnp.where(le_ref[i] == 1, 0.0, 1.0)
    keep_r = jnp.where(re_ref[i] == 1, 0.0, 1.0)
    xs = jnp.concatenate([xp_ref[...], x_ref[...], xn_ref[...]], axis=0)
    h = _rms(xs, g_ref[...]) * (1.0 + sc_ref[0]) + sh_ref[0]
    rows = lax.broadcasted_iota(I32, (n + 2 * XHALO, 1), 0)
    keep = jnp.where(rows < XHALO, keep_l, jnp.where(rows >= n + XHALO, keep_r, 1.0))
    hb = (h * keep).astype(BF16)
    off = XHALO - SHORT_W // 2
    heads_per_block = QKV_BLOCK // GDN_DK
    for blk in range((2 * GDN_KW + GDN_VW) // QKV_BLOCK):
        cs = slice(blk * QKV_BLOCK, (blk + 1) * QKV_BLOCK)
        p = _dot(hb, wqkv_ref[:, cs])
        acc = jnp.zeros((n, QKV_BLOCK), F32)
        for j in range(SHORT_W):
            acc = acc + p[off + j:off + j + n, :] * cw_ref[j:j + 1, cs]
        y = _silu(acc)
        for hh in range(heads_per_block):
            head = blk * heads_per_block + hh
            yh = y[:, hh * GDN_DK:(hh + 1) * GDN_DK]
            if head < 2 * GDN_HEADS:
                yh = yh * lax.rsqrt(jnp.sum(yh * yh, axis=-1, keepdims=True) + EPS)
                if head < GDN_HEADS:
                    yh = yh * (GDN_DK ** -0.5)
            qkv_ref[:, head * GDN_DK:(head + 1) * GDN_DK] = yh.astype(qkv_ref.dtype)
    hc = hb[XHALO:XHALO + n, :]
    z_ref[...] = _dot(hc, wz_ref[...]).astype(z_ref.dtype)
    ab = _dot(hc, wab_ref[...])
    zz = ab + dtb_ref[...]
    softplus = jnp.maximum(zz, 0.0) + jnp.log(1.0 + jnp.exp(-jnp.abs(zz)))
    gate = -jnp.exp(alog_ref[...]) * softplus
    beta = jax.nn.sigmoid(ab)
    lane = lax.broadcasted_iota(I32, ab.shape, 1)
    gb_ref[...] = jnp.where(lane < 2 * GDN_HEADS, gate, beta)


def _odd_in(lay, x, g, sc, sh, w_qkv, w_z, w_ab, conv_w, alog_row, dtb_row):
    t, d = x.shape
    w = w_qkv.shape[1]
    cr = lay.cond_rows(TOK)
    le, re = lay.edges(TOK)
    hb = TOK // XHALO
    n_halo = t // XHALO
    row = lambda i, *m: (i, 0)
    const = lambda i, *m: (0, 0)
    cond = lambda i, *m: (m[0][i], 0, 0)
    in_specs = [pl.BlockSpec((TOK, d), row),
                pl.BlockSpec((XHALO, d), lambda i, *m: (jnp.maximum(i * hb - 1, 0), 0)),
                pl.BlockSpec((XHALO, d), lambda i, *m: (jnp.minimum((i + 1) * hb, n_halo - 1), 0)),
                pl.BlockSpec((1, d), const), pl.BlockSpec((1, 1, d), cond), pl.BlockSpec((1, 1, d), cond),
                pl.BlockSpec(w_qkv.shape, const), pl.BlockSpec(w_z.shape, const), pl.BlockSpec(w_ab.shape, const),
                pl.BlockSpec(conv_w.shape, const), pl.BlockSpec((1, LANES), const), pl.BlockSpec((1, LANES), const)]
    out_specs = [pl.BlockSpec((TOK, w), row), pl.BlockSpec((TOK, w_z.shape[1]), row), pl.BlockSpec((TOK, LANES), row)]
    out_shape = [jax.ShapeDtypeStruct((t, w), BF16), jax.ShapeDtypeStruct((t, w_z.shape[1]), BF16),
                 jax.ShapeDtypeStruct((t, LANES), F32)]
    return pl.pallas_call(
        _odd_in_kernel,
        grid_spec=pltpu.PrefetchScalarGridSpec(num_scalar_prefetch=3, grid=(t // TOK,),
                                               in_specs=in_specs, out_specs=out_specs),
        out_shape=out_shape,
        compiler_params=_cparams(1),
        name="odd_in",
    )(cr, le, re, x, x, x, g.reshape(1, d), sc, sh, w_qkv, w_z, w_ab, conv_w, alog_row, dtb_row)


_GDN_LEVELS = tuple(2 ** p for p in range(int(math.log2(GDN_CHUNK))))
GDN_GROUPS = ((0, 1),)


def _gdn_masks(msk_ref, tri_ref):
    c = GDN_CHUNK
    ii = lax.broadcasted_iota(I32, (c, c), 0)
    jj = lax.broadcasted_iota(I32, (c, c), 1)
    one = jnp.ones((c, c), F32)
    zero = jnp.zeros((c, c), F32)
    for rev in (0, 1):
        a, b = (ii, jj) if rev == 0 else (jj, ii)
        base = rev * 9
        msk_ref[base + 0] = jnp.where(a >= b, one, zero)
        msk_ref[base + 1] = jnp.where(a > b, one, zero)
        for l, m in enumerate(_GDN_LEVELS):
            sh = int(math.log2(m))
            ab_, bb_ = a >> sh, b >> sh
            hit = ((ab_ & 1) == 1) & (bb_ == ab_ - 1)
            msk_ref[base + 2 + l] = jnp.where(hit, one, zero)
        tri_ref[rev] = jnp.where(a >= b, one, zero).astype(BF16)


def _gdn_chunk_step(dirs, st_ref, msk_ref, tri_ref, revs):
    c = GDN_CHUNK
    nh = GDN_HEADS
    probs = [(rev, h) for rev in revs for h in range(nh)]
    ones = jnp.ones((c, c), BF16)
    gcum, gcum_t, e_all, kdec_all, elast_all, gbs = {}, {}, {}, {}, {}, {}
    for rev in revs:
        gb = dirs[rev][3][...]
        g1, g2, g3 = _split3(gb)
        tri = tri_ref[rev]
        gc = _dot(tri, g1) + _dot(tri, g2) + _dot(tri, g3)
        gl = _dot(ones, g1) + _dot(ones, g2) + _dot(ones, g3)
        gbs[rev] = gb
        gcum[rev] = gc
        gcum_t[rev] = gc.T
        e_all[rev] = jnp.exp(gc)
        kdec_all[rev] = jnp.exp(gl - gc)
        elast_all[rev] = jnp.exp(gl)

    def col(rev, h):
        return rev * nh + h

    def hsl(h):
        return slice(h * GDN_DK, (h + 1) * GDN_DK)

    st = [st_ref[rev, h] for rev, h in probs]

    a, attn, kbs = [], [], []
    for rev, h in probs:
        base = rev * 9
        p = col(rev, h)
        q_ref, k_ref = dirs[rev][0], dirs[rev][1]
        k = k_ref[:, hsl(h)]
        gcb = jnp.broadcast_to(gcum[rev][:, p:p + 1], (c, c))
        grb = jnp.broadcast_to(gcum_t[rev][p:p + 1, :], (c, c))
        decay = jnp.exp(jnp.where(msk_ref[base] > 0, gcb - grb, NEG_BIG))
        kb = k.astype(F32) * gbs[rev][:, 2 * nh + p:2 * nh + p + 1]
        kbs.append(kb)
        a.append(_dot_nt(kb.astype(BF16), k) * decay * msk_ref[base + 1])
        attn.append((_dot_nt(q_ref[:, hsl(h)], k) * decay).astype(BF16))

    tinv = []
    for i, (rev, h) in enumerate(probs):
        base = rev * 9
        tinv.append(msk_ref[base] - msk_ref[base + 1] - a[i] * msk_ref[base + 2])
    for l in range(1, len(_GDN_LEVELS)):
        bt = []
        for i, (rev, h) in enumerate(probs):
            bt.append(_dot((a[i] * msk_ref[rev * 9 + 2 + l]).astype(BF16), tinv[i].astype(BF16)))
        for i in range(len(probs)):
            tinv[i] = tinv[i] - _dot(tinv[i].astype(BF16), bt[i].astype(BF16))

    sol = []
    for i, (rev, h) in enumerate(probs):
        p = col(rev, h)
        v = dirs[rev][2][:, hsl(h)].astype(F32)
        beta = gbs[rev][:, 2 * nh + p:2 * nh + p + 1]
        rhs = jnp.concatenate([v * beta, kbs[i] * e_all[rev][:, p:p + 1]], axis=1)
        sol.append(_dot(tinv[i].astype(BF16), rhs.astype(BF16)))
    v_new = []
    for i in range(len(probs)):
        u, w = sol[i][:, :GDN_DK], sol[i][:, GDN_DK:]
        v_new.append((u - _dot(w.astype(BF16), st[i].astype(BF16))).astype(BF16))
    for i, (rev, h) in enumerate(probs):
        p = col(rev, h)
        q = dirs[rev][0][:, hsl(h)].astype(F32)
        qs = (q * e_all[rev][:, p:p + 1]).astype(BF16)
        dirs[rev][4][:, hsl(h)] = (_dot(qs, st[i].astype(BF16)) + _dot(attn[i], v_new[i])).astype(BF16)
    new_st = []
    for i, (rev, h) in enumerate(probs):
        p = col(rev, h)
        kd = (dirs[rev][1][:, hsl(h)].astype(F32) * kdec_all[rev][:, p:p + 1]).astype(BF16)
        new_st.append(st[i] * elast_all[rev][:, p:p + 1] + _dot_tn(kd, v_new[i]))
    for i, (rev, h) in enumerate(probs):
        st_ref[rev, h] = new_st[i]


def _gdn_kernel(fb_ref, bb_ref, sq_ref, fi_ref, la_ref,
                qf_ref, kf_ref, vf_ref, gf_ref, qb_ref, kb_ref, vb_ref, gbk_ref, s0_ref,
                of_ref, ob_ref, sout_ref,
                st_ref, msk_ref, tri_ref, *, n_ctx):
    g = pl.program_id(0)

    @pl.when(g == 0)
    def _():
        _gdn_masks(msk_ref, tri_ref)

    @pl.when((fi_ref[g] == 1) & (sq_ref[g] < n_ctx))
    def _():
        st_ref[...] = jnp.zeros_like(st_ref)

    @pl.when((fi_ref[g] == 1) & (sq_ref[g] >= n_ctx))
    def _():
        st_ref[...] = s0_ref[0]

    dirs = ((qf_ref, kf_ref, vf_ref, gf_ref, of_ref), (qb_ref, kb_ref, vb_ref, gbk_ref, ob_ref))
    for revs in GDN_GROUPS:
        _gdn_chunk_step(dirs, st_ref, msk_ref, tri_ref, revs)

    @pl.when((la_ref[g] == 1) & (sq_ref[g] < n_ctx))
    def _():
        sout_ref[0] = st_ref[...]


def _gdn(lay, qkv, gb, s0):
    t = qkv.shape[0]
    c = GDN_CHUNK
    sched = lay.scan_schedule(c)[:5]

    def tok(which, col):
        return pl.BlockSpec((c, GDN_KW), lambda g, *m: (m[which][g], col))

    def gate(which):
        return pl.BlockSpec((c, LANES), lambda g, *m: (m[which][g], 0))

    n_ctx = lay.bc
    sblk = (1, 2, GDN_HEADS, GDN_DK, GDN_DK)
    state_in = pl.BlockSpec(sblk, lambda g, *m: (jnp.maximum(m[2][g] - n_ctx, 0), 0, 0, 0, 0))
    state_out = pl.BlockSpec(sblk, lambda g, *m: (jnp.minimum(m[2][g], n_ctx - 1), 0, 0, 0, 0))
    in_specs = [tok(0, 0), tok(0, 1), tok(0, 2), gate(0), tok(1, 0), tok(1, 1), tok(1, 2), gate(1), state_in]
    out_specs = [pl.BlockSpec((c, GDN_VW), lambda g, *m: (m[0][g], 0)),
                 pl.BlockSpec((c, GDN_VW), lambda g, *m: (m[1][g], 0)), state_out]
    out_shape = [jax.ShapeDtypeStruct((t, GDN_VW), BF16), jax.ShapeDtypeStruct((t, GDN_VW), BF16),
                 jax.ShapeDtypeStruct((n_ctx,) + s0.shape[1:], F32)]
    scratch = [pltpu.VMEM((2, GDN_HEADS, GDN_DK, GDN_DK), F32),
               pltpu.VMEM((18, c, c), F32), pltpu.VMEM((2, c, c), BF16)]
    return pl.pallas_call(
        functools.partial(_gdn_kernel, n_ctx=n_ctx),
        grid_spec=pltpu.PrefetchScalarGridSpec(num_scalar_prefetch=5, grid=(len(sched[0]),),
                                               in_specs=in_specs, out_specs=out_specs,
                                               scratch_shapes=scratch),
        out_shape=out_shape,
        compiler_params=_cparams(1),
        name="gdn_scan",
    )(*sched, qkv, qkv, qkv, gb, qkv, qkv, qkv, gb, s0)


def _odd_post_kernel(cr_ref, of_ref, ob_ref, z_ref, x_ref, nw_ref, wout_ref, npost_ref, g1_ref, out_ref):
    o = of_ref[...].astype(F32) + ob_ref[...].astype(F32)
    z = z_ref[...].astype(F32)
    nw = nw_ref[...]
    parts = []
    for h in range(GDN_HEADS):
        sl = slice(h * GDN_DK, (h + 1) * GDN_DK)
        parts.append(_rms(o[:, sl], nw) * _silu(z[:, sl]))
    y = jnp.concatenate(parts, axis=1)
    out = _dot(y.astype(BF16), wout_ref[...])
    out_ref[...] = x_ref[...] + g1_ref[0] * _rms(out, npost_ref[...])


def _odd_post(lay, o_f, o_b, proj, x, norm_w, w_out, n_post, g1):
    t, d = x.shape
    cr = lay.cond_rows(TOK)
    row = lambda i, *m: (i, 0)
    const = lambda i, *m: (0, 0)
    in_specs = [pl.BlockSpec((TOK, GDN_VW), row), pl.BlockSpec((TOK, GDN_VW), row),
                pl.BlockSpec((TOK, GDN_VW), row),
                pl.BlockSpec((TOK, d), row),
                pl.BlockSpec((1, GDN_DK), const), pl.BlockSpec(w_out.shape, const), pl.BlockSpec((1, d), const),
                pl.BlockSpec((1, 1, d), lambda i, *m: (m[0][i], 0, 0))]
    return pl.pallas_call(
        _odd_post_kernel,
        grid_spec=pltpu.PrefetchScalarGridSpec(num_scalar_prefetch=1, grid=(t // TOK,),
                                               in_specs=in_specs, out_specs=pl.BlockSpec((TOK, d), row)),
        out_shape=jax.ShapeDtypeStruct((t, d), F32),
        compiler_params=_cparams(1),
        name="odd_post",
    )(cr, o_f, o_b, proj, x, norm_w.reshape(1, -1), w_out, n_post.reshape(1, d), g1)


U32 = jnp.uint32
PSUB = D_MODEL // (2 * LANES)
HI_MASK = 0xFFFF0000
N_PAIRS = EXP_PER_GROUP * (EXP_PER_GROUP - 1) // 2
N_CLASSES = N_GROUPS * N_PAIRS
CLS_PAD = 32
PAIR_LO = (0, 0, 0, 1, 1, 2)
PAIR_HI = (1, 2, 3, 2, 3, 3)


def _pack_pairs(x):
    half = x.shape[1] // 2
    lo = lax.bitcast_convert_type(x[:, :half].astype(BF16).astype(F32), U32) >> 16
    hi = lax.bitcast_convert_type(x[:, half:].astype(BF16).astype(F32), U32) & jnp.uint32(HI_MASK)
    return lo | hi


def _unpack_pairs(w):
    lo = lax.bitcast_convert_type(w << 16, F32)
    hi = lax.bitcast_convert_type(w & jnp.uint32(HI_MASK), F32)
    return jnp.concatenate([lo, hi], axis=1)


def _store_token_major(ref, w):
    n = w.shape[0]
    for s in range(PSUB):
        ref[pl.ds(s, n, stride=PSUB), :] = w[:, s * LANES:(s + 1) * LANES]


def _load_token_major(ref, n):
    return jnp.concatenate([ref[pl.ds(s, n, stride=PSUB), :] for s in range(PSUB)], axis=1)


def _router_kernel(cr_ref, x_ref, g_ref, sc_ref, sh_ref, rwt_ref, rb_ref,
                   h_ref, route_ref, wts_ref, cnt_ref,
                   base_ref, su_ref):
    i = pl.program_id(0)
    tm = TM_ROUTE

    @pl.when(i == 0)
    def _():
        base_ref[...] = jnp.zeros_like(base_ref)
        ii = lax.broadcasted_iota(I32, (tm, tm), 0)
        jj = lax.broadcasted_iota(I32, (tm, tm), 1)
        su_ref[...] = jnp.where(ii < jj, 1.0, 0.0).astype(BF16)

    h = _rms(x_ref[...], g_ref[...]) * (1.0 + sc_ref[0]) + sh_ref[0]
    hb = h.astype(BF16)
    _store_token_major(h_ref, _pack_pairs(h))
    h1 = hb
    w1, w2, w3 = _split3(rwt_ref[...])
    logits = _dot_nt(w1, h1) + _dot_nt(w2, h1) + _dot_nt(w3, h1)
    score = jax.nn.sigmoid(logits)
    sel = score + rb_ref[...]

    def row(a, e):
        return a[e:e + 1, :]

    gsum = []
    for gi in range(N_GROUPS):
        a, b, c, d = (row(sel, gi * EXP_PER_GROUP + j) for j in range(EXP_PER_GROUP))
        hi1, lo1 = jnp.maximum(a, b), jnp.minimum(a, b)
        hi2, lo2 = jnp.maximum(c, d), jnp.minimum(c, d)
        gsum.append(jnp.maximum(hi1, hi2) + jnp.maximum(jnp.minimum(hi1, hi2), jnp.maximum(lo1, lo2)))
    best = jnp.zeros_like(gsum[0]).astype(I32)
    cur = gsum[0]
    for gi in range(1, N_GROUPS):
        upd = gsum[gi] > cur
        best = jnp.where(upd, gi, best)
        cur = jnp.where(upd, gsum[gi], cur)

    def pick(arr, j):
        out = row(arr, j)
        for gi in range(1, N_GROUPS):
            out = jnp.where(best == gi, row(arr, gi * EXP_PER_GROUP + j), out)
        return out

    vals = [pick(sel, j) for j in range(EXP_PER_GROUP)]
    scs = [pick(score, j) for j in range(EXP_PER_GROUP)]

    def argmax_first(vs):
        idx = jnp.zeros_like(best)
        m = vs[0]
        for j in range(1, EXP_PER_GROUP):
            upd = vs[j] > m
            idx = jnp.where(upd, j, idx)
            m = jnp.where(upd, vs[j], m)
        return idx

    i1 = argmax_first(vals)
    vals2 = [jnp.where(i1 == j, -jnp.inf, vals[j]) for j in range(EXP_PER_GROUP)]
    i2 = argmax_first(vals2)

    def take(vs, idx):
        out = vs[0]
        for j in range(1, EXP_PER_GROUP):
            out = jnp.where(idx == j, vs[j], out)
        return out

    s1, s2 = take(scs, i1), take(scs, i2)
    tot = s1 + s2
    first_low = i1 < i2
    lo = jnp.minimum(i1, i2)
    hi = jnp.maximum(i1, i2)
    pair = jnp.where(lo == 0, hi - 1, jnp.where(lo == 1, hi + 1, N_PAIRS - 1))
    cls = best * N_PAIRS + pair
    wts_ref[0:1, :] = jnp.where(first_low, s1, s2) / tot
    wts_ref[1:2, :] = jnp.where(first_low, s2, s1) / tot

    crow = lax.broadcasted_iota(I32, (CLS_PAD, tm), 0)
    onehot = jnp.where(crow == cls, 1.0, 0.0)
    before = _dot(onehot.astype(BF16), su_ref[...]) + base_ref[...]
    route_ref[0:1, :] = cls
    route_ref[1:2, :] = jnp.sum(onehot * before, axis=0, keepdims=True).astype(I32)
    base_ref[...] = base_ref[...] + jnp.sum(onehot, axis=1, keepdims=True)
    cnt_ref[...] = jnp.broadcast_to(base_ref[...], cnt_ref.shape).astype(I32)


def _router(lay, x, g, sc, sh, rwt, rbias):
    t, d = x.shape
    cr = lay.cond_rows(TM_ROUTE)
    tm = TM_ROUTE
    row = lambda i, cr: (i, 0)
    col = lambda i, cr: (0, i)
    const = lambda i, cr: (0, 0)
    cond = lambda i, cr: (cr[i], 0, 0)
    in_specs = [pl.BlockSpec((tm, d), row), pl.BlockSpec((1, d), const),
                pl.BlockSpec((1, 1, d), cond), pl.BlockSpec((1, 1, d), cond),
                pl.BlockSpec((N_EXPERTS, d), const), pl.BlockSpec((N_EXPERTS, 1), const)]
    out_specs = [pl.BlockSpec((tm * PSUB, LANES), row), pl.BlockSpec((2, tm), col), pl.BlockSpec((2, tm), col),
                 pl.BlockSpec((CLS_PAD, LANES), const)]
    out_shape = [jax.ShapeDtypeStruct((t * PSUB, LANES), U32), jax.ShapeDtypeStruct((2, t), I32),
                 jax.ShapeDtypeStruct((2, t), F32), jax.ShapeDtypeStruct((CLS_PAD, LANES), I32)]
    return pl.pallas_call(
        _router_kernel,
        grid_spec=pltpu.PrefetchScalarGridSpec(num_scalar_prefetch=1, grid=(t // tm,),
                                               in_specs=in_specs, out_specs=out_specs,
                                               scratch_shapes=[pltpu.VMEM((CLS_PAD, 1), F32),
                                                               pltpu.VMEM((tm, tm), BF16)]),
        out_shape=out_shape,
        compiler_params=_cparams(1),
        name="router",
    )(cr, x, g.reshape(1, d), sc, sh, rwt, rbias.reshape(N_EXPERTS, 1))


def _row_copy(src, dst, sem):
    return pltpu.make_async_copy(src, dst, sem)


def _dispatch_kernel(ends_ref, padded_ref, pos_ref, h_ref, sorted_ref, zero_ref, stage_ref, sem, zsem, *,
                     n_tiles, n_steps):
    n = TM_DISP
    i = pl.program_id(0)
    slot = lax.rem(i, 2)

    @pl.when(i == 0)
    def _():
        zero_ref[...] = jnp.zeros_like(zero_ref)

        def fill_tile(start_row, wait):
            start = pl.multiple_of(start_row * PSUB, PSUB)
            fill = _row_copy(zero_ref, sorted_ref.at[pl.ds(start, TM_EXP * PSUB)], zsem)
            if wait:
                fill.wait()
            else:
                fill.start()

        for wait in (False, True):
            for c in range(N_CLASSES):
                @pl.when(padded_ref[c] > 0)
                def _():
                    fill_tile(ends_ref[c] - TM_EXP, wait)
            for j in range(n_tiles - N_CLASSES, n_tiles):
                @pl.when(j * TM_EXP >= ends_ref[N_CLASSES - 1])
                def _():
                    fill_tile(j * TM_EXP, wait)

    stage_ref[slot] = h_ref[...]

    def issue(r, carry):
        for k in range(2):
            tok = 2 * r + k
            src = stage_ref.at[slot, pl.ds(pl.multiple_of(tok * PSUB, PSUB), PSUB)]
            dst = pl.multiple_of(pos_ref[0, 0, tok] * PSUB, PSUB)
            _row_copy(src, sorted_ref.at[pl.ds(dst, PSUB)], sem.at[slot]).start(priority=k)
        return carry

    lax.fori_loop(0, n // 2, issue, 0, unroll=8)

    def drain(s):
        _row_copy(stage_ref.at[s], sorted_ref.at[pl.ds(0, n * PSUB)], sem.at[s]).wait()

    @pl.when(i > 0)
    def _():
        drain(1 - slot)

    @pl.when(i == n_steps - 1)
    def _():
        drain(slot)


def _dispatch(h, pos3, ends, padded, n_rows):
    t = h.shape[0] // PSUB
    n_steps = t // TM_DISP
    return pl.pallas_call(
        functools.partial(_dispatch_kernel, n_tiles=n_rows // TM_EXP, n_steps=n_steps),
        grid_spec=pltpu.PrefetchScalarGridSpec(
            num_scalar_prefetch=2, grid=(n_steps,),
            in_specs=[pl.BlockSpec((1, 1, TM_DISP), lambda i, *m: (i, 0, 0), memory_space=pltpu.SMEM),
                      pl.BlockSpec((TM_DISP * PSUB, LANES), lambda i, *m: (i, 0))],
            out_specs=pl.BlockSpec(memory_space=pl.ANY),
            scratch_shapes=[pltpu.VMEM((TM_EXP * PSUB, LANES), h.dtype),
                            pltpu.VMEM((2, TM_DISP * PSUB, LANES), h.dtype),
                            pltpu.SemaphoreType.DMA((2,)), pltpu.SemaphoreType.DMA(())]),
        out_shape=jax.ShapeDtypeStruct((n_rows * PSUB, LANES), h.dtype),
        compiler_params=_cparams(1),
        name="dispatch",
    )(ends, padded, pos3, h)


def _expert_kernel(ea_ref, eb_ref, tv_ref, h_ref, *refs):
    i = pl.program_id(0)
    w_refs, y_ref, wb_refs = refs[:6], refs[6], refs[7:]
    prev = jnp.maximum(i - 1, 0)

    for slot, e_ref in enumerate((ea_ref, eb_ref)):
        @pl.when((i == 0) | (e_ref[i] != e_ref[prev]))
        def _():
            for j in range(3):
                wb_refs[3 * slot + j][...] = w_refs[3 * slot + j][0, 0].astype(BF16)

    @pl.when(tv_ref[i] == 1)
    def _():
        hb = _unpack_pairs(_load_token_major(h_ref, TM_EXP)).astype(BF16)
        for slot in range(2):
            wg, wu, wd = (wb_refs[3 * slot + j][...] for j in range(3))
            he = (_silu(_dot(hb, wg)) * _dot(hb, wu)).astype(BF16)
            y = _pack_pairs(_dot(he, wd))
            for s in range(PSUB):
                y_ref[pl.ds(slot * PSUB + s, TM_EXP, stride=2 * PSUB), :] = y[:, s * LANES:(s + 1) * LANES]

    @pl.when(tv_ref[i] == 0)
    def _():
        y_ref[...] = jnp.zeros_like(y_ref)


def _experts(hs, tile_ea, tile_eb, tile_valid, wg, wu, wd, layer):
    d = wg.shape[2]
    r = hs.shape[0] // PSUB
    sel_a = lambda i, ea, eb, tv: (layer, ea[i], 0, 0)
    sel_b = lambda i, ea, eb, tv: (layer, eb[i], 0, 0)
    w_specs = []
    for sel in (sel_a, sel_b):
        w_specs += [pl.BlockSpec((1, 1, d, D_EXPERT), sel), pl.BlockSpec((1, 1, d, D_EXPERT), sel),
                    pl.BlockSpec((1, 1, D_EXPERT, d), sel)]
    wb = [pltpu.VMEM((d, D_EXPERT), BF16), pltpu.VMEM((d, D_EXPERT), BF16), pltpu.VMEM((D_EXPERT, d), BF16)]
    return pl.pallas_call(
        _expert_kernel,
        grid_spec=pltpu.PrefetchScalarGridSpec(
            num_scalar_prefetch=3, grid=(r // TM_EXP,),
            in_specs=[pl.BlockSpec((TM_EXP * PSUB, LANES), lambda i, ea, eb, tv: (i * tv[i], 0))] + w_specs,
            out_specs=pl.BlockSpec((TM_EXP * 2 * PSUB, LANES), lambda i, ea, eb, tv: (i, 0)),
            scratch_shapes=wb + wb),
        out_shape=jax.ShapeDtypeStruct((r * 2 * PSUB, LANES), U32),
        compiler_params=_cparams(1),
        name="experts",
    )(tile_ea, tile_eb, tile_valid, hs, wg, wu, wd, wg, wu, wd)


def _combine_kernel(cr_ref, pos_ref, posn_ref, ys_ref, wts_ref, x_ref, npost_ref, g2_ref, *rest, n_first, n_steps):
    n = TM_COMB
    buf_ref, sem = rest[-2:]
    outs = rest[:-2]
    i = pl.program_id(0)
    wide = 2 * PSUB

    def gather(p_ref, slot):
        def issue(r, carry):
            for k in range(2):
                tok = 2 * r + k
                src = pl.multiple_of(p_ref[0, 0, tok] * wide, wide)
                dst = pl.multiple_of(tok * wide, wide)
                _row_copy(ys_ref.at[pl.ds(src, wide)], buf_ref.at[slot, pl.ds(dst, wide)],
                          sem.at[slot]).start(priority=k)
            return carry

        lax.fori_loop(0, n // 2, issue, 0, unroll=8)

    slot = lax.rem(i, 2)

    @pl.when(i == 0)
    def _():
        gather(pos_ref, 0)

    @pl.when(i + 1 < n_steps)
    def _():
        gather(posn_ref, 1 - slot)

    _row_copy(ys_ref.at[pl.ds(0, n * wide)], buf_ref.at[slot], sem.at[slot]).wait()

    w = wts_ref[...]
    cur = buf_ref.at[slot]
    y_lo = _unpack_pairs(jnp.concatenate([cur[pl.ds(s, n, stride=wide), :] for s in range(PSUB)], axis=1))
    y_hi = _unpack_pairs(jnp.concatenate([cur[pl.ds(PSUB + s, n, stride=wide), :] for s in range(PSUB)], axis=1))
    y = y_lo * w[:, 0:1] + y_hi * w[:, 1:2]
    res = x_ref[...] + g2_ref[0] * _rms(y, npost_ref[...])
    if n_first is None:
        outs[0][...] = res
    else:
        @pl.when(i < n_first)
        def _():
            outs[0][...] = res

        @pl.when(i >= n_first)
        def _():
            outs[1][...] = res


def _combine(lay, ys3, pos3, wts, x, n_post, g2, split):
    t, d = x.shape
    tm = TM_COMB
    cr = lay.cond_rows(tm)
    row = lambda i, cr: (i, 0)
    const = lambda i, cr: (0, 0)
    n_steps = t // tm
    in_specs = [pl.BlockSpec((1, 1, tm), lambda i, cr: (i, 0, 0), memory_space=pltpu.SMEM),
                pl.BlockSpec((1, 1, tm), lambda i, cr: (jnp.minimum(i + 1, n_steps - 1), 0, 0),
                             memory_space=pltpu.SMEM),
                pl.BlockSpec(memory_space=pl.ANY),
                pl.BlockSpec((tm, 2), row),
                pl.BlockSpec((tm, d), row),
                pl.BlockSpec((1, d), const),
                pl.BlockSpec((1, 1, d), lambda i, cr: (cr[i], 0, 0))]
    if split:
        n_first = lay.t_ctx // tm
        out_specs = [pl.BlockSpec((tm, d), lambda i, cr: (jnp.minimum(i, n_first - 1), 0)),
                     pl.BlockSpec((tm, d), lambda i, cr: (jnp.maximum(i - n_first, 0), 0))]
        out_shape = [jax.ShapeDtypeStruct((lay.t_ctx, d), F32), jax.ShapeDtypeStruct((t - lay.t_ctx, d), F32)]
    else:
        n_first = None
        out_specs = [pl.BlockSpec((tm, d), row)]
        out_shape = [jax.ShapeDtypeStruct((t, d), F32)]
    return pl.pallas_call(
        functools.partial(_combine_kernel, n_first=n_first, n_steps=n_steps),
        grid_spec=pltpu.PrefetchScalarGridSpec(
            num_scalar_prefetch=1, grid=(n_steps,), in_specs=in_specs, out_specs=out_specs,
            scratch_shapes=[pltpu.VMEM((2, tm * 2 * PSUB, LANES), U32), pltpu.SemaphoreType.DMA((2,))]),
        out_shape=out_shape,
        compiler_params=_cparams(1),
        name="combine",
    )(cr, pos3, pos3, ys3, wts, x, n_post.reshape(1, d), g2)


def _moe(lay, x, g_pre, sc, sh, g2, n_post, rwt, rbias, wg, wu, wd, layer, split):
    t, d = x.shape
    h, route, wts, cnt = _router(lay, x, g_pre, sc, sh, rwt, rbias)
    cls, rank = route[0], route[1]
    counts = cnt[:N_CLASSES, 0]
    padded = ((counts + TM_EXP - 1) // TM_EXP) * TM_EXP
    ends = jnp.cumsum(padded).astype(I32)
    offs = ends - padded
    cids = jnp.arange(N_CLASSES, dtype=I32)[:, None]
    pos = jnp.sum(jnp.where(cls[None] == cids, offs[:, None], 0), axis=0) + rank
    pos3 = pos.reshape(t // TM_COMB, 1, TM_COMB).astype(I32)
    n_rows = t + N_CLASSES * TM_EXP
    n_tiles = n_rows // TM_EXP
    starts = jnp.arange(n_tiles, dtype=I32) * TM_EXP
    tile_cls = jnp.minimum(jnp.sum((starts[:, None] >= ends[None, :]).astype(I32), axis=1), N_CLASSES - 1)
    tile_valid = (starts < ends[-1]).astype(I32)
    grp, pair = tile_cls // N_PAIRS, tile_cls % N_PAIRS
    pids = jnp.arange(N_PAIRS, dtype=I32)[None, :]
    lo = jnp.sum(jnp.where(pair[:, None] == pids, jnp.asarray(PAIR_LO, I32)[None, :], 0), axis=1)
    hi = jnp.sum(jnp.where(pair[:, None] == pids, jnp.asarray(PAIR_HI, I32)[None, :], 0), axis=1)
    tile_ea = (grp * EXP_PER_GROUP + lo).astype(I32)
    tile_eb = (grp * EXP_PER_GROUP + hi).astype(I32)
    hs = _dispatch(h, pos.reshape(t // TM_DISP, 1, TM_DISP).astype(I32), ends, padded.astype(I32), n_rows)
    ys = _experts(hs, tile_ea, tile_eb, tile_valid, wg, wu, wd, layer)
    return _combine(lay, ys, pos3, wts.T, x, n_post, g2, split)


def _rope_tables(ld):
    rows = ld // GRID_W
    r = jnp.repeat(jnp.arange(rows, dtype=F32), GRID_W)
    col = jnp.tile(jnp.arange(GRID_W, dtype=F32), rows)
    quarter = RET_DK // 4
    inv = ROPE_BASE ** (-jnp.arange(quarter, dtype=F32) / quarter)
    ang = jnp.concatenate([r[:, None] * inv, col[:, None] * inv], axis=-1)
    cos, sin = jnp.cos(ang), jnp.sin(ang)
    c = jnp.concatenate([cos, cos], axis=-1)
    s = jnp.concatenate([-sin, sin], axis=-1)
    ident_c = jnp.ones((TOK, RET_DK), F32)
    ident_s = jnp.zeros((TOK, RET_DK), F32)
    return jnp.concatenate([c, ident_c], axis=0), jnp.concatenate([s, ident_s], axis=0)


def kernel(x_prompt, x_sample, state_ret, state_gdn, c, c_ctx, w_mod, b_mod, norm_mix_pre, norm_mix_post,
           norm_ffn_pre, norm_ffn_post, ev_w_in, ev_conv_w, ev_conv_ln_g, ev_conv_ln_b, ev_ret_decay, ev_w_out,
           od_w_in, od_conv_w, od_a_log, od_dt_bias, od_norm_w, od_w_out, router_w, router_bias,
           moe_w_gate, moe_w_up, moe_w_down):
    bc, lc, d = x_prompt.shape
    bd, ld, _ = x_sample.shape
    depth = w_mod.shape[0]
    lay = _Layout(bc, lc, bd, ld)
    t = lay.t

    x = (x_prompt.reshape(bc * lc, d), x_sample.reshape(bd * ld, d))
    cond = jnp.zeros((COND_PAD, d), F32).at[0].set(c_ctx).at[1:1 + bd].set(c)
    mod = _modulation(cond, w_mod, b_mod)
    mod = mod.reshape(depth, COND_PAD, N_MOD, 1, d).transpose(0, 2, 1, 3, 4)

    rope_c, rope_s = _rope_tables(ld)
    rwt = router_w.T
    ret_states, gdn_states = [], []
    for layer in range(depth):
        sh1, sc1, g1, sh2, sc2, g2 = (mod[layer, j] for j in range(N_MOD))
        i = layer // 2
        if layer % 2 == 0:
            (proj,) = _inproj(lay, x, norm_mix_pre[layer], sc1, sh1, [ev_w_in[i].astype(BF16)], [BF16])
            log_gamma = -jnp.exp(ev_ret_decay[i].astype(F32))
            o_f, o_b, s_out = _retention(lay, proj, log_gamma, rope_c, rope_s, state_ret[:, i])
            ret_states.append(s_out)
            cw = jnp.zeros((32, CONV_CH), F32).at[:CONV_W].set(ev_conv_w[i])
            x = _even_post(lay, proj, o_f, o_b, x, cw, ev_conv_ln_g[i], ev_conv_ln_b[i],
                           ev_w_out[i].astype(BF16), norm_mix_post[layer], g1)
        else:
            w_in = od_w_in[i]
            n_qkv = 2 * GDN_KW + GDN_VW
            n_main = n_qkv + GDN_VW
            w_ab = jnp.zeros((d, LANES), F32).at[:, :4 * GDN_HEADS].set(w_in[:, n_main:])
            cw = jnp.zeros((8, n_qkv), F32).at[:SHORT_W].set(od_conv_w[i])
            alog_row = jnp.zeros((1, LANES), F32).at[0, :2 * GDN_HEADS].set(od_a_log[i].reshape(-1))
            dtb_row = jnp.zeros((1, LANES), F32).at[0, :2 * GDN_HEADS].set(od_dt_bias[i].reshape(-1))
            qkv, z, gb = _odd_in(lay, x, norm_mix_pre[layer], sc1, sh1, w_in[:, :n_qkv].astype(BF16),
                                 w_in[:, n_qkv:n_main].astype(BF16), w_ab.astype(BF16), cw, alog_row, dtb_row)
            o_f, o_b, s_out = _gdn(lay, qkv, gb, state_gdn[:, i])
            gdn_states.append(s_out)
            x = _odd_post(lay, o_f, o_b, z, x, od_norm_w[i], od_w_out[i].astype(BF16),
                          norm_mix_post[layer], g1)
        outs = _moe(lay, x, norm_ffn_pre[layer], sc2, sh2, g2, norm_ffn_post[layer], rwt, router_bias,
                    moe_w_gate, moe_w_up, moe_w_down, layer, split=(layer == depth - 1))
        x = outs[0]

    y_prompt = outs[0].reshape(bc, lc, d)
    y_sample = outs[1].reshape(bd, ld, d)
    new_ret = jnp.stack(ret_states, axis=1)
    new_gdn = jnp.stack(gdn_states, axis=1)
    return y_prompt, y_sample, new_ret.astype(x_prompt.dtype), new_gdn.astype(x_prompt.dtype)
```
